```python
import jax, jax.numpy as jnp
from jax import lax
import numpy as np

D_MODEL = 2048
BATCH = 4
SEQ = 2048
DEPTH = 4

GRID_W = 64
CTX_LEN = 256
N_MOD = 9
D_FF = 5632
EPS = 1e-6
N_BR = 4
W_BR = D_MODEL // 4

A_HD = 64
A_HEADS = W_BR // A_HD
A_W_RANK = 64
A_A_RANK = 64
A_G_RANK = 128
A_LN_EPS = 64e-5
A_COLS = 3 * W_BR + 2 * A_W_RANK + 2 * A_A_RANK + A_G_RANK

B_CHUNK = 128
B_GD = 128
B_GROUPS = W_BR // B_GD
B_COLS = 2 * W_BR

C_HEADS = 4
C_HD = W_BR // C_HEADS
C_CONV = 3
C_CHUNK = 64
C_COLS = 4 * W_BR + 4 * C_HEADS

D_HEADS = 4
D_DV = W_BR // D_HEADS
D_DK = D_DV // 2
D_RANK = 16
D_TAU = 16.0
D_CHUNK = 64
D_COLS = 2 * D_HEADS * D_DK + 2 * W_BR + 2 * D_RANK

D_IN = A_COLS + B_COLS + C_COLS + D_COLS + N_BR * D_MODEL

kernel_name = "hybrid_rwkv7_gmlp_mlstm_gla_dit"


def _split(z, sizes):
    idx = [int(i) for i in np.cumsum(sizes)[:-1]]
    return jnp.split(z, idx, axis=-1)


def _rms(x, g):
    xf = x.astype(jnp.float32)
    y = xf * lax.rsqrt(jnp.mean(xf * xf, axis=-1, keepdims=True) + EPS)
    return (y * g.astype(jnp.float32)).astype(x.dtype)


def _head_ln(z, g, b, eps):
    H, d = z.shape[-2], z.shape[-1]
    zf = z.astype(jnp.float32)
    mu = jnp.mean(zf, axis=-1, keepdims=True)
    zc = zf - mu
    y = zc * lax.rsqrt(jnp.mean(zc * zc, axis=-1, keepdims=True) + eps) * g.reshape(H, d)
    if b is not None:
        y = y + b.reshape(H, d)
    return y


def _head_rms(z, g, eps):
    H, d = z.shape[-2], z.shape[-1]
    zf = z.astype(jnp.float32)
    return zf * lax.rsqrt(jnp.mean(zf * zf, axis=-1, keepdims=True) + eps) * g.reshape(H, d)


def _modulate(hn, shift, scale, L):
    B = hn.shape[0]
    return jnp.concatenate([hn[:, :L] * (1.0 + scale[B:, None]) + shift[B:, None],
                            hn[:, L:] * (1.0 + scale[:B, None]) + shift[:B, None]], axis=1)


def _gated_add(h, y, gate, L):
    B = h.shape[0]
    return h + jnp.concatenate([y[:, :L] * gate[B:, None], y[:, L:] * gate[:B, None]], axis=1)


def _swiglu(x, w1, w3, w2):
    return (jax.nn.silu(x @ w1) * (x @ w3)) @ w2


def _shift_grid(z):
    B, T, C = z.shape
    rows = T // GRID_W
    zr = z.reshape(B, rows, GRID_W, C // 4, 4)
    left = jnp.pad(zr[..., 0], ((0, 0), (0, 0), (1, 0), (0, 0)))[:, :, :-1]
    right = jnp.pad(zr[..., 1], ((0, 0), (0, 0), (0, 1), (0, 0)))[:, :, 1:]
    up = jnp.pad(zr[..., 2], ((0, 0), (1, 0), (0, 0), (0, 0)))[:, :-1]
    down = jnp.pad(zr[..., 3], ((0, 0), (0, 1), (0, 0), (0, 0)))[:, 1:]
    return jnp.stack([left, right, up, down], axis=-1).reshape(B, T, C)


def _shift_seq(z):
    B, T, C = z.shape
    zr = z.reshape(B, T, C // 2, 2)
    prev = jnp.pad(zr[..., 0], ((0, 0), (1, 0), (0, 0)))[:, :-1]
    nxt = jnp.pad(zr[..., 1], ((0, 0), (0, 1), (0, 0)))[:, 1:]
    return jnp.stack([prev, nxt], axis=-1).reshape(B, T, C)


def _conv_centred(z, w, b):
    C = z.shape[-1]
    pad = w.shape[0] // 2
    y = lax.conv_general_dilated(z, w[:, None, :].astype(z.dtype), window_strides=(1,),
                                 padding=((pad, pad),), dimension_numbers=("NWC", "WIO", "NWC"),
                                 feature_group_count=C)
    return y + b


def _to_chunks(z, Lc):
    B, T, H, d = z.shape
    return jnp.moveaxis(z.reshape(B, T // Lc, Lc, H, d), (1, 3), (0, 2))


def _from_chunks(z):
    nc, B, H, Lc, d = z.shape
    return jnp.moveaxis(z, (0, 2), (1, 3)).reshape(B, nc * Lc, H, d)


def _rwkv7_scan(S0, xs, reverse):
    def step(S, inp):
        r, w, kk, b, kt, v = inp
        sa = jnp.einsum("bhvk,bhk->bhv", S, kk)
        S = S * w[:, :, None, :] - sa[..., None] * b[:, :, None, :] + v[..., None] * kt[:, :, None, :]
        return S, jnp.einsum("bhvk,bhk->bhv", S, r)
    return lax.scan(step, S0, xs, reverse=reverse)


def _rwkv7_branch(pa, L, mu, w0, wup, a0, aup, gup, k_k, k_a, r_k, ln_g, ln_b):
    dt = pa.dtype
    pa = pa.astype(jnp.float32)
    pc, px = pa[:, :L], pa[:, L:]
    za = jnp.concatenate([pc + (_shift_seq(pc) - pc) * mu, px + (_shift_grid(px) - px) * mu], axis=1)
    r, k, v, wd_f, wd_b, ad_f, ad_b, gd = _split(
        za, [W_BR, W_BR, W_BR, A_W_RANK, A_W_RANK, A_A_RANK, A_A_RANK, A_G_RANK])
    B, T, _ = za.shape
    hd = lambda z: z.reshape(B, T, A_HEADS, A_HD)
    tm = lambda z: jnp.moveaxis(z, 1, 0)
    r_h, k_h, v_h = hd(r), hd(k), hd(v)
    kk = hd(k * k_k)
    kk = kk * lax.rsqrt(jnp.maximum(jnp.sum(kk * kk, axis=-1, keepdims=True), 1e-24))
    ys = []
    for d, (wd, ad) in enumerate(((wd_f, ad_f), (wd_b, ad_b))):
        w = jnp.exp(-jnp.exp(-jax.nn.softplus(-(w0[d] + jnp.tanh(wd) @ wup[d])) - 0.5))
        a = jax.nn.sigmoid(a0[d] + ad @ aup[d])
        kt = k * (1.0 + (a - 1.0) * k_a)
        xs = tuple(tm(z) for z in (r_h, hd(w), kk, kk * hd(a), hd(kt), v_h))
        S0 = jnp.zeros((B, A_HEADS, A_HD, A_HD), jnp.float32)
        S_ctx, y_ctx = _rwkv7_scan(S0, tuple(z[:L] for z in xs), d == 1)
        _, y_lat = _rwkv7_scan(S_ctx, tuple(z[L:] for z in xs), d == 1)
        ys.append(jnp.concatenate([y_ctx, y_lat], axis=0))
    y = _head_ln(jnp.moveaxis(ys[0] + ys[1], 0, 1), ln_g, ln_b, A_LN_EPS)
    bonus = jnp.sum(r_h * k_h * r_k, axis=-1, keepdims=True) * v_h
    g = jax.nn.sigmoid(gd) @ gup
    return ((y + bonus).reshape(B, T, W_BR) * g).astype(dt)


def _chunk_mix(vs, ws, bs):
    B, T, _ = vs.shape
    vg = vs.reshape(B, T // B_CHUNK, B_CHUNK, B_GROUPS, B_GD)
    s = jnp.einsum("gpq,bnqgc->bnpgc", ws, vg) + bs.T[None, None, :, :, None]
    return s.reshape(B, T, W_BR)


def _gmlp_branch(pb, L, ws, bs, ln_g, ln_b):
    z = jax.nn.gelu(pb)
    u, v = _split(z, [W_BR, W_BR])
    vf = v.astype(jnp.float32)
    mu = jnp.mean(vf, axis=-1, keepdims=True)
    vc = vf - mu
    vn = (vc * lax.rsqrt(jnp.mean(vc * vc, axis=-1, keepdims=True) + 1e-5) * ln_g + ln_b).astype(pb.dtype)
    s = jnp.concatenate([_chunk_mix(vn[:, :L], ws, bs), _chunk_mix(vn[:, L:], ws, bs)], axis=1)
    return (u * s).astype(pb.dtype)


def _mlstm_chunked(state, q, k, v, ig, lf, reverse):
    if reverse:
        q, k, v, ig, lf = [jnp.flip(z, 1) for z in (q, k, v, ig, lf)]
    tril = jnp.tril(jnp.ones((C_CHUNK, C_CHUNK), bool))

    def body(carry, inp):
        Cm, n, m = carry
        qc, kc, vc, ic, fc = inp
        ic, fc = ic[..., 0], fc[..., 0]
        b = jnp.cumsum(fc, axis=-1)
        Dlog = jnp.where(tril, b[..., :, None] - b[..., None, :] + ic[..., None, :], -jnp.inf)
        inter = b + m[..., None]
        m_t = jnp.maximum(inter, jnp.max(Dlog, axis=-1))
        Dw = jnp.exp(Dlog - m_t[..., None])
        iw = jnp.exp(inter - m_t)
        s = jnp.einsum("bhtd,bhsd->bhts", qc, kc) * Dw
        num = jnp.einsum("bhts,bhsv->bhtv", s, vc) + iw[..., None] * jnp.einsum("bhvk,bhtk->bhtv", Cm, qc)
        den = jnp.sum(s, axis=-1) + iw * jnp.einsum("bhk,bhtk->bht", n, qc)
        den = jnp.maximum(jnp.abs(den), jnp.exp(-m_t))
        h = num / den[..., None]
        bL = b[..., -1]
        gl = bL[..., None] - b + ic
        m_new = jnp.maximum(bL + m, jnp.max(gl, axis=-1))
        sw = jnp.exp(gl - m_new[..., None])
        dec = jnp.exp(bL + m - m_new)
        Cm = dec[..., None, None] * Cm + jnp.einsum("bhs,bhsv,bhsk->bhvk", sw, vc, kc)
        n = dec[..., None] * n + jnp.einsum("bhs,bhsk->bhk", sw, kc)
        return (Cm, n, m_new), h

    xs = (_to_chunks(q, C_CHUNK), _to_chunks(k, C_CHUNK), _to_chunks(v, C_CHUNK),
          _to_chunks(ig[..., None], C_CHUNK), _to_chunks(lf[..., None], C_CHUNK))
    state, h = lax.scan(body, state, xs)
    h = _from_chunks(h)
    if reverse:
        h = jnp.flip(h, 1)
    return h, state


def _mlstm_branch(pmc, L, conv_w, conv_b, gate_b, ln_g):
    dt = pmc.dtype
    pmc = pmc.astype(jnp.float32)
    qk, v, o, gts = _split(pmc, [2 * W_BR, W_BR, W_BR, 4 * C_HEADS])
    qk = jax.nn.silu(jnp.concatenate([_conv_centred(qk[:, :L], conv_w, conv_b),
                                      _conv_centred(qk[:, L:], conv_w, conv_b)], axis=1))
    q, k = _split(qk, [W_BR, W_BR])
    B, T, _ = pmc.shape
    q = q.reshape(B, T, C_HEADS, C_HD)
    k = k.reshape(B, T, C_HEADS, C_HD) * C_HD ** -0.5
    v = v.reshape(B, T, C_HEADS, C_HD)
    gts = gts.reshape(B, T, 2, 2, C_HEADS) + gate_b
    hs = []
    for d in range(2):
        ig = gts[:, :, d, 0]
        lf = jax.nn.log_sigmoid(gts[:, :, d, 1])
        st0 = (jnp.zeros((B, C_HEADS, C_HD, C_HD), jnp.float32),
               jnp.zeros((B, C_HEADS, C_HD), jnp.float32),
               jnp.zeros((B, C_HEADS), jnp.float32))
        h_c, st = _mlstm_chunked(st0, q[:, :L], k[:, :L], v[:, :L], ig[:, :L], lf[:, :L], d == 1)
        h_x, _ = _mlstm_chunked(st, q[:, L:], k[:, L:], v[:, L:], ig[:, L:], lf[:, L:], d == 1)
        hs.append(jnp.concatenate([h_c, h_x], axis=1))
    h = _head_ln(hs[0] + hs[1], ln_g, None, 1e-5)
    return (h.reshape(B, T, W_BR) * jax.nn.sigmoid(o)).astype(dt)


def _gla_chunked(S, q, k, v, la, reverse):
    if reverse:
        q, k, v, la = [jnp.flip(z, 1) for z in (q, k, v, la)]
    tril = jnp.tril(jnp.ones((D_CHUNK, D_CHUNK), bool))

    def body(S, inp):
        qc, kc, vc, lac = inp
        bc = jnp.cumsum(lac, axis=2)
        rel = jnp.where(tril[:, :, None], bc[:, :, :, None, :] - bc[:, :, None, :, :], -jnp.inf)
        A = jnp.einsum("bhtk,bhsk,bhtsk->bhts", qc, kc, jnp.exp(rel))
        o = jnp.einsum("bhts,bhsv->bhtv", A, vc) + jnp.einsum("bhtk,bhkv->bhtv", qc * jnp.exp(bc), S)
        bL = bc[:, :, -1]
        S = jnp.exp(bL)[..., None] * S + jnp.einsum("bhsk,bhsv->bhkv", kc * jnp.exp(bL[:, :, None] - bc), vc)
        return S, o

    xs = (_to_chunks(q, D_CHUNK), _to_chunks(k, D_CHUNK), _to_chunks(v, D_CHUNK), _to_chunks(la, D_CHUNK))
    S, o = lax.scan(body, S, xs)
    o = _from_chunks(o)
    if reverse:
        o = jnp.flip(o, 1)
    return o, S


def _gla_branch(pd, L, aup, ab, ln_g):
    dt = pd.dtype
    pd = pd.astype(jnp.float32)
    q, k, v, g, ad_f, ad_b = _split(pd, [D_HEADS * D_DK, D_HEADS * D_DK, W_BR, W_BR, D_RANK, D_RANK])
    B, T, _ = pd.shape
    q = q.reshape(B, T, D_HEADS, D_DK) * D_DK ** -0.5
    k = k.reshape(B, T, D_HEADS, D_DK)
    v = v.reshape(B, T, D_HEADS, D_DV)
    outs = []
    for d, ad in enumerate((ad_f, ad_b)):
        la = (jax.nn.log_sigmoid(ad @ aup[d] + ab[d]) / D_TAU).reshape(B, T, D_HEADS, D_DK)
        S0 = jnp.zeros((B, D_HEADS, D_DK, D_DV), jnp.float32)
        o_c, S_c = _gla_chunked(S0, q[:, :L], k[:, :L], v[:, :L], la[:, :L], d == 1)
        o_x, _ = _gla_chunked(S_c, q[:, L:], k[:, L:], v[:, L:], la[:, L:], d == 1)
        outs.append(jnp.concatenate([o_c, o_x], axis=1))
    o = _head_rms(outs[0] + outs[1], ln_g, 1e-6)
    return (o.reshape(B, T, W_BR) * jax.nn.silu(g)).astype(dt)


def _merge(hs, pg, br_w, out_w):
    B, T, _ = pg.shape
    gates = jax.nn.sigmoid(pg).reshape(B, T, N_BR, D_MODEL)
    y = gates[:, :, 0] * (hs[0] @ br_w[0])
    for n in range(1, N_BR):
        y = y + gates[:, :, n] * (hs[n] @ br_w[n])
    return y @ out_w


def setup_inputs(seed: int = 0) -> dict:
    key = jax.random.key(seed)
    ks = iter(jax.random.split(key, 40))
    f32 = jnp.float32

    def nrm(shape, s):
        return jax.random.normal(next(ks), shape, f32) * s

    Dm = D_MODEL
    x = nrm((BATCH, SEQ, Dm), 1.0)
    c = nrm((BATCH, Dm), 1.0)
    ctx = nrm((BATCH, CTX_LEN, Dm), 1.0)
    c_ctx = nrm((Dm,), 1.0)
    ada_w = nrm((DEPTH, Dm, N_MOD * Dm), 0.5 * Dm ** -0.5)
    ada_b = nrm((DEPTH, N_MOD * Dm), 0.02)
    norm_g = 1.0 + nrm((DEPTH, 3, Dm), 0.05)
    ffn_w1 = nrm((DEPTH, 2, Dm, D_FF), Dm ** -0.5)
    ffn_w3 = nrm((DEPTH, 2, Dm, D_FF), Dm ** -0.5)
    ffn_w2 = nrm((DEPTH, 2, D_FF, Dm), D_FF ** -0.5)
    in_w = nrm((DEPTH, Dm, D_IN), Dm ** -0.5)
    in_b = nrm((DEPTH, D_IN), 0.02)
    a_mu = jax.random.uniform(next(ks), (DEPTH, A_COLS), f32)
    a_w0 = jnp.linspace(-6.5, -1.5, W_BR, dtype=f32) + nrm((DEPTH, 2, W_BR), 0.1)
    a_wup = nrm((DEPTH, 2, A_W_RANK, W_BR), 0.1)
    a_a0 = nrm((DEPTH, 2, W_BR), 0.1)
    a_aup = nrm((DEPTH, 2, A_A_RANK, W_BR), 0.1)
    a_gup = nrm((DEPTH, A_G_RANK, W_BR), A_G_RANK ** -0.5)
    a_kk = 0.85 + nrm((DEPTH, W_BR), 0.05)
    a_ka = 1.0 + nrm((DEPTH, W_BR), 0.05)
    a_rk = nrm((DEPTH, A_HEADS, A_HD), 0.1)
    a_ln_g = 1.0 + nrm((DEPTH, W_BR), 0.05)
    a_ln_b = nrm((DEPTH, W_BR), 0.02)
    b_ws = nrm((DEPTH, B_GROUPS, B_CHUNK, B_CHUNK), 0.5 * B_CHUNK ** -0.5)
    b_bs = 1.0 + nrm((DEPTH, B_GROUPS, B_CHUNK), 0.1)
    b_ln_g = 1.0 + nrm((DEPTH, W_BR), 0.05)
    b_ln_b = nrm((DEPTH, W_BR), 0.02)
    c_conv_w = nrm((DEPTH, C_CONV, 2 * W_BR), C_CONV ** -0.5)
    c_conv_b = nrm((DEPTH, 2 * W_BR), 0.02)
    c_gate_b = jnp.stack([nrm((DEPTH, 2, C_HEADS), 0.1),
                          jnp.linspace(3.0, 6.0, C_HEADS, dtype=f32) + nrm((DEPTH, 2, C_HEADS), 0.1)],
                         axis=2)
    c_ln_g = 1.0 + nrm((DEPTH, W_BR), 0.05)
    d_aup = nrm((DEPTH, 2, D_RANK, D_HEADS * D_DK), D_RANK ** -0.5)
    d_ab = nrm((DEPTH, 2, D_HEADS * D_DK), 0.1)
    d_ln_g = 1.0 + nrm((DEPTH, W_BR), 0.05)
    br_w = nrm((DEPTH, N_BR, W_BR, Dm), W_BR ** -0.5)
    out_w = nrm((DEPTH, Dm, Dm), Dm ** -0.5)
    final_g = 1.0 + nrm((Dm,), 0.05)
    return {"x": x, "c": c, "ctx": ctx, "c_ctx": c_ctx, "ada_w": ada_w, "ada_b": ada_b,
            "norm_g": norm_g, "ffn_w1": ffn_w1, "ffn_w3": ffn_w3, "ffn_w2": ffn_w2,
            "in_w": in_w, "in_b": in_b, "a_mu": a_mu, "a_w0": a_w0, "a_wup": a_wup, "a_a0": a_a0,
            "a_aup": a_aup, "a_gup": a_gup, "a_kk": a_kk, "a_ka": a_ka, "a_rk": a_rk,
            "a_ln_g": a_ln_g, "a_ln_b": a_ln_b, "b_ws": b_ws, "b_bs": b_bs, "b_ln_g": b_ln_g,
            "b_ln_b": b_ln_b, "c_conv_w": c_conv_w, "c_conv_b": c_conv_b, "c_gate_b": c_gate_b,
            "c_ln_g": c_ln_g, "d_aup": d_aup, "d_ab": d_ab, "d_ln_g": d_ln_g, "br_w": br_w,
            "out_w": out_w, "final_g": final_g}


def reference(x, c, ctx, c_ctx, ada_w, ada_b, norm_g, ffn_w1, ffn_w3, ffn_w2, in_w, in_b,
              a_mu, a_w0, a_wup, a_a0, a_aup, a_gup, a_kk, a_ka, a_rk, a_ln_g, a_ln_b,
              b_ws, b_bs, b_ln_g, b_ln_b, c_conv_w, c_conv_b, c_gate_b, c_ln_g,
              d_aup, d_ab, d_ln_g, br_w, out_w, final_g):
    B = x.shape[0]
    L = ctx.shape[1]
    h = jnp.concatenate([ctx, x], axis=1)
    cond = jax.nn.silu(jnp.concatenate([c, c_ctx[None]], axis=0))
    for i in range(DEPTH):
        mod = (cond @ ada_w[i] + ada_b[i]).reshape(B + 1, N_MOD, D_MODEL)
        hn = _modulate(_rms(h, norm_g[i, 0]), mod[:, 0], mod[:, 1], L)
        h = _gated_add(h, 0.5 * _swiglu(hn, ffn_w1[i, 0], ffn_w3[i, 0], ffn_w2[i, 0]), mod[:, 2], L)
        hn = _modulate(_rms(h, norm_g[i, 1]), mod[:, 3], mod[:, 4], L)
        p = hn @ in_w[i] + in_b[i]
        pa, pb, pmc, pd, pg = _split(p, [A_COLS, B_COLS, C_COLS, D_COLS, N_BR * D_MODEL])
        hs = [_rwkv7_branch(pa, L, a_mu[i], a_w0[i], a_wup[i], a_a0[i], a_aup[i], a_gup[i],
                            a_kk[i], a_ka[i], a_rk[i], a_ln_g[i], a_ln_b[i]),
              _gmlp_branch(pb, L, b_ws[i], b_bs[i], b_ln_g[i], b_ln_b[i]),
              _mlstm_branch(pmc, L, c_conv_w[i], c_conv_b[i], c_gate_b[i], c_ln_g[i]),
              _gla_branch(pd, L, d_aup[i], d_ab[i], d_ln_g[i])]
        if i == DEPTH - 1:
            hs = [z[:, L:] for z in hs]
            pg = pg[:, L:]
            h = h[:, L:]
            L = 0
        h = _gated_add(h, _merge(hs, pg, br_w[i], out_w[i]), mod[:, 5], L)
        hn = _modulate(_rms(h, norm_g[i, 2]), mod[:, 6], mod[:, 7], L)
        h = _gated_add(h, 0.5 * _swiglu(hn, ffn_w1[i, 1], ffn_w3[i, 1], ffn_w2[i, 1]), mod[:, 8], L)
    return _rms(h, final_g)
```

```python
import functools

import jax
import jax.numpy as jnp
from jax import lax
from jax.experimental import pallas as pl
from jax.experimental.pallas import tpu as pltpu

F32 = jnp.float32
BF16 = jnp.bfloat16
HIGHEST = lax.Precision.HIGHEST

EPS = 1e-6
GRID_W = 64
N_MOD = 9
CHUNK = 64
A_HD = 64
A_LN_EPS = 64e-5
B_CHUNK = 128
C_HD = 128
C_LN_EPS = 1e-5
D_DK = 64
D_DV = 128
D_TAU = 16.0
D_SUB = 16
D_LN_EPS = 1e-6
VMEM_LIMIT = 56 * 1024 * 1024


def _cp(*sem):
    return pltpu.CompilerParams(dimension_semantics=sem, vmem_limit_bytes=VMEM_LIMIT)


def _mm(a, b, precision=None):
    return jnp.dot(a, b, precision=precision, preferred_element_type=F32)


def _nt(a, b, precision=None):
    return lax.dot_general(a, b, (((1,), (1,)), ((), ())), precision=precision,
                           preferred_element_type=F32)


def _tn(a, b, precision=None):
    return lax.dot_general(a, b, (((0,), (0,)), ((), ())), precision=precision,
                           preferred_element_type=F32)


def _log_sigmoid(x):
    return jnp.minimum(x, 0.0) - jnp.log(1.0 + jnp.exp(-jnp.abs(x)))


def _seg_sum(x, ones_blockdiag):
    hi = x.astype(BF16)
    lo = (x - hi.astype(F32)).astype(BF16)
    return _mm(hi, ones_blockdiag) + _mm(lo, ones_blockdiag)


def _tri(n, rev, strict):
    row = lax.broadcasted_iota(jnp.int32, (n, n), 0)
    col = lax.broadcasted_iota(jnp.int32, (n, n), 1)
    if rev:
        return (col > row) if strict else (col >= row)
    return (col < row) if strict else (col <= row)


def _norm_mod_kernel(h_ref, g_ref, sh_ref, sc_ref, o_ref):
    x = h_ref[...]
    y = x * lax.rsqrt(jnp.mean(x * x, axis=-1, keepdims=True) + EPS) * g_ref[...]
    o_ref[...] = (y * (1.0 + sc_ref[...]) + sh_ref[...]).astype(o_ref.dtype)


def _norm_mod(h, g, modp, k_shift, rows, tm, grp):
    d = h.shape[1]
    return pl.pallas_call(
        _norm_mod_kernel,
        grid=(rows // tm,),
        in_specs=[pl.BlockSpec((tm, d), lambda i: (i, 0)),
                  pl.BlockSpec((1, d), lambda i: (0, 0)),
                  pl.BlockSpec((None, None, 1, d), lambda i: (grp(i), k_shift, 0, 0)),
                  pl.BlockSpec((None, None, 1, d), lambda i: (grp(i), k_shift + 1, 0, 0))],
        out_specs=pl.BlockSpec((tm, d), lambda i: (i, 0)),
        out_shape=jax.ShapeDtypeStruct((rows, d), BF16),
        compiler_params=_cp("arbitrary"),
        name="norm_mod",
    )(h, g.reshape(1, d), modp, modp)


def _final_norm_kernel(h_ref, g_ref, o_ref):
    x = h_ref[...]
    o_ref[...] = x * lax.rsqrt(jnp.mean(x * x, axis=-1, keepdims=True) + EPS) * g_ref[...]


def _final_norm(h, g, tm):
    rows, d = h.shape
    return pl.pallas_call(
        _final_norm_kernel,
        grid=(rows // tm,),
        in_specs=[pl.BlockSpec((tm, d), lambda i: (i, 0)), pl.BlockSpec((1, d), lambda i: (0, 0))],
        out_specs=pl.BlockSpec((tm, d), lambda i: (i, 0)),
        out_shape=jax.ShapeDtypeStruct((rows, d), F32),
        compiler_params=_cp("arbitrary"),
        name="final_norm",
    )(h, g.reshape(1, d))


def _ada_kernel(c_ref, w_ref, b_ref, o_ref):
    c = c_ref[...]
    cond = (c * jax.nn.sigmoid(c)).astype(BF16)
    o_ref[...] = _mm(cond, w_ref[...].astype(BF16)) + b_ref[...]


def _ada_mod(cpad, ada_w, ada_b, tn):
    depth, d, n = ada_w.shape
    m = cpad.shape[0]
    return pl.pallas_call(
        _ada_kernel,
        grid=(depth, n // tn),
        in_specs=[pl.BlockSpec((m, d), lambda l, j: (0, 0)),
                  pl.BlockSpec((None, d, tn), lambda l, j: (l, 0, j)),
                  pl.BlockSpec((None, 1, tn), lambda l, j: (l, 0, j))],
        out_specs=pl.BlockSpec((None, m, tn), lambda l, j: (l, 0, j)),
        out_shape=jax.ShapeDtypeStruct((depth, m, n), F32),
        compiler_params=_cp("arbitrary", "arbitrary"),
        name="ada_mod",
    )(cpad, ada_w, ada_b.reshape(depth, 1, n))


def _ffn_up_kernel(x_ref, w1_ref, w3_ref, o_ref):
    x = x_ref[...]
    a = _mm(x, w1_ref[...].astype(BF16))
    b = _mm(x, w3_ref[...].astype(BF16))
    o_ref[...] = (a * jax.nn.sigmoid(a) * b).astype(o_ref.dtype)


def _ffn_up(x, w1, w3, l, s, tm, tf):
    rows, d = x.shape
    f = w1.shape[-1]
    wspec = pl.BlockSpec((None, None, d, tf), lambda i, j: (l, s, 0, j))
    return pl.pallas_call(
        _ffn_up_kernel,
        grid=(rows // tm, f // tf),
        in_specs=[pl.BlockSpec((tm, d), lambda i, j: (i, 0)), wspec, wspec],
        out_specs=pl.BlockSpec((tm, tf), lambda i, j: (i, j)),
        out_shape=jax.ShapeDtypeStruct((rows, f), BF16),
        compiler_params=_cp("arbitrary", "arbitrary"),
        name="ffn_up",
    )(x, w1, w3)


def _ffn_down_kernel(g_ref, w_ref, h_ref, gate_ref, o_ref):
    acc = _mm(g_ref[...], w_ref[...].astype(BF16))
    o_ref[...] = h_ref[...] + (0.5 * acc) * gate_ref[...]


def _ffn_down(gact, w2, h, modp, k_gate, l, s, tm, tn, grp):
    rows, f = gact.shape
    d = h.shape[1]
    return pl.pallas_call(
        _ffn_down_kernel,
        grid=(rows // tm, d // tn),
        in_specs=[pl.BlockSpec((tm, f), lambda i, j: (i, 0)),
                  pl.BlockSpec((None, None, f, tn), lambda i, j: (l, s, 0, j)),
                  pl.BlockSpec((tm, tn), lambda i, j: (i, j)),
                  pl.BlockSpec((None, None, 1, tn), lambda i, j: (grp(i), k_gate, 0, j))],
        out_specs=pl.BlockSpec((tm, tn), lambda i, j: (i, j)),
        out_shape=jax.ShapeDtypeStruct((rows, d), F32),
        compiler_params=_cp("arbitrary", "arbitrary"),
        name="ffn_down",
    )(gact, w2, h, modp)


def _in_proj_kernel(x_ref, w_ref, b_ref, o_ref):
    o_ref[...] = _mm(x_ref[...], w_ref[...]) + b_ref[...]


def _in_proj(x, w, b, tm, tn):
    rows, d = x.shape
    n = w.shape[1]
    return pl.pallas_call(
        _in_proj_kernel,
        grid=(rows // tm, n // tn),
        in_specs=[pl.BlockSpec((tm, d), lambda i, j: (i, 0)),
                  pl.BlockSpec((d, tn), lambda i, j: (0, j)),
                  pl.BlockSpec((1, tn), lambda i, j: (0, j))],
        out_specs=pl.BlockSpec((tm, tn), lambda i, j: (i, j)),
        out_shape=jax.ShapeDtypeStruct((rows, n), F32),
        compiler_params=_cp("arbitrary", "arbitrary"),
        name="in_proj",
    )(x, w, b)


def _merge_kernel(ha, hb, hc, hd, ga, gb, gc, gd, wa, wb, wc, wd, o_ref):
    y = jax.nn.sigmoid(ga[...]) * _mm(ha[...], wa[...].astype(BF16))
    y = y + jax.nn.sigmoid(gb[...]) * _mm(hb[...], wb[...].astype(BF16))
    y = y + jax.nn.sigmoid(gc[...]) * _mm(hc[...], wc[...].astype(BF16))
    y = y + jax.nn.sigmoid(gd[...]) * _mm(hd[...], wd[...].astype(BF16))
    o_ref[...] = y.astype(o_ref.dtype)


def _merge(hs, p, br_w, l, rows, tm, tn):
    wbr = hs[0].shape[1]
    d = br_w.shape[-1]
    nj = d // tn
    hspec = pl.BlockSpec((tm, wbr), lambda i, j: (i, 0))
    gspecs = [pl.BlockSpec((tm, tn), functools.partial(lambda i, j, n: (i, n * nj + j), n=n))
              for n in range(4)]
    wspecs = [pl.BlockSpec((None, None, wbr, tn), functools.partial(lambda i, j, n: (l, n, 0, j), n=n))
              for n in range(4)]
    return pl.pallas_call(
        _merge_kernel,
        grid=(rows // tm, nj),
        in_specs=[hspec] * 4 + gspecs + wspecs,
        out_specs=pl.BlockSpec((tm, tn), lambda i, j: (i, j)),
        out_shape=jax.ShapeDtypeStruct((rows, d), BF16),
        compiler_params=_cp("arbitrary", "arbitrary"),
        name="merge",
    )(*hs, p, p, p, p, br_w, br_w, br_w, br_w)


def _out_proj_kernel(y_ref, w_ref, h_ref, gate_ref, o_ref):
    acc = _mm(y_ref[...], w_ref[...].astype(BF16))
    o_ref[...] = h_ref[...] + acc * gate_ref[...]


def _out_proj(y, out_w, h, modp, k_gate, l, tm, tn, grp):
    rows, d = y.shape
    return pl.pallas_call(
        _out_proj_kernel,
        grid=(rows // tm, d // tn),
        in_specs=[pl.BlockSpec((tm, d), lambda i, j: (i, 0)),
                  pl.BlockSpec((None, d, tn), lambda i, j: (l, 0, j)),
                  pl.BlockSpec((tm, tn), lambda i, j: (i, j)),
                  pl.BlockSpec((None, None, 1, tn), lambda i, j: (grp(i), k_gate, 0, j))],
        out_specs=pl.BlockSpec((tm, tn), lambda i, j: (i, j)),
        out_shape=jax.ShapeDtypeStruct((rows, d), F32),
        compiler_params=_cp("arbitrary", "arbitrary"),
        name="out_proj",
    )(y, out_w, h, modp)


def _shift_lat_kernel(x_ref, mu_ref, o_ref):
    x = x_ref[...]
    n = x.shape[0]
    t = lax.broadcasted_iota(jnp.int32, x.shape, 0)
    c4 = lax.broadcasted_iota(jnp.int32, x.shape, 1) & 3
    tw = t & (GRID_W - 1)
    left = jnp.where(tw == 0, 0.0, pltpu.roll(x, 1, 0))
    right = jnp.where(tw == GRID_W - 1, 0.0, pltpu.roll(x, n - 1, 0))
    up = jnp.where(t < GRID_W, 0.0, pltpu.roll(x, GRID_W, 0))
    down = jnp.where(t >= n - GRID_W, 0.0, pltpu.roll(x, n - GRID_W, 0))
    sh = jnp.where(c4 == 0, left, jnp.where(c4 == 1, right, jnp.where(c4 == 2, up, down)))
    o_ref[...] = x + (sh - x) * mu_ref[...]


def _shift_ctx_kernel(x_ref, mu_ref, prev_ref, o_ref):
    del prev_ref
    x = x_ref[...]
    n = x.shape[0]
    t = lax.broadcasted_iota(jnp.int32, x.shape, 0)
    c2 = lax.broadcasted_iota(jnp.int32, x.shape, 1) & 1
    prv = jnp.where(t == 0, 0.0, pltpu.roll(x, 1, 0))
    nxt = jnp.where(t == n - 1, 0.0, pltpu.roll(x, n - 1, 0))
    sh = jnp.where(c2 == 0, prv, nxt)
    o_ref[...] = x + (sh - x) * mu_ref[...]


def _conv_body(x_ref, w_ref, b_ref, s_ref, o_ref):
    x = x_ref[...]
    n = x.shape[0]
    t = lax.broadcasted_iota(jnp.int32, x.shape, 0)
    prv = jnp.where(t == 0, 0.0, pltpu.roll(x, 1, 0))
    nxt = jnp.where(t == n - 1, 0.0, pltpu.roll(x, n - 1, 0))
    w = w_ref[...]
    y = prv * w[0:1] + x * w[1:2] + nxt * w[2:3] + b_ref[...]
    o_ref[...] = y * jax.nn.sigmoid(y) * s_ref[...]


def _conv_lat_kernel(x_ref, w_ref, b_ref, s_ref, o_ref):
    _conv_body(x_ref, w_ref, b_ref, s_ref, o_ref)


def _conv_ctx_kernel(x_ref, w_ref, b_ref, s_ref, prev_ref, o_ref):
    del prev_ref
    _conv_body(x_ref, w_ref, b_ref, s_ref, o_ref)


def _segment_pair(lat_kernel, ctx_kernel, p, col0, width, tc, small, nb, t_lat, t_ctx, name):
    rows = p.shape[0]
    nct = width // tc
    cb = col0 // tc
    ctx0 = nb * t_lat // t_ctx
    small_specs = [pl.BlockSpec((a.shape[0], tc), lambda b, j: (0, j)) for a in small]
    out_shape = jax.ShapeDtypeStruct((rows, width), F32)
    lat = pl.pallas_call(
        lat_kernel,
        grid=(nb, nct),
        in_specs=[pl.BlockSpec((t_lat, tc), lambda b, j: (b, cb + j))] + small_specs,
        out_specs=pl.BlockSpec((t_lat, tc), lambda b, j: (b, j)),
        out_shape=out_shape,
        compiler_params=_cp("arbitrary", "arbitrary"),
        name=name + "_lat",
    )(p, *small)
    return pl.pallas_call(
        ctx_kernel,
        grid=(nb, nct),
        in_specs=[pl.BlockSpec((t_ctx, tc), lambda b, j: (ctx0 + b, cb + j))] + small_specs
                 + [pl.BlockSpec(memory_space=pl.ANY)],
        out_specs=pl.BlockSpec((t_ctx, tc), lambda b, j: (ctx0 + b, j)),
        out_shape=out_shape,
        input_output_aliases={1 + len(small): 0},
        compiler_params=_cp("arbitrary", "arbitrary"),
        name=name + "_ctx",
    )(p, *small, lat)


def _rwkv_pre_kernel(za_ref, kk_ref, ka_ref, rk_ref, w0_ref, a0_ref, wup_ref, aup_ref, gup_ref, ones_ref,
                     r_o, v_o, kk_o, lwf_o, lwb_o, ktf_o, ktb_o, bf_o, bb_o, bonus_o, g_o):
    wb = r_o.shape[1]
    r = za_ref[:, 0:wb]
    k = za_ref[:, wb:2 * wb]
    v = za_ref[:, 2 * wb:3 * wb]
    wd = jnp.tanh(za_ref[:, 3 * wb:3 * wb + 128])
    ad = za_ref[:, 3 * wb + 128:3 * wb + 256]
    gd = jax.nn.sigmoid(za_ref[:, 3 * wb + 256:3 * wb + 384])
    ones = ones_ref[...]
    kq = k * kk_ref[...]
    kk = kq * lax.rsqrt(jnp.maximum(_seg_sum(kq * kq, ones), 1e-24))
    r_o[...] = r
    v_o[...] = v
    kk_o[...] = kk
    ka = ka_ref[...]
    for d, (lw_o, kt_o, b_o) in enumerate(((lwf_o, ktf_o, bf_o), (lwb_o, ktb_o, bb_o))):
        xw = w0_ref[d:d + 1] + _mm(wd, wup_ref[d])
        lw_o[...] = -jax.nn.sigmoid(xw) * 0.6065306597126334
        a = jax.nn.sigmoid(a0_ref[d:d + 1] + _mm(ad, aup_ref[d]))
        kt_o[...] = k * (1.0 + (a - 1.0) * ka)
        b_o[...] = kk * a
    bonus_o[...] = _seg_sum(r * k * rk_ref[...], ones) * v
    g_o[...] = _mm(gd, gup_ref[...])


def _rwkv_pre(za, prm, tm):
    rows = za.shape[0]
    wb = prm["kk"].shape[1]
    full = lambda a: pl.BlockSpec(a.shape, lambda i: (0,) * a.ndim)
    small = [prm["kk"], prm["ka"], prm["rk"], prm["w0"], prm["a0"], prm["wupp"], prm["aupp"], prm["gup"],
             prm["ones"]]
    ospec = pl.BlockSpec((tm, wb), lambda i: (i, 0))
    return pl.pallas_call(
        _rwkv_pre_kernel,
        grid=(rows // tm,),
        in_specs=[pl.BlockSpec((tm, za.shape[1]), lambda i: (i, 0))] + [full(a) for a in small],
        out_specs=[ospec] * 11,
        out_shape=[jax.ShapeDtypeStruct((rows, wb), F32)] * 11,
        compiler_params=_cp("arbitrary"),
        name="rwkv_pre",
    )(za, *small)


def _rwkv_head(r, kk, b, kt, v, s_prev, g_last, rev):
    n = r.shape[0]
    strict = _tri(n, rev, True)
    incl = _tri(n, rev, False)
    z = _nt(jnp.concatenate([kk, r], axis=0), jnp.concatenate([kt, b, s_prev], axis=0))
    a_kv = jnp.where(strict, z[0:n, 0:n], 0.0)
    a_kb = jnp.where(strict, z[0:n, n:2 * n], 0.0)
    r_kv = jnp.where(incl, z[n:2 * n, 0:n], 0.0)
    r_kb = jnp.where(incl, z[n:2 * n, n:2 * n], 0.0)
    rhs = z[0:n, 2 * n:] + _mm(a_kv, v)
    u = rhs - _mm(a_kb, rhs)
    pw = _mm(a_kb, a_kb)
    m = 2
    while 2 * m < n:
        u = u + _mm(pw, u)
        pw = _mm(pw, pw)
        m *= 2
    u = u + _mm(pw, u)
    y = z[n:2 * n, 2 * n:] + _mm(r_kv, v) - _mm(r_kb, u)
    s_new = (s_prev + _tn(jnp.concatenate([v, u], axis=0), jnp.concatenate([kt, -b], axis=0))) * g_last
    return y, s_new


def _rwkv_scan_kernel(rf, vf, kkf, lwf, ktf, bf, rb, vb, kkb, lwb, ktb, bb, yf, yb, s_ref):
    @pl.when(pl.program_id(1) == 0)
    def _():
        s_ref[...] = jnp.zeros_like(s_ref)

    n = rf.shape[0]
    nh = rf.shape[1] // A_HD
    for d, (r_r, v_r, kk_r, lw_r, kt_r, b_r, y_r) in enumerate(((rf, vf, kkf, lwf, ktf, bf, yf),
                                                                 (rb, vb, kkb, lwb, ktb, bb, yb))):
        rev = d == 1
        lw = lw_r[...]
        c = _mm(_tri(n, rev, False).astype(F32), lw, HIGHEST)
        eg = jnp.exp(c)
        ieg = jnp.exp(-c)
        r_all = r_r[...] * eg
        kk_all = kk_r[...] * jnp.exp(c - lw)
        kt_all = kt_r[...] * ieg
        b_all = b_r[...] * ieg
        v_all = v_r[...]
        g_last = eg[0:1] if rev else eg[n - 1:n]
        for h in range(nh):
            sl = slice(h * A_HD, (h + 1) * A_HD)
            y, s_new = _rwkv_head(r_all[:, sl], kk_all[:, sl], b_all[:, sl], kt_all[:, sl], v_all[:, sl],
                                  s_ref[d * nh + h], g_last[:, sl], rev)
            y_r[:, sl] = y
            s_ref[d * nh + h] = s_new


def _chunk_maps(nb, t_lat, t_ctx):
    nlc = t_lat // CHUNK
    ncc = t_ctx // CHUNK

    def fwd(b, j):
        return jnp.where(j < ncc, nb * nlc + b * ncc + j, b * nlc + (j - ncc))

    def rev(b, j):
        return jnp.where(j < ncc, nb * nlc + b * ncc + (ncc - 1 - j), b * nlc + (nlc - 1 - (j - ncc)))

    return fwd, rev, nlc + ncc


def _rwkv_scan(r, v, kk, lwf, lwb, ktf, ktb, bf, bb, nb, t_lat, t_ctx):
    rows, wb = r.shape
    fwd, rev, nch = _chunk_maps(nb, t_lat, t_ctx)
    fs = pl.BlockSpec((CHUNK, wb), lambda b, j: (fwd(b, j), 0))
    rs = pl.BlockSpec((CHUNK, wb), lambda b, j: (rev(b, j), 0))
    return pl.pallas_call(
        _rwkv_scan_kernel,
        grid=(nb, nch),
        in_specs=[fs] * 6 + [rs] * 6,
        out_specs=[fs, rs],
        out_shape=[jax.ShapeDtypeStruct((rows, wb), F32)] * 2,
        scratch_shapes=[pltpu.VMEM((2 * (wb // A_HD), A_HD, A_HD), F32)],
        compiler_params=_cp("arbitrary", "arbitrary"),
        name="rwkv_scan",
    )(r, v, kk, lwf, ktf, bf, r, v, kk, lwb, ktb, bb)


def _mlstm_scan_kernel(qf, kf, vf, gf, qb, kb, vb, gb, gbias_ref, hf, hb, c_ref, n_ref, m_ref):
    @pl.when(pl.program_id(1) == 0)
    def _():
        c_ref[...] = jnp.zeros_like(c_ref)
        n_ref[...] = jnp.zeros_like(n_ref)
        m_ref[...] = jnp.zeros_like(m_ref)

    n = qf.shape[0]
    nh = qf.shape[1] // C_HD
    for d, (q_r, k_r, v_r, g_r, h_r) in enumerate(((qf, kf, vf, gf, hf), (qb, kb, vb, gb, hb))):
        rev = d == 1
        gates = g_r[...] + gbias_ref[...]
        fg = _log_sigmoid(gates)
        mask = _tri(n, rev, False)
        mi = mask.astype(F32)
        bcol = _mm(mi, fg, HIGHEST)
        gates_t = gates.T
        brow = _nt(fg.T, mi, HIGHEST)
        q_all = q_r[...]
        k_all = k_r[...]
        v_all = v_r[...]
        for h in range(nh):
            sl = slice(h * C_HD, (h + 1) * C_HD)
            ii = d * 2 * nh + h
            fi = ii + nh
            idx = d * nh + h
            q = q_all[:, sl]
            k = k_all[:, sl]
            v = v_all[:, sl]
            b_c = bcol[:, fi:fi + 1]
            b_r = brow[fi:fi + 1, :]
            i_c = gates[:, ii:ii + 1]
            i_r = gates_t[ii:ii + 1, :]
            m_prev = m_ref[idx][:, 0:1]
            cm = c_ref[idx]
            nn = n_ref[idx]
            dlog = jnp.where(mask, b_c - b_r + i_r, -jnp.inf)
            inter = b_c + m_prev
            m_t = jnp.maximum(inter, jnp.max(dlog, axis=1, keepdims=True))
            dw = jnp.exp(dlog - m_t)
            iw = jnp.exp(inter - m_t)
            s = _nt(q, k) * dw
            num = _mm(s, v) + iw * _nt(q, cm)
            den = jnp.sum(s, axis=1, keepdims=True) + iw * jnp.sum(q * nn, axis=1, keepdims=True)
            den = jnp.maximum(jnp.abs(den), jnp.exp(-m_t))
            h_r[:, sl] = num / den
            b_l = b_c[0:1] if rev else b_c[n - 1:n]
            gl = b_l - b_c + i_c
            m_new = jnp.maximum(b_l + m_prev, jnp.max(gl, axis=0, keepdims=True))
            sw = jnp.exp(gl - m_new)
            dec = jnp.exp(b_l + m_prev - m_new)
            c_ref[idx] = dec * cm + _tn(v * sw, k)
            n_ref[idx] = dec * nn + jnp.sum(sw * k, axis=0, keepdims=True)
            m_ref[idx] = jnp.broadcast_to(m_new, m_ref.shape[1:])


def _mlstm_scan(qk, p, gbias, col_v, col_g, nb, t_lat, t_ctx, wb):
    rows = p.shape[0]
    fwd, rev, nch = _chunk_maps(nb, t_lat, t_ctx)
    nh = wb // C_HD

    def specs(cm):
        return [pl.BlockSpec((CHUNK, wb), lambda b, j: (cm(b, j), 0)),
                pl.BlockSpec((CHUNK, wb), lambda b, j: (cm(b, j), 1)),
                pl.BlockSpec((CHUNK, wb), lambda b, j: (cm(b, j), col_v // wb)),
                pl.BlockSpec((CHUNK, 128), lambda b, j: (cm(b, j), col_g // 128))]

    return pl.pallas_call(
        _mlstm_scan_kernel,
        grid=(nb, nch),
        in_specs=specs(fwd) + specs(rev) + [pl.BlockSpec((1, 128), lambda b, j: (0, 0))],
        out_specs=[pl.BlockSpec((CHUNK, wb), lambda b, j: (fwd(b, j), 0)),
                   pl.BlockSpec((CHUNK, wb), lambda b, j: (rev(b, j), 0))],
        out_shape=[jax.ShapeDtypeStruct((rows, wb), F32)] * 2,
        scratch_shapes=[pltpu.VMEM((2 * nh, C_HD, C_HD), F32), pltpu.VMEM((2 * nh, 1, C_HD), F32),
                        pltpu.VMEM((2 * nh, 1, 128), F32)],
        compiler_params=_cp("arbitrary", "arbitrary"),
        name="mlstm_scan",
    )(qk, qk, p, p, qk, qk, p, p, gbias)


def _gla_head(q, k, v, bc, s_prev, rev):
    n = q.shape[0]
    outs = []
    trow = lax.broadcasted_iota(jnp.int32, (D_SUB, 1), 0)
    for blk in range(n // D_SUB):
        r0 = blk * D_SUB
        qi = q[r0:r0 + D_SUB]
        bi = bc[r0:r0 + D_SUB]
        o = _nt(qi * jnp.exp(bi), s_prev)
        if rev:
            gn, lo, hi = bc[r0 + D_SUB - 1:r0 + D_SUB], r0 + D_SUB, n
        else:
            gn, lo, hi = bc[r0:r0 + 1], 0, r0
        if hi > lo:
            qs = qi * jnp.exp(bi - gn)
            ks = k[lo:hi] * jnp.exp(gn - bc[lo:hi])
            o = o + _mm(_nt(qs, ks), v[lo:hi])
        for s in range(D_SUB):
            e = qi * k[r0 + s:r0 + s + 1] * jnp.exp(bi - bc[r0 + s:r0 + s + 1])
            col = jnp.sum(e, axis=1, keepdims=True)
            keep = (trow <= s) if rev else (trow >= s)
            o = o + jnp.where(keep, col, 0.0) * v[r0 + s:r0 + s + 1]
        outs.append(o)
    b_l = bc[0:1] if rev else bc[n - 1:n]
    s_new = s_prev * jnp.exp(b_l) + _tn(v, k * jnp.exp(b_l - bc))
    return jnp.concatenate(outs, axis=0), s_new


def _gla_scan_kernel(qf, kf, vf, af, qb, kb, vb, ab, aup_ref, abias_ref, of, ob, s_ref):
    @pl.when(pl.program_id(1) == 0)
    def _():
        s_ref[...] = jnp.zeros_like(s_ref)

    n = qf.shape[0]
    nh = qf.shape[1] // D_DK
    for d, (q_r, k_r, v_r, a_r, o_r) in enumerate(((qf, kf, vf, af, of), (qb, kb, vb, ab, ob))):
        rev = d == 1
        la = _log_sigmoid(_mm(a_r[...], aup_ref[d]) + abias_ref[d]) * (1.0 / D_TAU)
        bc = _mm(_tri(n, rev, False).astype(F32), la, HIGHEST)
        q_all = q_r[...] * (D_DK ** -0.5)
        k_all = k_r[...]
        v_all = v_r[...]
        for h in range(nh):
            ksl = slice(h * D_DK, (h + 1) * D_DK)
            vsl = slice(h * D_DV, (h + 1) * D_DV)
            o, s_new = _gla_head(q_all[:, ksl], k_all[:, ksl], v_all[:, vsl], bc[:, ksl], s_ref[d * nh + h], rev)
            o_r[:, vsl] = o
            s_ref[d * nh + h] = s_new


def _gla_scan(p, aupp, abias, col_q, col_k, col_v, col_a, nb, t_lat, t_ctx, wb):
    rows = p.shape[0]
    fwd, rev, nch = _chunk_maps(nb, t_lat, t_ctx)
    wk = aupp.shape[-1]
    nh = wk // D_DK

    def specs(cm):
        return [pl.BlockSpec((CHUNK, wk), lambda b, j: (cm(b, j), col_q // wk)),
                pl.BlockSpec((CHUNK, wk), lambda b, j: (cm(b, j), col_k // wk)),
                pl.BlockSpec((CHUNK, wb), lambda b, j: (cm(b, j), col_v // wb)),
                pl.BlockSpec((CHUNK, 128), lambda b, j: (cm(b, j), col_a // 128))]

    return pl.pallas_call(
        _gla_scan_kernel,
        grid=(nb, nch),
        in_specs=specs(fwd) + specs(rev) + [pl.BlockSpec(aupp.shape, lambda b, j: (0, 0, 0)),
                                           pl.BlockSpec(abias.shape, lambda b, j: (0, 0, 0))],
        out_specs=[pl.BlockSpec((CHUNK, wb), lambda b, j: (fwd(b, j), 0)),
                   pl.BlockSpec((CHUNK, wb), lambda b, j: (rev(b, j), 0))],
        out_shape=[jax.ShapeDtypeStruct((rows, wb), F32)] * 2,
        scratch_shapes=[pltpu.VMEM((2 * nh, D_DV, D_DK), F32)],
        compiler_params=_cp("arbitrary", "arbitrary"),
        name="gla_scan",
    )(p, p, p, p, p, p, p, p, aupp, abias)


def _branch_out_kernel(ya_f, ya_b, bonus, ga, pb, hc_f, hc_b, oc, od_f, od_b, gdd,
                       a_lng, a_lnb, ones, b_ws, b_bias, b_lng, b_lnb, c_lng, d_lng,
                       hs_a, hs_b, hs_c, hs_d):
    wb = hs_a.shape[1]
    y = ya_f[...] + ya_b[...]
    on = ones[...]
    mu = _seg_sum(y, on) * (1.0 / A_HD)
    yc = y - mu
    var = _seg_sum(yc * yc, on) * (1.0 / A_HD)
    yn = yc * lax.rsqrt(var + A_LN_EPS) * a_lng[...] + a_lnb[...]
    hs_a[...] = ((yn + bonus[...]) * ga[...]).astype(hs_a.dtype)
    z = jax.nn.gelu(pb[...])
    u = z[:, 0:wb]
    vv = z[:, wb:2 * wb]
    mu = jnp.mean(vv, axis=-1, keepdims=True)
    vc = vv - mu
    vn = (vc * lax.rsqrt(jnp.mean(vc * vc, axis=-1, keepdims=True) + 1e-5) * b_lng[...] + b_lnb[...])
    vn = vn.astype(BF16)
    for ck in range(u.shape[0] // B_CHUNK):
        rs = slice(ck * B_CHUNK, (ck + 1) * B_CHUNK)
        for g in range(wb // B_CHUNK):
            cs = slice(g * B_CHUNK, (g + 1) * B_CHUNK)
            s = _mm(b_ws[g].astype(BF16), vn[rs, cs]) + b_bias[:, cs]
            hs_b[rs, cs] = (u[rs, cs] * s).astype(hs_b.dtype)
    hc = hc_f[...] + hc_b[...]
    ogate = jax.nn.sigmoid(oc[...])
    gc = c_lng[...]
    for h in range(wb // C_HD):
        sl = slice(h * C_HD, (h + 1) * C_HD)
        x = hc[:, sl]
        xc = x - jnp.mean(x, axis=-1, keepdims=True)
        xn = xc * lax.rsqrt(jnp.mean(xc * xc, axis=-1, keepdims=True) + C_LN_EPS) * gc[:, sl]
        hs_c[:, sl] = (xn * ogate[:, sl]).astype(hs_c.dtype)
    od = od_f[...] + od_b[...]
    gg = gdd[...]
    gg = gg * jax.nn.sigmoid(gg)
    gd_ = d_lng[...]
    for h in range(wb // D_DV):
        sl = slice(h * D_DV, (h + 1) * D_DV)
        x = od[:, sl]
        xn = x * lax.rsqrt(jnp.mean(x * x, axis=-1, keepdims=True) + D_LN_EPS) * gd_[:, sl]
        hs_d[:, sl] = (xn * gg[:, sl]).astype(hs_d.dtype)


def _branch_out(ya_f, ya_b, bonus, ga, p, hc_f, hc_b, od_f, od_b, prm, cols, rows, tm):
    wb = ya_f.shape[1]
    row = lambda w, cb: pl.BlockSpec((tm, w), lambda i: (i, cb))
    small = [prm["a_lng"], prm["a_lnb"], prm["ones"], prm["b_ws"], prm["b_bias"], prm["b_lng"], prm["b_lnb"],
             prm["c_lng"], prm["d_lng"]]
    full = lambda a: pl.BlockSpec(a.shape, lambda i: (0,) * a.ndim)
    return pl.pallas_call(
        _branch_out_kernel,
        grid=(rows // tm,),
        in_specs=[row(wb, 0)] * 4 + [row(2 * wb, cols["b"] // (2 * wb))] + [row(wb, 0)] * 2
                 + [row(wb, cols["c_o"] // wb)] + [row(wb, 0)] * 2 + [row(wb, cols["d_g"] // wb)]
                 + [full(a) for a in small],
        out_specs=[row(wb, 0)] * 4,
        out_shape=[jax.ShapeDtypeStruct((rows, wb), BF16)] * 4,
        compiler_params=_cp("arbitrary"),
        name="branch_out",
    )(ya_f, ya_b, bonus, ga, p, hc_f, hc_b, p, od_f, od_b, p, *small)


def _pick_tile(*sizes):
    for t in (1024, 512, 256, 128):
        if all(s % t == 0 for s in sizes):
            return t
    raise ValueError("token counts must be multiples of 128")


def _pick_cols(n, cands):
    for t in cands:
        if n % t == 0:
            return t
    raise ValueError(f"no column tile for {n}")


def _proj_cols(d_model):
    wb = d_model // 4
    col = {"g": 0, "b": 4 * d_model, "c_qk": 4 * d_model + 2 * wb}
    col["c_v"] = col["c_qk"] + 2 * wb
    col["c_o"] = col["c_v"] + wb
    col["d_q"] = col["c_o"] + wb
    col["d_k"] = col["d_q"] + wb // 2
    col["d_v"] = col["d_k"] + wb // 2
    col["d_g"] = col["d_v"] + wb
    col["c_gate"] = col["d_g"] + wb
    col["d_a"] = col["c_gate"] + 128
    col["a"] = col["d_a"] + 128
    col["end"] = col["a"] + 3 * wb + 384
    assert col["a"] % 384 == 0
    return col


def _mixers(p, l, nb, t_lat, t_ctx, rows, w):
    wb = w["a_kk"].shape[1]
    col = _proj_cols(4 * wb)
    a_cols = 3 * wb + 384
    row = lambda a: a.reshape(1, -1)
    ones_bd = jnp.kron(jnp.eye(wb // A_HD, dtype=F32), jnp.ones((A_HD, A_HD), F32)).astype(BF16)

    za = _segment_pair(_shift_lat_kernel, _shift_ctx_kernel, p, col["a"], a_cols, 384,
                       [row(w["a_mu"][l])], nb, t_lat, t_ctx, "rwkv_shift")
    a_wup, a_aup = w["a_wup"], w["a_aup"]
    wupp = jnp.zeros((2, 128, wb), F32).at[0, 0:64].set(a_wup[l, 0]).at[1, 64:128].set(a_wup[l, 1])
    aupp = jnp.zeros((2, 128, wb), F32).at[0, 0:64].set(a_aup[l, 0]).at[1, 64:128].set(a_aup[l, 1])
    prm_a = {"kk": row(w["a_kk"][l]), "ka": row(w["a_ka"][l]), "rk": row(w["a_rk"][l]), "w0": w["a_w0"][l],
             "a0": w["a_a0"][l], "wupp": wupp, "aupp": aupp, "gup": w["a_gup"][l], "ones": ones_bd}
    r_, v_, kk_, lwf, lwb, ktf, ktb, bf_, bb_, bonus, ga = _rwkv_pre(za, prm_a, 256)
    ya_f, ya_b = _rwkv_scan(r_, v_, kk_, lwf, lwb, ktf, ktb, bf_, bb_, nb, t_lat, t_ctx)

    kscale = jnp.concatenate([jnp.ones((wb,), F32), jnp.full((wb,), C_HD ** -0.5, F32)]).reshape(1, -1)
    qk = _segment_pair(_conv_lat_kernel, _conv_ctx_kernel, p, col["c_qk"], 2 * wb, 256,
                       [w["c_conv_w"][l], row(w["c_conv_b"][l]), kscale], nb, t_lat, t_ctx, "mlstm_conv")
    gbias = jnp.zeros((1, 128), F32).at[0, 0:16].set(w["c_gate_b"][l].reshape(-1))
    hc_f, hc_b = _mlstm_scan(qk, p, gbias, col["c_v"], col["c_gate"], nb, t_lat, t_ctx, wb)

    d_aup = w["d_aup"]
    rk_d = d_aup.shape[2]
    aupp_d = (jnp.zeros((2, 128, wb // 2), F32).at[0, 0:rk_d].set(d_aup[l, 0])
              .at[1, rk_d:2 * rk_d].set(d_aup[l, 1]))
    od_f, od_b = _gla_scan(p, aupp_d, w["d_ab"][l].reshape(2, 1, -1), col["d_q"], col["d_k"], col["d_v"],
                           col["d_a"], nb, t_lat, t_ctx, wb)

    b_bias = jnp.repeat(w["b_bs"][l].T, B_CHUNK, axis=1)
    prm_o = {"a_lng": row(w["a_ln_g"][l]), "a_lnb": row(w["a_ln_b"][l]), "ones": ones_bd, "b_ws": w["b_ws"][l],
             "b_bias": b_bias, "b_lng": row(w["b_ln_g"][l]), "b_lnb": row(w["b_ln_b"][l]),
             "c_lng": row(w["c_ln_g"][l]), "d_lng": row(w["d_ln_g"][l])}
    return _branch_out(ya_f, ya_b, bonus, ga, p, hc_f, hc_b, od_f, od_b, prm_o, col, rows, 256)


def kernel(x, c, ctx, c_ctx, ada_w, ada_b, norm_g, ffn_w1, ffn_w3, ffn_w2, in_w, in_b, a_mu, a_w0, a_wup, a_a0, a_aup, a_gup, a_kk, a_ka, a_rk, a_ln_g, a_ln_b, b_ws, b_bs, b_ln_g, b_ln_b, c_conv_w, c_conv_b, c_gate_b, c_ln_g, d_aup, d_ab, d_ln_g, br_w, out_w, final_g):
    weights = dict(a_mu=a_mu, a_w0=a_w0, a_wup=a_wup, a_a0=a_a0, a_aup=a_aup, a_gup=a_gup, a_kk=a_kk, a_ka=a_ka,
                   a_rk=a_rk.reshape(a_rk.shape[0], -1), a_ln_g=a_ln_g, a_ln_b=a_ln_b, b_ws=b_ws, b_bs=b_bs,
                   b_ln_g=b_ln_g, b_ln_b=b_ln_b, c_conv_w=c_conv_w, c_conv_b=c_conv_b, c_gate_b=c_gate_b,
                   c_ln_g=c_ln_g, d_aup=d_aup, d_ab=d_ab, d_ln_g=d_ln_g)
    nb, t_lat, d_model = x.shape
    t_ctx = ctx.shape[1]
    depth = ada_w.shape[0]
    wb = d_model // 4
    d_ff = ffn_w1.shape[-1]
    n_lat = nb * t_lat
    n_ctx = nb * t_ctx
    rows_all = n_lat + n_ctx
    assert t_lat % (GRID_W * 2) == 0 and t_ctx % B_CHUNK == 0 and n_lat % t_ctx == 0
    assert wb == 512 and a_wup.shape[2] == 64 and a_aup.shape[2] == 64 and a_gup.shape[1] == 128

    tm = _pick_tile(t_lat, n_ctx)

    def grp(i):
        return jnp.where(i < n_lat // tm, 1 + i // (t_lat // tm), 0)

    tm_s = _pick_tile(t_lat, n_ctx, 256)
    tf = _pick_cols(d_ff, (512, 256, 128))

    a_cols = 3 * wb + 384
    n_proj = _proj_cols(d_model)["end"]
    o_a, o_b = 0, a_cols
    o_c = o_b + 2 * wb
    o_d = o_c + 4 * wb + 16
    o_g = o_d + 3 * wb + 32
    tn_proj = _pick_cols(n_proj, (1152, 384, 128))

    h = jnp.concatenate([x.reshape(n_lat, d_model), ctx.reshape(n_ctx, d_model)], axis=0)

    m_pad = -(-(nb + 1) // 8) * 8
    cpad = jnp.zeros((m_pad, d_model), F32).at[0].set(c_ctx).at[1:nb + 1].set(c)
    mod_all = _ada_mod(cpad, ada_w, ada_b, _pick_cols(ada_w.shape[-1], (1024, 512, 256, 128)))
    mod_all = mod_all.reshape(depth, m_pad, N_MOD, 1, d_model)

    for l in range(depth):
        modp = mod_all[l]
        last = l == depth - 1

        hn = _norm_mod(h, norm_g[l, 0], modp, 0, rows_all, tm_s, lambda i: jnp.where(
            i < n_lat // tm_s, 1 + i // (t_lat // tm_s), 0))
        gact = _ffn_up(hn, ffn_w1, ffn_w3, l, 0, tm, tf)
        h = _ffn_down(gact, ffn_w2, h, modp, 2, l, 0, tm, 256, grp)

        hn = _norm_mod(h, norm_g[l, 1], modp, 3, rows_all, tm_s, lambda i: jnp.where(
            i < n_lat // tm_s, 1 + i // (t_lat // tm_s), 0))
        w = in_w[l]
        bi = in_b[l]
        z = lambda k: jnp.zeros((d_model, k), F32)
        zb = lambda k: jnp.zeros((k,), F32)
        w_perm = jnp.concatenate([
            w[:, o_g:o_g + 4 * d_model], w[:, o_b:o_b + 2 * wb], w[:, o_c:o_c + 4 * wb],
            w[:, o_d:o_d + 3 * wb], w[:, o_c + 4 * wb:o_c + 4 * wb + 16], z(112),
            w[:, o_d + 3 * wb:o_d + 3 * wb + 32], z(96), w[:, o_a:o_a + a_cols]], axis=1).astype(BF16)
        b_perm = jnp.concatenate([
            bi[o_g:o_g + 4 * d_model], bi[o_b:o_b + 2 * wb], bi[o_c:o_c + 4 * wb],
            bi[o_d:o_d + 3 * wb], bi[o_c + 4 * wb:o_c + 4 * wb + 16], zb(112),
            bi[o_d + 3 * wb:o_d + 3 * wb + 32], zb(96), bi[o_a:o_a + a_cols]]).reshape(1, n_proj)
        p = _in_proj(hn, w_perm, b_perm, tm, tn_proj)

        rows = n_lat if last else rows_all
        hs = _mixers(p, l, nb, t_lat, t_ctx, rows, weights)
        y = _merge(hs, p, br_w, l, rows, tm, 512)
        h = _out_proj(y, out_w, h, modp, 5, l, tm, 512, grp)

        hn = _norm_mod(h, norm_g[l, 2], modp, 6, rows, tm_s, lambda i: jnp.where(
            i < n_lat // tm_s, 1 + i // (t_lat // tm_s), 0))
        gact = _ffn_up(hn, ffn_w1, ffn_w3, l, 1, tm, tf)
        h = _ffn_down(gact, ffn_w2, h, modp, 8, l, 1, tm, 256, grp)

    out = _final_norm(h, final_g, tm_s)
    return out.reshape(nb, t_lat, d_model)
```

```python
import functools

import jax
import jax.numpy as jnp
from jax import lax
from jax.experimental import pallas as pl
from jax.experimental.pallas import tpu as pltpu

F32 = jnp.float32
BF16 = jnp.bfloat16
HIGHEST = lax.Precision.HIGHEST

EPS = 1e-6
GRID_W = 64
N_MOD = 9
CHUNK = 64
A_HD = 64
A_LN_EPS = 64e-5
B_CHUNK = 128
C_HD = 128
C_LN_EPS = 1e-5
D_DK = 64
D_DV = 128
D_TAU = 16.0
D_SUB = 16
D_LN_EPS = 1e-6
VMEM_LIMIT = 56 * 1024 * 1024


def _cp(*sem):
    return pltpu.CompilerParams(dimension_semantics=sem, vmem_limit_bytes=VMEM_LIMIT)


def _mm(a, b, precision=None):
    return jnp.dot(a, b, precision=precision, preferred_element_type=F32)


def _nt(a, b, precision=None):
    return lax.dot_general(a, b, (((1,), (1,)), ((), ())), precision=precision,
                           preferred_element_type=F32)


def _tn(a, b, precision=None):
    return lax.dot_general(a, b, (((0,), (0,)), ((), ())), precision=precision,
                           preferred_element_type=F32)


def _log_sigmoid(x):
    return jnp.minimum(x, 0.0) - jnp.log(1.0 + jnp.exp(-jnp.abs(x)))


def _seg_sum(x, ones_blockdiag):
    hi = x.astype(BF16)
    lo = (x - hi.astype(F32)).astype(BF16)
    return _mm(hi, ones_blockdiag) + _mm(lo, ones_blockdiag)


def _tri(n, rev, strict):
    row = lax.broadcasted_iota(jnp.int32, (n, n), 0)
    col = lax.broadcasted_iota(jnp.int32, (n, n), 1)
    if rev:
        return (col > row) if strict else (col >= row)
    return (col < row) if strict else (col <= row)


def _norm_mod_kernel(h_ref, g_ref, sh_ref, sc_ref, o_ref):
    x = h_ref[...]
    y = x * lax.rsqrt(jnp.mean(x * x, axis=-1, keepdims=True) + EPS) * g_ref[...]
    o_ref[...] = (y * (1.0 + sc_ref[...]) + sh_ref[...]).astype(o_ref.dtype)


def _norm_mod(h, g, modp, k_shift, rows, tm, grp):
    d = h.shape[1]
    return pl.pallas_call(
        _norm_mod_kernel,
        grid=(rows // tm,),
        in_specs=[pl.BlockSpec((tm, d), lambda i: (i, 0)),
                  pl.BlockSpec((1, d), lambda i: (0, 0)),
                  pl.BlockSpec((None, None, 1, d), lambda i: (grp(i), k_shift, 0, 0)),
                  pl.BlockSpec((None, None, 1, d), lambda i: (grp(i), k_shift + 1, 0, 0))],
        out_specs=pl.BlockSpec((tm, d), lambda i: (i, 0)),
        out_shape=jax.ShapeDtypeStruct((rows, d), BF16),
        compiler_params=_cp("arbitrary"),
        name="norm_mod",
    )(h, g.reshape(1, d), modp, modp)


def _final_norm_kernel(h_ref, g_ref, o_ref):
    x = h_ref[...]
    o_ref[...] = x * lax.rsqrt(jnp.mean(x * x, axis=-1, keepdims=True) + EPS) * g_ref[...]


def _final_norm(h, g, tm):
    rows, d = h.shape
    return pl.pallas_call(
        _final_norm_kernel,
        grid=(rows // tm,),
        in_specs=[pl.BlockSpec((tm, d), lambda i: (i, 0)), pl.BlockSpec((1, d), lambda i: (0, 0))],
        out_specs=pl.BlockSpec((tm, d), lambda i: (i, 0)),
        out_shape=jax.ShapeDtypeStruct((rows, d), F32),
        compiler_params=_cp("arbitrary"),
        name="final_norm",
    )(h, g.reshape(1, d))


def _ada_kernel(c_ref, w_ref, b_ref, o_ref):
    c = c_ref[...]
    cond = (c * jax.nn.sigmoid(c)).astype(BF16)
    o_ref[...] = _mm(cond, w_ref[...].astype(BF16)) + b_ref[...]


def _ada_mod(cpad, ada_w, ada_b, tn):
    depth, d, n = ada_w.shape
    m = cpad.shape[0]
    return pl.pallas_call(
        _ada_kernel,
        grid=(depth, n // tn),
        in_specs=[pl.BlockSpec((m, d), lambda l, j: (0, 0)),
                  pl.BlockSpec((None, d, tn), lambda l, j: (l, 0, j)),
                  pl.BlockSpec((None, 1, tn), lambda l, j: (l, 0, j))],
        out_specs=pl.BlockSpec((None, m, tn), lambda l, j: (l, 0, j)),
        out_shape=jax.ShapeDtypeStruct((depth, m, n), F32),
        compiler_params=_cp("arbitrary", "arbitrary"),
        name="ada_mod",
    )(cpad, ada_w, ada_b.reshape(depth, 1, n))


def _ffn_up_kernel(x_ref, w1_ref, w3_ref, o_ref):
    x = x_ref[...]
    a = _mm(x, w1_ref[...].astype(BF16))
    b = _mm(x, w3_ref[...].astype(BF16))
    o_ref[...] = (a * jax.nn.sigmoid(a) * b).astype(o_ref.dtype)


def _ffn_up(x, w1, w3, l, s, tm, tf):
    rows, d = x.shape
    f = w1.shape[-1]
    wspec = pl.BlockSpec((None, None, d, tf), lambda i, j: (l, s, 0, j))
    return pl.pallas_call(
        _ffn_up_kernel,
        grid=(rows // tm, f // tf),
        in_specs=[pl.BlockSpec((tm, d), lambda i, j: (i, 0)), wspec, wspec],
        out_specs=pl.BlockSpec((tm, tf), lambda i, j: (i, j)),
        out_shape=jax.ShapeDtypeStruct((rows, f), BF16),
        compiler_params=_cp("arbitrary", "arbitrary"),
        name="ffn_up",
    )(x, w1, w3)


def _ffn_down_kernel(g_ref, w_ref, h_ref, gate_ref, o_ref):
    acc = _mm(g_ref[...], w_ref[...].astype(BF16))
    o_ref[...] = h_ref[...] + (0.5 * acc) * gate_ref[...]


def _ffn_down(gact, w2, h, modp, k_gate, l, s, tm, tn, grp):
    rows, f = gact.shape
    d = h.shape[1]
    return pl.pallas_call(
        _ffn_down_kernel,
        grid=(rows // tm, d // tn),
        in_specs=[pl.BlockSpec((tm, f), lambda i, j: (i, 0)),
                  pl.BlockSpec((None, None, f, tn), lambda i, j: (l, s, 0, j)),
                  pl.BlockSpec((tm, tn), lambda i, j: (i, j)),
                  pl.BlockSpec((None, None, 1, tn), lambda i, j: (grp(i), k_gate, 0, j))],
        out_specs=pl.BlockSpec((tm, tn), lambda i, j: (i, j)),
        out_shape=jax.ShapeDtypeStruct((rows, d), F32),
        compiler_params=_cp("arbitrary", "arbitrary"),
        name="ffn_down",
    )(gact, w2, h, modp)


def _in_proj_kernel(x_ref, w_ref, b_ref, o_ref):
    o_ref[...] = _mm(x_ref[...], w_ref[...]) + b_ref[...]


def _in_proj(x, w, b, tm, tn):
    rows, d = x.shape
    n = w.shape[1]
    return pl.pallas_call(
        _in_proj_kernel,
        grid=(rows // tm, n // tn),
        in_specs=[pl.BlockSpec((tm, d), lambda i, j: (i, 0)),
                  pl.BlockSpec((d, tn), lambda i, j: (0, j)),
                  pl.BlockSpec((1, tn), lambda i, j: (0, j))],
        out_specs=pl.BlockSpec((tm, tn), lambda i, j: (i, j)),
        out_shape=jax.ShapeDtypeStruct((rows, n), F32),
        compiler_params=_cp("arbitrary", "arbitrary"),
        name="in_proj",
    )(x, w, b)


def _merge_kernel(ha, hb, hc, hd, ga, gb, gc, gd, wa, wb, wc, wd, o_ref):
    y = jax.nn.sigmoid(ga[...]) * _mm(ha[...], wa[...].astype(BF16))
    y = y + jax.nn.sigmoid(gb[...]) * _mm(hb[...], wb[...].astype(BF16))
    y = y + jax.nn.sigmoid(gc[...]) * _mm(hc[...], wc[...].astype(BF16))
    y = y + jax.nn.sigmoid(gd[...]) * _mm(hd[...], wd[...].astype(BF16))
    o_ref[...] = y.astype(o_ref.dtype)


def _merge(hs, p, br_w, l, rows, tm, tn):
    wbr = hs[0].shape[1]
    d = br_w.shape[-1]
    nj = d // tn
    hspec = pl.BlockSpec((tm, wbr), lambda i, j: (i, 0))
    gspecs = [pl.BlockSpec((tm, tn), functools.partial(lambda i, j, n: (i, n * nj + j), n=n))
              for n in range(4)]
    wspecs = [pl.BlockSpec((None, None, wbr, tn), functools.partial(lambda i, j, n: (l, n, 0, j), n=n))
              for n in range(4)]
    return pl.pallas_call(
        _merge_kernel,
        grid=(rows // tm, nj),
        in_specs=[hspec] * 4 + gspecs + wspecs,
        out_specs=pl.BlockSpec((tm, tn), lambda i, j: (i, j)),
        out_shape=jax.ShapeDtypeStruct((rows, d), BF16),
        compiler_params=_cp("arbitrary", "arbitrary"),
        name="merge",
    )(*hs, p, p, p, p, br_w, br_w, br_w, br_w)


def _out_proj_kernel(y_ref, w_ref, h_ref, gate_ref, o_ref):
    acc = _mm(y_ref[...], w_ref[...].astype(BF16))
    o_ref[...] = h_ref[...] + acc * gate_ref[...]


def _out_proj(y, out_w, h, modp, k_gate, l, tm, tn, grp):
    rows, d = y.shape
    return pl.pallas_call(
        _out_proj_kernel,
        grid=(rows // tm, d // tn),
        in_specs=[pl.BlockSpec((tm, d), lambda i, j: (i, 0)),
                  pl.BlockSpec((None, d, tn), lambda i, j: (l, 0, j)),
                  pl.BlockSpec((tm, tn), lambda i, j: (i, j)),
                  pl.BlockSpec((None, None, 1, tn), lambda i, j: (grp(i), k_gate, 0, j))],
        out_specs=pl.BlockSpec((tm, tn), lambda i, j: (i, j)),
        out_shape=jax.ShapeDtypeStruct((rows, d), F32),
        compiler_params=_cp("arbitrary", "arbitrary"),
        name="out_proj",
    )(y, out_w, h, modp)


def _tile_place(geom):
    n_lat_tiles, lat_tiles, ctx_tiles = geom
    i = pl.program_id(0)
    is_ctx = i >= n_lat_tiles
    per_seq = jnp.where(is_ctx, ctx_tiles, lat_tiles)
    pos = lax.rem(jnp.where(is_ctx, i - n_lat_tiles, i), per_seq)
    return is_ctx, pos == 0, pos == per_seq - 1


def _row_neighbours(xp_ref, x, xn_ref, first, last):
    ts = x.shape[0]
    t = lax.broadcasted_iota(jnp.int32, x.shape, 0)
    prv = jnp.where(t == 0, jnp.where(first, 0.0, xp_ref[ts - 1:ts, :]), pltpu.roll(x, 1, 0))
    nxt = jnp.where(t == ts - 1, jnp.where(last, 0.0, xn_ref[0:1, :]), pltpu.roll(x, ts - 1, 0))
    return prv, nxt


def _shift_kernel(xp_ref, x_ref, xn_ref, mu_ref, o_ref, *, geom):
    is_ctx, first, last = _tile_place(geom)
    x = x_ref[...]
    ts = x.shape[0]
    t = lax.broadcasted_iota(jnp.int32, x.shape, 0)
    lane = lax.broadcasted_iota(jnp.int32, x.shape, 1)
    prv, nxt = _row_neighbours(xp_ref, x, xn_ref, first, last)
    sh_ctx = jnp.where((lane & 1) == 0, prv, nxt)
    tw = t & (GRID_W - 1)
    left = jnp.where(tw == 0, 0.0, prv)
    right = jnp.where(tw == GRID_W - 1, 0.0, nxt)
    up = jnp.concatenate([jnp.where(first, 0.0, xp_ref[ts - GRID_W:ts, :]), x[0:ts - GRID_W]], axis=0)
    down = jnp.concatenate([x[GRID_W:ts], jnp.where(last, 0.0, xn_ref[0:GRID_W, :])], axis=0)
    c4 = lane & 3
    sh_lat = jnp.where(c4 == 0, left, jnp.where(c4 == 1, right, jnp.where(c4 == 2, up, down)))
    sh = jnp.where(is_ctx, sh_ctx, sh_lat)
    o_ref[...] = x + (sh - x) * mu_ref[...]


def _conv_kernel(xp_ref, x_ref, xn_ref, w_ref, b_ref, s_ref, o_ref, *, geom):
    _, first, last = _tile_place(geom)
    x = x_ref[...]
    prv, nxt = _row_neighbours(xp_ref, x, xn_ref, first, last)
    w = w_ref[...]
    y = prv * w[0:1] + x * w[1:2] + nxt * w[2:3] + b_ref[...]
    o_ref[...] = y * jax.nn.sigmoid(y) * s_ref[...]


def _sequence_tiles(body, p, col0, width, tc, small, nb, t_lat, t_ctx, name):
    rows = p.shape[0]
    ts = min(256, t_ctx)
    assert t_lat % ts == 0 and t_ctx % ts == 0 and ts >= 2 * GRID_W
    nt = rows // ts
    geom = (nb * t_lat // ts, t_lat // ts, t_ctx // ts)
    cb = col0 // tc
    small_specs = [pl.BlockSpec((a.shape[0], tc), lambda i, j: (0, j)) for a in small]
    return pl.pallas_call(
        functools.partial(body, geom=geom),
        grid=(nt, width // tc),
        in_specs=[pl.BlockSpec((ts, tc), lambda i, j: (jnp.maximum(i - 1, 0), cb + j)),
                  pl.BlockSpec((ts, tc), lambda i, j: (i, cb + j)),
                  pl.BlockSpec((ts, tc), lambda i, j: (jnp.minimum(i + 1, nt - 1), cb + j))] + small_specs,
        out_specs=pl.BlockSpec((ts, tc), lambda i, j: (i, j)),
        out_shape=jax.ShapeDtypeStruct((rows, width), F32),
        compiler_params=_cp("arbitrary", "arbitrary"),
        name=name,
    )(p, p, p, *small)


def _rwkv_pre_kernel(za_ref, kk_ref, ka_ref, rk_ref, w0_ref, a0_ref, wup_ref, aup_ref, gup_ref, ones_ref,
                     r_o, v_o, kk_o, lwf_o, lwb_o, ktf_o, ktb_o, bf_o, bb_o, bonus_o, g_o):
    wb = r_o.shape[1]
    r = za_ref[:, 0:wb]
    k = za_ref[:, wb:2 * wb]
    v = za_ref[:, 2 * wb:3 * wb]
    wd = jnp.tanh(za_ref[:, 3 * wb:3 * wb + 128])
    ad = za_ref[:, 3 * wb + 128:3 * wb + 256]
    gd = jax.nn.sigmoid(za_ref[:, 3 * wb + 256:3 * wb + 384])
    ones = ones_ref[...]
    kq = k * kk_ref[...]
    kk = kq * lax.rsqrt(jnp.maximum(_seg_sum(kq * kq, ones), 1e-24))
    r_o[...] = r
    v_o[...] = v
    kk_o[...] = kk
    ka = ka_ref[...]
    for d, (lw_o, kt_o, b_o) in enumerate(((lwf_o, ktf_o, bf_o), (lwb_o, ktb_o, bb_o))):
        xw = w0_ref[d:d + 1] + _mm(wd, wup_ref[d])
        lw_o[...] = -jax.nn.sigmoid(xw) * 0.6065306597126334
        a = jax.nn.sigmoid(a0_ref[d:d + 1] + _mm(ad, aup_ref[d]))
        kt_o[...] = k * (1.0 + (a - 1.0) * ka)
        b_o[...] = kk * a
    bonus_o[...] = _seg_sum(r * k * rk_ref[...], ones) * v
    g_o[...] = _mm(gd, gup_ref[...])


def _rwkv_pre(za, prm, tm):
    rows = za.shape[0]
    wb = prm["kk"].shape[1]
    full = lambda a: pl.BlockSpec(a.shape, lambda i: (0,) * a.ndim)
    small = [prm["kk"], prm["ka"], prm["rk"], prm["w0"], prm["a0"], prm["wupp"], prm["aupp"], prm["gup"],
             prm["ones"]]
    ospec = pl.BlockSpec((tm, wb), lambda i: (i, 0))
    return pl.pallas_call(
        _rwkv_pre_kernel,
        grid=(rows // tm,),
        in_specs=[pl.BlockSpec((tm, za.shape[1]), lambda i: (i, 0))] + [full(a) for a in small],
        out_specs=[ospec] * 11,
        out_shape=[jax.ShapeDtypeStruct((rows, wb), F32)] * 11,
        compiler_params=_cp("arbitrary"),
        name="rwkv_pre",
    )(za, *small)


def _rwkv_scan_kernel(rf, vf, kkf, lwf, ktf, bf, rb, vb, kkb, lwb, ktb, bb, yf, yb, s_ref):
    @pl.when(pl.program_id(1) == 0)
    def _():
        s_ref[...] = jnp.zeros_like(s_ref)

    n = rf.shape[0]
    nh = rf.shape[1] // A_HD
    heads = []
    for d, (r_r, v_r, kk_r, lw_r, kt_r, b_r, y_r) in enumerate(((rf, vf, kkf, lwf, ktf, bf, yf),
                                                                 (rb, vb, kkb, lwb, ktb, bb, yb))):
        rev = d == 1
        lw = lw_r[...]
        c = _mm(_tri(n, rev, False).astype(F32), lw, HIGHEST)
        eg = jnp.exp(c)
        ieg = jnp.exp(-c)
        r_all = r_r[...] * eg
        kk_all = kk_r[...] * jnp.exp(c - lw)
        kt_all = kt_r[...] * ieg
        b_all = b_r[...] * ieg
        v_all = v_r[...]
        g_last = eg[0:1] if rev else eg[n - 1:n]
        strict = _tri(n, rev, True)
        incl = _tri(n, rev, False)
        for h in range(nh):
            sl = slice(h * A_HD, (h + 1) * A_HD)
            heads.append(dict(r=r_all[:, sl], kk=kk_all[:, sl], b=b_all[:, sl], kt=kt_all[:, sl], v=v_all[:, sl],
                              s=s_ref[d * nh + h], g=g_last[:, sl], strict=strict, incl=incl, y_ref=y_r, sl=sl,
                              idx=d * nh + h))
    for t in heads:
        t["z"] = _nt(jnp.concatenate([t["kk"], t["r"]], axis=0), jnp.concatenate([t["kt"], t["b"], t["s"]], axis=0))
    for t in heads:
        z = t["z"]
        t["a_kv"] = jnp.where(t["strict"], z[0:n, 0:n], 0.0)
        t["a_kb"] = jnp.where(t["strict"], z[0:n, n:2 * n], 0.0)
        t["r_kv"] = jnp.where(t["incl"], z[n:2 * n, 0:n], 0.0)
        t["r_kb"] = jnp.where(t["incl"], z[n:2 * n, n:2 * n], 0.0)
    for t in heads:
        t["rhs"] = t["z"][0:n, 2 * n:] + _mm(t["a_kv"], t["v"])
        t["pw"] = _mm(t["a_kb"], t["a_kb"])
    for t in heads:
        t["u"] = t["rhs"] - _mm(t["a_kb"], t["rhs"])
    m = 2
    while 2 * m < n:
        for t in heads:
            t["u"] = t["u"] + _mm(t["pw"], t["u"])
            t["pw"] = _mm(t["pw"], t["pw"])
        m *= 2
    for t in heads:
        t["u"] = t["u"] + _mm(t["pw"], t["u"])
    for t in heads:
        y = t["z"][n:2 * n, 2 * n:] + _mm(t["r_kv"], t["v"]) - _mm(t["r_kb"], t["u"])
        t["y_ref"][:, t["sl"]] = y
        s_new = t["s"] + _tn(jnp.concatenate([t["v"], t["u"]], axis=0), jnp.concatenate([t["kt"], -t["b"]], axis=0))
        s_ref[t["idx"]] = s_new * t["g"]


def _chunk_maps(nb, t_lat, t_ctx):
    nlc = t_lat // CHUNK
    ncc = t_ctx // CHUNK

    def fwd(b, j):
        return jnp.where(j < ncc, nb * nlc + b * ncc + j, b * nlc + (j - ncc))

    def rev(b, j):
        return jnp.where(j < ncc, nb * nlc + b * ncc + (ncc - 1 - j), b * nlc + (nlc - 1 - (j - ncc)))

    return fwd, rev, nlc + ncc


def _rwkv_scan(r, v, kk, lwf, lwb, ktf, ktb, bf, bb, nb, t_lat, t_ctx):
    rows, wb = r.shape
    fwd, rev, nch = _chunk_maps(nb, t_lat, t_ctx)
    fs = pl.BlockSpec((CHUNK, wb), lambda b, j: (fwd(b, j), 0))
    rs = pl.BlockSpec((CHUNK, wb), lambda b, j: (rev(b, j), 0))
    return pl.pallas_call(
        _rwkv_scan_kernel,
        grid=(nb, nch),
        in_specs=[fs] * 6 + [rs] * 6,
        out_specs=[fs, rs],
        out_shape=[jax.ShapeDtypeStruct((rows, wb), F32)] * 2,
        scratch_shapes=[pltpu.VMEM((2 * (wb // A_HD), A_HD, A_HD), F32)],
        compiler_params=_cp("arbitrary", "arbitrary"),
        name="rwkv_scan",
    )(r, v, kk, lwf, ktf, bf, r, v, kk, lwb, ktb, bb)


def _mlstm_scan_kernel(qf, kf, vf, gf, qb, kb, vb, gb, gbias_ref, hf, hb, c_ref, n_ref, m_ref):
    @pl.when(pl.program_id(1) == 0)
    def _():
        c_ref[...] = jnp.zeros_like(c_ref)
        n_ref[...] = jnp.zeros_like(n_ref)
        m_ref[...] = jnp.zeros_like(m_ref)

    n = qf.shape[0]
    nh = qf.shape[1] // C_HD
    heads = []
    for d, (q_r, k_r, v_r, g_r, h_r) in enumerate(((qf, kf, vf, gf, hf), (qb, kb, vb, gb, hb))):
        rev = d == 1
        gates = g_r[...] + gbias_ref[...]
        fg = _log_sigmoid(gates)
        mask = _tri(n, rev, False)
        mi = mask.astype(F32)
        bcol = _mm(mi, fg, HIGHEST)
        gates_t = gates.T
        brow = _nt(fg.T, mi, HIGHEST)
        q_all = q_r[...]
        k_all = k_r[...]
        v_all = v_r[...]
        for h in range(nh):
            sl = slice(h * C_HD, (h + 1) * C_HD)
            ii = d * 2 * nh + h
            fi = ii + nh
            idx = d * nh + h
            b_c = bcol[:, fi:fi + 1]
            heads.append(dict(q=q_all[:, sl], k=k_all[:, sl], v=v_all[:, sl], b_c=b_c, b_r=brow[fi:fi + 1, :],
                              i_c=gates[:, ii:ii + 1], i_r=gates_t[ii:ii + 1, :], m=m_ref[idx][:, 0:1],
                              cm=c_ref[idx], nn=n_ref[idx], mask=mask, b_l=b_c[0:1] if rev else b_c[n - 1:n],
                              idx=idx, h_ref=h_r, sl=sl))
    for t in heads:
        t["qk"] = _nt(t["q"], t["k"])
        t["qc"] = _nt(t["q"], t["cm"])
    for t in heads:
        dlog = jnp.where(t["mask"], t["b_c"] - t["b_r"] + t["i_r"], -jnp.inf)
        inter = t["b_c"] + t["m"]
        m_t = jnp.maximum(inter, jnp.max(dlog, axis=1, keepdims=True))
        t["iw"] = jnp.exp(inter - m_t)
        t["m_t"] = m_t
        t["s"] = t["qk"] * jnp.exp(dlog - m_t)
        gl = t["b_l"] - t["b_c"] + t["i_c"]
        m_new = jnp.maximum(t["b_l"] + t["m"], jnp.max(gl, axis=0, keepdims=True))
        t["sw"] = jnp.exp(gl - m_new)
        t["dec"] = jnp.exp(t["b_l"] + t["m"] - m_new)
        t["m_new"] = m_new
    for t in heads:
        t["sv"] = _mm(t["s"], t["v"])
        t["vk"] = _tn(t["v"] * t["sw"], t["k"])
    for t in heads:
        iw = t["iw"]
        num = t["sv"] + iw * t["qc"]
        den = jnp.sum(t["s"], axis=1, keepdims=True) + iw * jnp.sum(t["q"] * t["nn"], axis=1, keepdims=True)
        den = jnp.maximum(jnp.abs(den), jnp.exp(-t["m_t"]))
        t["h_ref"][:, t["sl"]] = num / den
        idx = t["idx"]
        c_ref[idx] = t["dec"] * t["cm"] + t["vk"]
        n_ref[idx] = t["dec"] * t["nn"] + jnp.sum(t["sw"] * t["k"], axis=0, keepdims=True)
        m_ref[idx] = jnp.broadcast_to(t["m_new"], m_ref.shape[1:])


def _mlstm_scan(qk, p, gbias, col_v, col_g, nb, t_lat, t_ctx, wb):
    rows = p.shape[0]
    fwd, rev, nch = _chunk_maps(nb, t_lat, t_ctx)
    nh = wb // C_HD

    def specs(cm):
        return [pl.BlockSpec((CHUNK, wb), lambda b, j: (cm(b, j), 0)),
                pl.BlockSpec((CHUNK, wb), lambda b, j: (cm(b, j), 1)),
                pl.BlockSpec((CHUNK, wb), lambda b, j: (cm(b, j), col_v // wb)),
                pl.BlockSpec((CHUNK, 128), lambda b, j: (cm(b, j), col_g // 128))]

    return pl.pallas_call(
        _mlstm_scan_kernel,
        grid=(nb, nch),
        in_specs=specs(fwd) + specs(rev) + [pl.BlockSpec((1, 128), lambda b, j: (0, 0))],
        out_specs=[pl.BlockSpec((CHUNK, wb), lambda b, j: (fwd(b, j), 0)),
                   pl.BlockSpec((CHUNK, wb), lambda b, j: (rev(b, j), 0))],
        out_shape=[jax.ShapeDtypeStruct((rows, wb), F32)] * 2,
        scratch_shapes=[pltpu.VMEM((2 * nh, C_HD, C_HD), F32), pltpu.VMEM((2 * nh, 1, C_HD), F32),
                        pltpu.VMEM((2 * nh, 1, 128), F32)],
        compiler_params=_cp("arbitrary", "arbitrary"),
        name="mlstm_scan",
    )(qk, qk, p, p, qk, qk, p, p, gbias)


def _gla_scan_kernel(qf, kf, vf, af, qb, kb, vb, ab, aup_ref, abias_ref, of, ob, s_ref):
    @pl.when(pl.program_id(1) == 0)
    def _():
        s_ref[...] = jnp.zeros_like(s_ref)

    n = qf.shape[0]
    nh = qf.shape[1] // D_DK
    nsub = n // D_SUB
    pw = 2 * D_DK
    row = lax.broadcasted_iota(jnp.int32, (n, pw), 0)
    colx = lax.broadcasted_iota(jnp.int32, (n, pw), 1) & (D_DK - 1)
    rel = colx - (row & -D_SUB)
    rin = row & (D_SUB - 1)
    same_head = ((lax.broadcasted_iota(jnp.int32, (pw, pw), 0) & D_DK)
                 == (lax.broadcasted_iota(jnp.int32, (pw, pw), 1) & D_DK))
    ones_pair = jnp.where(same_head, 1.0, 0.0).astype(BF16)
    heads, pairs = [], []
    for d, (q_r, k_r, v_r, a_r, o_r) in enumerate(((qf, kf, vf, af, of), (qb, kb, vb, ab, ob))):
        rev = d == 1
        la = _log_sigmoid(_mm(a_r[...], aup_ref[d]) + abias_ref[d]) * (1.0 / D_TAU)
        bc = _mm(_tri(n, rev, False).astype(F32), la, HIGHEST)
        q_all = q_r[...] * (D_DK ** -0.5)
        k_all = k_r[...]
        v_all = v_r[...]
        for h in range(nh):
            ksl = slice(h * D_DK, (h + 1) * D_DK)
            vsl = slice(h * D_DV, (h + 1) * D_DV)
            heads.append(dict(q=q_all[:, ksl], k=k_all[:, ksl], v=v_all[:, vsl], bc=bc[:, ksl], rev=rev,
                              s=s_ref[d * nh + h], idx=d * nh + h, o_ref=o_r, vsl=vsl))
        for p in range(nh // 2):
            psl = slice(p * pw, (p + 1) * pw)
            pairs.append(dict(q=q_all[:, psl], k=k_all[:, psl], bc=bc[:, psl], rev=rev, heads=(d * nh + 2 * p,
                                                                                              d * nh + 2 * p + 1)))
    for t in heads:
        t["o"] = _nt(t["q"] * jnp.exp(t["bc"]), t["s"])
    for t in heads:
        q, k, bc, rev = t["q"], t["k"], t["bc"], t["rev"]
        pieces = []
        for blk in range(nsub):
            r0 = blk * D_SUB
            edge = bc[r0 + D_SUB - 1:r0 + D_SUB] if rev else bc[r0:r0 + 1]
            has_other = blk < nsub - 1 if rev else blk > 0
            if has_other:
                qs = q[r0:r0 + D_SUB] * jnp.exp(bc[r0:r0 + D_SUB] - edge)
                ks = k * jnp.exp(jnp.minimum(edge - bc, 0.0))
                pieces.append(_nt(qs, ks))
            else:
                pieces.append(jnp.zeros((D_SUB, n), F32))
        t["sc"] = jnp.concatenate(pieces, axis=0)
    for t in pairs:
        q, k, bc = t["q"], t["k"], t["bc"]
        es = []
        for j in range(D_SUB):
            kj = jnp.concatenate([jnp.broadcast_to(k[b * D_SUB + j:b * D_SUB + j + 1], (D_SUB, pw))
                                  for b in range(nsub)], axis=0)
            bj = jnp.concatenate([jnp.broadcast_to(bc[b * D_SUB + j:b * D_SUB + j + 1], (D_SUB, pw))
                                  for b in range(nsub)], axis=0)
            es.append(q * kj * jnp.exp(bc - bj))
        t["e"] = jnp.concatenate(es, axis=0)
    for t in pairs:
        t["c"] = _seg_sum(t["e"], ones_pair)
    for t in pairs:
        rev = t["rev"]
        sc = jnp.concatenate([heads[t["heads"][0]]["sc"], heads[t["heads"][1]]["sc"]], axis=1)
        a = jnp.where((rel >= D_SUB) if rev else (rel < 0), sc, 0.0)
        for j in range(D_SUB):
            keep = (rel == j) & ((rin <= j) if rev else (rin >= j))
            a = jnp.where(keep, t["c"][j * n:(j + 1) * n], a)
        heads[t["heads"][0]]["a"] = a[:, 0:D_DK]
        heads[t["heads"][1]]["a"] = a[:, D_DK:pw]
    for t in heads:
        t["o_ref"][:, t["vsl"]] = t["o"] + _mm(t["a"], t["v"])
        bc = t["bc"]
        b_l = bc[0:1] if t["rev"] else bc[n - 1:n]
        s_ref[t["idx"]] = t["s"] * jnp.exp(b_l) + _tn(t["v"], t["k"] * jnp.exp(b_l - bc))


def _gla_scan(p, aupp, abias, col_q, col_k, col_v, col_a, nb, t_lat, t_ctx, wb):
    rows = p.shape[0]
    fwd, rev, nch = _chunk_maps(nb, t_lat, t_ctx)
    wk = aupp.shape[-1]
    nh = wk // D_DK

    def specs(cm):
        return [pl.BlockSpec((CHUNK, wk), lambda b, j: (cm(b, j), col_q // wk)),
                pl.BlockSpec((CHUNK, wk), lambda b, j: (cm(b, j), col_k // wk)),
                pl.BlockSpec((CHUNK, wb), lambda b, j: (cm(b, j), col_v // wb)),
                pl.BlockSpec((CHUNK, 128), lambda b, j: (cm(b, j), col_a // 128))]

    return pl.pallas_call(
        _gla_scan_kernel,
        grid=(nb, nch),
        in_specs=specs(fwd) + specs(rev) + [pl.BlockSpec(aupp.shape, lambda b, j: (0, 0, 0)),
                                           pl.BlockSpec(abias.shape, lambda b, j: (0, 0, 0))],
        out_specs=[pl.BlockSpec((CHUNK, wb), lambda b, j: (fwd(b, j), 0)),
                   pl.BlockSpec((CHUNK, wb), lambda b, j: (rev(b, j), 0))],
        out_shape=[jax.ShapeDtypeStruct((rows, wb), F32)] * 2,
        scratch_shapes=[pltpu.VMEM((2 * nh, D_DV, D_DK), F32)],
        compiler_params=_cp("arbitrary", "arbitrary"),
        name="gla_scan",
    )(p, p, p, p, p, p, p, p, aupp, abias)


def _branch_out_kernel(ya_f, ya_b, bonus, ga, pb, hc_f, hc_b, oc, od_f, od_b, gdd,
                       a_lng, a_lnb, ones, b_ws, b_bias, b_lng, b_lnb, c_lng, d_lng,
                       hs_a, hs_b, hs_c, hs_d):
    wb = hs_a.shape[1]
    y = ya_f[...] + ya_b[...]
    on = ones[...]
    mu = _seg_sum(y, on) * (1.0 / A_HD)
    yc = y - mu
    var = _seg_sum(yc * yc, on) * (1.0 / A_HD)
    yn = yc * lax.rsqrt(var + A_LN_EPS) * a_lng[...] + a_lnb[...]
    hs_a[...] = ((yn + bonus[...]) * ga[...]).astype(hs_a.dtype)
    z = jax.nn.gelu(pb[...])
    u = z[:, 0:wb]
    vv = z[:, wb:2 * wb]
    mu = jnp.mean(vv, axis=-1, keepdims=True)
    vc = vv - mu
    vn = (vc * lax.rsqrt(jnp.mean(vc * vc, axis=-1, keepdims=True) + 1e-5) * b_lng[...] + b_lnb[...])
    vn = vn.astype(BF16)
    for ck in range(u.shape[0] // B_CHUNK):
        rs = slice(ck * B_CHUNK, (ck + 1) * B_CHUNK)
        for g in range(wb // B_CHUNK):
            cs = slice(g * B_CHUNK, (g + 1) * B_CHUNK)
            s = _mm(b_ws[g].astype(BF16), vn[rs, cs]) + b_bias[:, cs]
            hs_b[rs, cs] = (u[rs, cs] * s).astype(hs_b.dtype)
    hc = hc_f[...] + hc_b[...]
    ogate = jax.nn.sigmoid(oc[...])
    gc = c_lng[...]
    for h in range(wb // C_HD):
        sl = slice(h * C_HD, (h + 1) * C_HD)
        x = hc[:, sl]
        xc = x - jnp.mean(x, axis=-1, keepdims=True)
        xn = xc * lax.rsqrt(jnp.mean(xc * xc, axis=-1, keepdims=True) + C_LN_EPS) * gc[:, sl]
        hs_c[:, sl] = (xn * ogate[:, sl]).astype(hs_c.dtype)
    od = od_f[...] + od_b[...]
    gg = gdd[...]
    gg = gg * jax.nn.sigmoid(gg)
    gd_ = d_lng[...]
    for h in range(wb // D_DV):
        sl = slice(h * D_DV, (h + 1) * D_DV)
        x = od[:, sl]
        xn = x * lax.rsqrt(jnp.mean(x * x, axis=-1, keepdims=True) + D_LN_EPS) * gd_[:, sl]
        hs_d[:, sl] = (xn * gg[:, sl]).astype(hs_d.dtype)


def _branch_out(ya_f, ya_b, bonus, ga, p, hc_f, hc_b, od_f, od_b, prm, cols, rows, tm):
    wb = ya_f.shape[1]
    row = lambda w, cb: pl.BlockSpec((tm, w), lambda i: (i, cb))
    small = [prm["a_lng"], prm["a_lnb"], prm["ones"], prm["b_ws"], prm["b_bias"], prm["b_lng"], prm["b_lnb"],
             prm["c_lng"], prm["d_lng"]]
    full = lambda a: pl.BlockSpec(a.shape, lambda i: (0,) * a.ndim)
    return pl.pallas_call(
        _branch_out_kernel,
        grid=(rows // tm,),
        in_specs=[row(wb, 0)] * 4 + [row(2 * wb, cols["b"] // (2 * wb))] + [row(wb, 0)] * 2
                 + [row(wb, cols["c_o"] // wb)] + [row(wb, 0)] * 2 + [row(wb, cols["d_g"] // wb)]
                 + [full(a) for a in small],
        out_specs=[row(wb, 0)] * 4,
        out_shape=[jax.ShapeDtypeStruct((rows, wb), BF16)] * 4,
        compiler_params=_cp("arbitrary"),
        name="branch_out",
    )(ya_f, ya_b, bonus, ga, p, hc_f, hc_b, p, od_f, od_b, p, *small)


def _pick_tile(*sizes):
    for t in (1024, 512, 256, 128):
        if all(s % t == 0 for s in sizes):
            return t
    raise ValueError("token counts must be multiples of 128")


def _pick_cols(n, cands):
    for t in cands:
        if n % t == 0:
            return t
    raise ValueError(f"no column tile for {n}")


def _proj_cols(d_model):
    wb = d_model // 4
    col = {"g": 0, "b": 4 * d_model, "c_qk": 4 * d_model + 2 * wb}
    col["c_v"] = col["c_qk"] + 2 * wb
    col["c_o"] = col["c_v"] + wb
    col["d_q"] = col["c_o"] + wb
    col["d_k"] = col["d_q"] + wb // 2
    col["d_v"] = col["d_k"] + wb // 2
    col["d_g"] = col["d_v"] + wb
    col["c_gate"] = col["d_g"] + wb
    col["d_a"] = col["c_gate"] + 128
    col["a"] = col["d_a"] + 128
    col["end"] = col["a"] + 3 * wb + 384
    assert col["a"] % 384 == 0
    return col


def _mixers(p, l, nb, t_lat, t_ctx, rows, w):
    wb = w["a_kk"].shape[1]
    col = _proj_cols(4 * wb)
    a_cols = 3 * wb + 384
    row = lambda a: a.reshape(1, -1)
    ones_bd = jnp.kron(jnp.eye(wb // A_HD, dtype=F32), jnp.ones((A_HD, A_HD), F32)).astype(BF16)

    za = _sequence_tiles(_shift_kernel, p, col["a"], a_cols, 384, [row(w["a_mu"][l])], nb, t_lat, t_ctx,
                         "rwkv_shift")
    a_wup, a_aup = w["a_wup"], w["a_aup"]
    wupp = jnp.zeros((2, 128, wb), F32).at[0, 0:64].set(a_wup[l, 0]).at[1, 64:128].set(a_wup[l, 1])
    aupp = jnp.zeros((2, 128, wb), F32).at[0, 0:64].set(a_aup[l, 0]).at[1, 64:128].set(a_aup[l, 1])
    prm_a = {"kk": row(w["a_kk"][l]), "ka": row(w["a_ka"][l]), "rk": row(w["a_rk"][l]), "w0": w["a_w0"][l],
             "a0": w["a_a0"][l], "wupp": wupp, "aupp": aupp, "gup": w["a_gup"][l], "ones": ones_bd}
    r_, v_, kk_, lwf, lwb, ktf, ktb, bf_, bb_, bonus, ga = _rwkv_pre(za, prm_a, 256)
    ya_f, ya_b = _rwkv_scan(r_, v_, kk_, lwf, lwb, ktf, ktb, bf_, bb_, nb, t_lat, t_ctx)

    kscale = jnp.concatenate([jnp.ones((wb,), F32), jnp.full((wb,), C_HD ** -0.5, F32)]).reshape(1, -1)
    qk = _sequence_tiles(_conv_kernel, p, col["c_qk"], 2 * wb, 256,
                         [w["c_conv_w"][l], row(w["c_conv_b"][l]), kscale], nb, t_lat, t_ctx, "mlstm_conv")
    gbias = jnp.zeros((1, 128), F32).at[0, 0:16].set(w["c_gate_b"][l].reshape(-1))
    hc_f, hc_b = _mlstm_scan(qk, p, gbias, col["c_v"], col["c_gate"], nb, t_lat, t_ctx, wb)

    d_aup = w["d_aup"]
    rk_d = d_aup.shape[2]
    aupp_d = (jnp.zeros((2, 128, wb // 2), F32).at[0, 0:rk_d].set(d_aup[l, 0])
              .at[1, rk_d:2 * rk_d].set(d_aup[l, 1]))
    od_f, od_b = _gla_scan(p, aupp_d, w["d_ab"][l].reshape(2, 1, -1), col["d_q"], col["d_k"], col["d_v"],
                           col["d_a"], nb, t_lat, t_ctx, wb)

    b_bias = jnp.repeat(w["b_bs"][l].T, B_CHUNK, axis=1)
    prm_o = {"a_lng": row(w["a_ln_g"][l]), "a_lnb": row(w["a_ln_b"][l]), "ones": ones_bd, "b_ws": w["b_ws"][l],
             "b_bias": b_bias, "b_lng": row(w["b_ln_g"][l]), "b_lnb": row(w["b_ln_b"][l]),
             "c_lng": row(w["c_ln_g"][l]), "d_lng": row(w["d_ln_g"][l])}
    return _branch_out(ya_f, ya_b, bonus, ga, p, hc_f, hc_b, od_f, od_b, prm_o, col, rows, 256)


def kernel(x, c, ctx, c_ctx, ada_w, ada_b, norm_g, ffn_w1, ffn_w3, ffn_w2, in_w, in_b, a_mu, a_w0, a_wup, a_a0, a_aup, a_gup, a_kk, a_ka, a_rk, a_ln_g, a_ln_b, b_ws, b_bs, b_ln_g, b_ln_b, c_conv_w, c_conv_b, c_gate_b, c_ln_g, d_aup, d_ab, d_ln_g, br_w, out_w, final_g):
    weights = dict(a_mu=a_mu, a_w0=a_w0, a_wup=a_wup, a_a0=a_a0, a_aup=a_aup, a_gup=a_gup, a_kk=a_kk, a_ka=a_ka,
                   a_rk=a_rk.reshape(a_rk.shape[0], -1), a_ln_g=a_ln_g, a_ln_b=a_ln_b, b_ws=b_ws, b_bs=b_bs,
                   b_ln_g=b_ln_g, b_ln_b=b_ln_b, c_conv_w=c_conv_w, c_conv_b=c_conv_b, c_gate_b=c_gate_b,
                   c_ln_g=c_ln_g, d_aup=d_aup, d_ab=d_ab, d_ln_g=d_ln_g)
    nb, t_lat, d_model = x.shape
    t_ctx = ctx.shape[1]
    depth = ada_w.shape[0]
    wb = d_model // 4
    d_ff = ffn_w1.shape[-1]
    n_lat = nb * t_lat
    n_ctx = nb * t_ctx
    rows_all = n_lat + n_ctx
    assert t_lat % (GRID_W * 2) == 0 and t_ctx % B_CHUNK == 0 and n_lat % t_ctx == 0
    assert wb == 512 and a_wup.shape[2] == 64 and a_aup.shape[2] == 64 and a_gup.shape[1] == 128

    tm = _pick_tile(t_lat, n_ctx)

    def grp(i):
        return jnp.where(i < n_lat // tm, 1 + i // (t_lat // tm), 0)

    tm_s = _pick_tile(t_lat, n_ctx, 256)
    tf = _pick_cols(d_ff, (512, 256, 128))

    a_cols = 3 * wb + 384
    n_proj = _proj_cols(d_model)["end"]
    o_a, o_b = 0, a_cols
    o_c = o_b + 2 * wb
    o_d = o_c + 4 * wb + 16
    o_g = o_d + 3 * wb + 32
    tn_proj = _pick_cols(n_proj, (1152, 384, 128))

    h = jnp.concatenate([x.reshape(n_lat, d_model), ctx.reshape(n_ctx, d_model)], axis=0)

    m_pad = -(-(nb + 1) // 8) * 8
    cpad = jnp.zeros((m_pad, d_model), F32).at[0].set(c_ctx).at[1:nb + 1].set(c)
    mod_all = _ada_mod(cpad, ada_w, ada_b, _pick_cols(ada_w.shape[-1], (1024, 512, 256, 128)))
    mod_all = mod_all.reshape(depth, m_pad, N_MOD, 1, d_model)

    for l in range(depth):
        modp = mod_all[l]
        last = l == depth - 1

        hn = _norm_mod(h, norm_g[l, 0], modp, 0, rows_all, tm_s, lambda i: jnp.where(
            i < n_lat // tm_s, 1 + i // (t_lat // tm_s), 0))
        gact = _ffn_up(hn, ffn_w1, ffn_w3, l, 0, tm, tf)
        h = _ffn_down(gact, ffn_w2, h, modp, 2, l, 0, tm, 256, grp)

        hn = _norm_mod(h, norm_g[l, 1], modp, 3, rows_all, tm_s, lambda i: jnp.where(
            i < n_lat // tm_s, 1 + i // (t_lat // tm_s), 0))
        w = in_w[l]
        bi = in_b[l]
        z = lambda k: jnp.zeros((d_model, k), F32)
        zb = lambda k: jnp.zeros((k,), F32)
        w_perm = jnp.concatenate([
            w[:, o_g:o_g + 4 * d_model], w[:, o_b:o_b + 2 * wb], w[:, o_c:o_c + 4 * wb],
            w[:, o_d:o_d + 3 * wb], w[:, o_c + 4 * wb:o_c + 4 * wb + 16], z(112),
            w[:, o_d + 3 * wb:o_d + 3 * wb + 32], z(96), w[:, o_a:o_a + a_cols]], axis=1).astype(BF16)
        b_perm = jnp.concatenate([
            bi[o_g:o_g + 4 * d_model], bi[o_b:o_b + 2 * wb], bi[o_c:o_c + 4 * wb],
            bi[o_d:o_d + 3 * wb], bi[o_c + 4 * wb:o_c + 4 * wb + 16], zb(112),
            bi[o_d + 3 * wb:o_d + 3 * wb + 32], zb(96), bi[o_a:o_a + a_cols]]).reshape(1, n_proj)
        p = _in_proj(hn, w_perm, b_perm, tm, tn_proj)

        rows = n_lat if last else rows_all
        hs = _mixers(p, l, nb, t_lat, t_ctx, rows, weights)
        y = _merge(hs, p, br_w, l, rows, tm, 512)
        h = _out_proj(y, out_w, h, modp, 5, l, tm, 512, grp)

        hn = _norm_mod(h, norm_g[l, 2], modp, 6, rows, tm_s, lambda i: jnp.where(
            i < n_lat // tm_s, 1 + i // (t_lat // tm_s), 0))
        gact = _ffn_up(hn, ffn_w1, ffn_w3, l, 1, tm, tf)
        h = _ffn_down(gact, ffn_w2, h, modp, 8, l, 1, tm, 256, grp)

    out = _final_norm(h, final_g, tm_s)
    return out.reshape(nb, t_lat, d_model)
```

```python
import functools

import jax
import jax.numpy as jnp
from jax import lax
from jax.experimental import pallas as pl
from jax.experimental.pallas import tpu as pltpu

F32 = jnp.float32
BF16 = jnp.bfloat16
HIGHEST = lax.Precision.HIGHEST

EPS = 1e-6
GRID_W = 64
N_MOD = 9
CHUNK = 64
A_HD = 64
A_LN_EPS = 64e-5
B_CHUNK = 128
C_HD = 128
C_CHUNK = 128
C_LN_EPS = 1e-5
D_DK = 64
D_DV = 128
D_TAU = 16.0
D_SUB = 16
D_LN_EPS = 1e-6
VMEM_LIMIT = 56 * 1024 * 1024


def _cp(*sem):
    return pltpu.CompilerParams(dimension_semantics=sem, vmem_limit_bytes=VMEM_LIMIT)


def _mm(a, b, precision=None):
    return jnp.dot(a, b, precision=precision, preferred_element_type=F32)


def _nt(a, b, precision=None):
    return lax.dot_general(a, b, (((1,), (1,)), ((), ())), precision=precision,
                           preferred_element_type=F32)


def _tn(a, b, precision=None):
    return lax.dot_general(a, b, (((0,), (0,)), ((), ())), precision=precision,
                           preferred_element_type=F32)


def _log_sigmoid(x):
    return jnp.minimum(x, 0.0) - jnp.log(1.0 + jnp.exp(-jnp.abs(x)))


def _seg_sum(x, ones_blockdiag):
    hi = x.astype(BF16)
    lo = (x - hi.astype(F32)).astype(BF16)
    return _mm(hi, ones_blockdiag) + _mm(lo, ones_blockdiag)


def _tri(n, rev, strict):
    row = lax.broadcasted_iota(jnp.int32, (n, n), 0)
    col = lax.broadcasted_iota(jnp.int32, (n, n), 1)
    if rev:
        return (col > row) if strict else (col >= row)
    return (col < row) if strict else (col <= row)


def _norm_mod_kernel(h_ref, g_ref, sh_ref, sc_ref, o_ref):
    x = h_ref[...]
    y = x * lax.rsqrt(jnp.mean(x * x, axis=-1, keepdims=True) + EPS) * g_ref[...]
    o_ref[...] = (y * (1.0 + sc_ref[...]) + sh_ref[...]).astype(o_ref.dtype)


def _norm_mod(h, g, modp, k_shift, rows, tm, grp):
    d = h.shape[1]
    return pl.pallas_call(
        _norm_mod_kernel,
        grid=(rows // tm,),
        in_specs=[pl.BlockSpec((tm, d), lambda i: (i, 0)),
                  pl.BlockSpec((1, d), lambda i: (0, 0)),
                  pl.BlockSpec((None, None, 1, d), lambda i: (grp(i), k_shift, 0, 0)),
                  pl.BlockSpec((None, None, 1, d), lambda i: (grp(i), k_shift + 1, 0, 0))],
        out_specs=pl.BlockSpec((tm, d), lambda i: (i, 0)),
        out_shape=jax.ShapeDtypeStruct((rows, d), BF16),
        compiler_params=_cp("arbitrary"),
        name="norm_mod",
    )(h, g.reshape(1, d), modp, modp)


def _final_norm_kernel(h_ref, g_ref, o_ref):
    x = h_ref[...]
    o_ref[...] = x * lax.rsqrt(jnp.mean(x * x, axis=-1, keepdims=True) + EPS) * g_ref[...]


def _final_norm(h, g, tm):
    rows, d = h.shape
    return pl.pallas_call(
        _final_norm_kernel,
        grid=(rows // tm,),
        in_specs=[pl.BlockSpec((tm, d), lambda i: (i, 0)), pl.BlockSpec((1, d), lambda i: (0, 0))],
        out_specs=pl.BlockSpec((tm, d), lambda i: (i, 0)),
        out_shape=jax.ShapeDtypeStruct((rows, d), F32),
        compiler_params=_cp("arbitrary"),
        name="final_norm",
    )(h, g.reshape(1, d))


def _ada_kernel(c_ref, w_ref, b_ref, o_ref):
    c = c_ref[...]
    cond = (c * jax.nn.sigmoid(c)).astype(BF16)
    o_ref[...] = _mm(cond, w_ref[...].astype(BF16)) + b_ref[...]


def _ada_mod(cpad, ada_w, ada_b, tn):
    depth, d, n = ada_w.shape
    m = cpad.shape[0]
    return pl.pallas_call(
        _ada_kernel,
        grid=(depth, n // tn),
        in_specs=[pl.BlockSpec((m, d), lambda l, j: (0, 0)),
                  pl.BlockSpec((None, d, tn), lambda l, j: (l, 0, j)),
                  pl.BlockSpec((None, 1, tn), lambda l, j: (l, 0, j))],
        out_specs=pl.BlockSpec((None, m, tn), lambda l, j: (l, 0, j)),
        out_shape=jax.ShapeDtypeStruct((depth, m, n), F32),
        compiler_params=_cp("arbitrary", "arbitrary"),
        name="ada_mod",
    )(cpad, ada_w, ada_b.reshape(depth, 1, n))


def _ffn_up_kernel(x_ref, w1_ref, w3_ref, o_ref):
    x = x_ref[...]
    a = _mm(x, w1_ref[...].astype(BF16))
    b = _mm(x, w3_ref[...].astype(BF16))
    o_ref[...] = (a * jax.nn.sigmoid(a) * b).astype(o_ref.dtype)


def _ffn_up(x, w1, w3, l, s, tm, tf):
    rows, d = x.shape
    f = w1.shape[-1]
    wspec = pl.BlockSpec((None, None, d, tf), lambda i, j: (l, s, 0, j))
    return pl.pallas_call(
        _ffn_up_kernel,
        grid=(rows // tm, f // tf),
        in_specs=[pl.BlockSpec((tm, d), lambda i, j: (i, 0)), wspec, wspec],
        out_specs=pl.BlockSpec((tm, tf), lambda i, j: (i, j)),
        out_shape=jax.ShapeDtypeStruct((rows, f), BF16),
        compiler_params=_cp("arbitrary", "arbitrary"),
        name="ffn_up",
    )(x, w1, w3)


def _ffn_down_kernel(g_ref, w_ref, h_ref, gate_ref, o_ref):
    acc = _mm(g_ref[...], w_ref[...].astype(BF16))
    o_ref[...] = h_ref[...] + (0.5 * acc) * gate_ref[...]


def _ffn_down(gact, w2, h, modp, k_gate, l, s, tm, tn, grp):
    rows, f = gact.shape
    d = h.shape[1]
    return pl.pallas_call(
        _ffn_down_kernel,
        grid=(rows // tm, d // tn),
        in_specs=[pl.BlockSpec((tm, f), lambda i, j: (i, 0)),
                  pl.BlockSpec((None, None, f, tn), lambda i, j: (l, s, 0, j)),
                  pl.BlockSpec((tm, tn), lambda i, j: (i, j)),
                  pl.BlockSpec((None, None, 1, tn), lambda i, j: (grp(i), k_gate, 0, j))],
        out_specs=pl.BlockSpec((tm, tn), lambda i, j: (i, j)),
        out_shape=jax.ShapeDtypeStruct((rows, d), F32),
        compiler_params=_cp("arbitrary", "arbitrary"),
        name="ffn_down",
    )(gact, w2, h, modp)


def _in_proj_kernel(x_ref, w_ref, b_ref, o_ref):
    o_ref[...] = _mm(x_ref[...], w_ref[...]) + b_ref[...]


def _gate_proj_kernel(x_ref, w_ref, b_ref, o_ref):
    o_ref[...] = jax.nn.sigmoid(_mm(x_ref[...], w_ref[...]) + b_ref[...]).astype(o_ref.dtype)


def _in_proj(x, w, b, tm, tn, gates=False):
    rows, d = x.shape
    n = w.shape[1]
    return pl.pallas_call(
        _gate_proj_kernel if gates else _in_proj_kernel,
        grid=(rows // tm, n // tn),
        in_specs=[pl.BlockSpec((tm, d), lambda i, j: (i, 0)),
                  pl.BlockSpec((d, tn), lambda i, j: (0, j)),
                  pl.BlockSpec((1, tn), lambda i, j: (0, j))],
        out_specs=pl.BlockSpec((tm, tn), lambda i, j: (i, j)),
        out_shape=jax.ShapeDtypeStruct((rows, n), BF16 if gates else F32),
        compiler_params=_cp("arbitrary", "arbitrary"),
        name="gate_proj" if gates else "in_proj",
    )(x, w, b)


def _merge_kernel(ha, hb, hc, hd, ga, gb, gc, gd, wa, wb, wc, wd, o_ref):
    y = ga[...].astype(F32) * _mm(ha[...], wa[...].astype(BF16))
    y = y + gb[...].astype(F32) * _mm(hb[...], wb[...].astype(BF16))
    y = y + gc[...].astype(F32) * _mm(hc[...], wc[...].astype(BF16))
    y = y + gd[...].astype(F32) * _mm(hd[...], wd[...].astype(BF16))
    o_ref[...] = y.astype(o_ref.dtype)


def _merge(hs, gates, br_w, l, rows, tm, tn):
    wbr = hs[0].shape[1]
    d = br_w.shape[-1]
    nj = d // tn
    hspec = pl.BlockSpec((tm, wbr), lambda i, j: (i, 0))
    gspecs = [pl.BlockSpec((tm, tn), functools.partial(lambda i, j, n: (i, n * nj + j), n=n))
              for n in range(4)]
    wspecs = [pl.BlockSpec((None, None, wbr, tn), functools.partial(lambda i, j, n: (l, n, 0, j), n=n))
              for n in range(4)]
    return pl.pallas_call(
        _merge_kernel,
        grid=(rows // tm, nj),
        in_specs=[hspec] * 4 + gspecs + wspecs,
        out_specs=pl.BlockSpec((tm, tn), lambda i, j: (i, j)),
        out_shape=jax.ShapeDtypeStruct((rows, d), BF16),
        compiler_params=_cp("arbitrary", "arbitrary"),
        name="merge",
    )(*hs, gates, gates, gates, gates, br_w, br_w, br_w, br_w)


def _out_proj_kernel(y_ref, w_ref, h_ref, gate_ref, o_ref):
    acc = _mm(y_ref[...], w_ref[...].astype(BF16))
    o_ref[...] = h_ref[...] + acc * gate_ref[...]


def _out_proj(y, out_w, h, modp, k_gate, l, tm, tn, grp):
    rows, d = y.shape
    return pl.pallas_call(
        _out_proj_kernel,
        grid=(rows // tm, d // tn),
        in_specs=[pl.BlockSpec((tm, d), lambda i, j: (i, 0)),
                  pl.BlockSpec((None, d, tn), lambda i, j: (l, 0, j)),
                  pl.BlockSpec((tm, tn), lambda i, j: (i, j)),
                  pl.BlockSpec((None, None, 1, tn), lambda i, j: (grp(i), k_gate, 0, j))],
        out_specs=pl.BlockSpec((tm, tn), lambda i, j: (i, j)),
        out_shape=jax.ShapeDtypeStruct((rows, d), F32),
        compiler_params=_cp("arbitrary", "arbitrary"),
        name="out_proj",
    )(y, out_w, h, modp)


def _tile_place(geom):
    n_lat_tiles, lat_tiles, ctx_tiles = geom
    i = pl.program_id(0)
    is_ctx = i >= n_lat_tiles
    per_seq = jnp.where(is_ctx, ctx_tiles, lat_tiles)
    pos = lax.rem(jnp.where(is_ctx, i - n_lat_tiles, i), per_seq)
    return is_ctx, pos == 0, pos == per_seq - 1


def _row_neighbours(xp_ref, x, xn_ref, first, last):
    ts = x.shape[0]
    hp = xp_ref.shape[0]
    t = lax.broadcasted_iota(jnp.int32, x.shape, 0)
    prv = jnp.where(t == 0, jnp.where(first, 0.0, xp_ref[hp - 1:hp, :]), pltpu.roll(x, 1, 0))
    nxt = jnp.where(t == ts - 1, jnp.where(last, 0.0, xn_ref[0:1, :]), pltpu.roll(x, ts - 1, 0))
    return prv, nxt


def _token_shift(xp_ref, x_ref, xn_ref, mu_ref, geom):
    is_ctx, first, last = _tile_place(geom)
    x = x_ref[...]
    ts = x.shape[0]
    t = lax.broadcasted_iota(jnp.int32, x.shape, 0)
    lane = lax.broadcasted_iota(jnp.int32, x.shape, 1)
    prv, nxt = _row_neighbours(xp_ref, x, xn_ref, first, last)
    sh_ctx = jnp.where((lane & 1) == 0, prv, nxt)
    tw = t & (GRID_W - 1)
    left = jnp.where(tw == 0, 0.0, prv)
    right = jnp.where(tw == GRID_W - 1, 0.0, nxt)
    up = jnp.concatenate([jnp.where(first, 0.0, xp_ref[...]), x[0:ts - GRID_W]], axis=0)
    down = jnp.concatenate([x[GRID_W:ts], jnp.where(last, 0.0, xn_ref[...])], axis=0)
    c4 = lane & 3
    sh_lat = jnp.where(c4 == 0, left, jnp.where(c4 == 1, right, jnp.where(c4 == 2, up, down)))
    sh = jnp.where(is_ctx, sh_ctx, sh_lat)
    return x + (sh - x) * mu_ref[...]


def _conv_kernel(xp_ref, x_ref, xn_ref, w_ref, b_ref, s_ref, o_ref, *, geom):
    _, first, last = _tile_place(geom)
    x = x_ref[...]
    prv, nxt = _row_neighbours(xp_ref, x, xn_ref, first, last)
    w = w_ref[...]
    y = prv * w[0:1] + x * w[1:2] + nxt * w[2:3] + b_ref[...]
    o_ref[...] = y * jax.nn.sigmoid(y) * s_ref[...]


def _sequence_tiles(body, p, col0, width, halo, small, out_widths, nb, t_lat, t_ctx, name):
    rows = p.shape[0]
    ts = min(256, t_ctx)
    assert t_lat % ts == 0 and t_ctx % ts == 0 and ts >= 2 * GRID_W and ts % halo == 0 and col0 % width == 0
    nt = rows // ts
    nh = rows // halo
    r = ts // halo
    geom = (nb * t_lat // ts, t_lat // ts, t_ctx // ts)
    cb = col0 // width
    full = lambda a: pl.BlockSpec(a.shape, lambda i: (0,) * a.ndim)
    return pl.pallas_call(
        functools.partial(body, geom=geom),
        grid=(nt,),
        in_specs=[pl.BlockSpec((halo, width), lambda i: (jnp.maximum(i * r - 1, 0), cb)),
                  pl.BlockSpec((ts, width), lambda i: (i, cb)),
                  pl.BlockSpec((halo, width), lambda i: (jnp.minimum((i + 1) * r, nh - 1), cb))]
                 + [full(a) for a in small],
        out_specs=[pl.BlockSpec((ts, ow), lambda i: (i, 0)) for ow in out_widths],
        out_shape=[jax.ShapeDtypeStruct((rows, ow), F32) for ow in out_widths],
        compiler_params=_cp("arbitrary"),
        name=name,
    )(p, p, p, *small)


def _rwkv_pre_kernel(xp_ref, x_ref, xn_ref, mu_ref, kk_ref, ka_ref, rk_ref, w0_ref, a0_ref, wup_ref, aup_ref,
                     gup_ref, ones_ref, r_o, v_o, kk_o, lwf_o, lwb_o, ktf_o, ktb_o, bf_o, bb_o, bonus_o, g_o, *,
                     geom):
    wb = r_o.shape[1]
    za = _token_shift(xp_ref, x_ref, xn_ref, mu_ref, geom)
    r = za[:, 0:wb]
    k = za[:, wb:2 * wb]
    v = za[:, 2 * wb:3 * wb]
    wd = jnp.tanh(za[:, 3 * wb:3 * wb + 128])
    ad = za[:, 3 * wb + 128:3 * wb + 256]
    gd = jax.nn.sigmoid(za[:, 3 * wb + 256:3 * wb + 384])
    ones = ones_ref[...]
    kq = k * kk_ref[...]
    kk = kq * lax.rsqrt(jnp.maximum(_seg_sum(kq * kq, ones), 1e-24))
    r_o[...] = r
    v_o[...] = v
    kk_o[...] = kk
    ka = ka_ref[...]
    for d, (lw_o, kt_o, b_o) in enumerate(((lwf_o, ktf_o, bf_o), (lwb_o, ktb_o, bb_o))):
        xw = w0_ref[d:d + 1] + _mm(wd, wup_ref[d])
        lw_o[...] = -jax.nn.sigmoid(xw) * 0.6065306597126334
        a = jax.nn.sigmoid(a0_ref[d:d + 1] + _mm(ad, aup_ref[d]))
        kt_o[...] = k * (1.0 + (a - 1.0) * ka)
        b_o[...] = kk * a
    bonus_o[...] = _seg_sum(r * k * rk_ref[...], ones) * v
    g_o[...] = _mm(gd, gup_ref[...])


def _rwkv_scan_kernel(rf, vf, kkf, lwf, ktf, bf, rb, vb, kkb, lwb, ktb, bb, yf, yb, s_ref):
    @pl.when(pl.program_id(1) == 0)
    def _():
        s_ref[...] = jnp.zeros_like(s_ref)

    n = rf.shape[0]
    nh = rf.shape[1] // A_HD
    heads = []
    for d, (r_r, v_r, kk_r, lw_r, kt_r, b_r, y_r) in enumerate(((rf, vf, kkf, lwf, ktf, bf, yf),
                                                                 (rb, vb, kkb, lwb, ktb, bb, yb))):
        rev = d == 1
        lw = lw_r[...]
        c = _mm(_tri(n, rev, False).astype(F32), lw, HIGHEST)
        eg = jnp.exp(c)
        ieg = jnp.exp(-c)
        r_all = r_r[...] * eg
        kk_all = kk_r[...] * jnp.exp(c - lw)
        kt_all = kt_r[...] * ieg
        b_all = b_r[...] * ieg
        v_all = v_r[...]
        g_last = eg[0:1] if rev else eg[n - 1:n]
        strict = _tri(n, rev, True)
        incl = _tri(n, rev, False)
        for h in range(nh):
            sl = slice(h * A_HD, (h + 1) * A_HD)
            heads.append(dict(r=r_all[:, sl], kk=kk_all[:, sl], b=b_all[:, sl], kt=kt_all[:, sl], v=v_all[:, sl],
                              s=s_ref[d * nh + h], g=g_last[:, sl], strict=strict, incl=incl, y_ref=y_r, sl=sl,
                              idx=d * nh + h))
    for t in heads:
        t["z"] = _nt(jnp.concatenate([t["kk"], t["r"]], axis=0), jnp.concatenate([t["kt"], t["b"], t["s"]], axis=0))
    for t in heads:
        z = t["z"]
        t["a_kv"] = jnp.where(t["strict"], z[0:n, 0:n], 0.0)
        t["a_kb"] = jnp.where(t["strict"], z[0:n, n:2 * n], 0.0)
        t["r_kv"] = jnp.where(t["incl"], z[n:2 * n, 0:n], 0.0)
        t["r_kb"] = jnp.where(t["incl"], z[n:2 * n, n:2 * n], 0.0)
    for t in heads:
        t["av"] = _mm(jnp.concatenate([t["a_kv"], t["r_kv"]], axis=0), t["v"])
    left = lax.broadcasted_iota(jnp.int32, (n, 2 * n), 1) < n
    for t in heads:
        t["w"] = jnp.concatenate([t["a_kb"], t["z"][0:n, 2 * n:] + t["av"][0:n]], axis=1)
    for t in heads:
        r = _mm(t["a_kb"], t["w"])
        t["w"] = jnp.where(left, r, t["w"] - r)
    m = 2
    while m < n:
        for t in heads:
            r = _mm(t["w"][:, 0:n], t["w"])
            t["w"] = jnp.where(left, r, t["w"] + r)
        m *= 2
    for t in heads:
        t["u"] = t["w"][:, n:2 * n]
        y = t["z"][n:2 * n, 2 * n:] + t["av"][n:2 * n] - _mm(t["r_kb"], t["u"])
        t["y_ref"][:, t["sl"]] = y
        s_new = t["s"] + _tn(jnp.concatenate([t["v"], t["u"]], axis=0), jnp.concatenate([t["kt"], -t["b"]], axis=0))
        s_ref[t["idx"]] = s_new * t["g"]


def _chunk_maps(nb, t_lat, t_ctx, chunk=CHUNK):
    nlc = t_lat // chunk
    ncc = t_ctx // chunk

    def fwd(b, j):
        return jnp.where(j < ncc, nb * nlc + b * ncc + j, b * nlc + (j - ncc))

    def rev(b, j):
        return jnp.where(j < ncc, nb * nlc + b * ncc + (ncc - 1 - j), b * nlc + (nlc - 1 - (j - ncc)))

    return fwd, rev, nlc + ncc


def _rwkv_scan(r, v, kk, lwf, lwb, ktf, ktb, bf, bb, nb, t_lat, t_ctx):
    rows, wb = r.shape
    fwd, rev, nch = _chunk_maps(nb, t_lat, t_ctx)
    fs = pl.BlockSpec((CHUNK, wb), lambda b, j: (fwd(b, j), 0))
    rs = pl.BlockSpec((CHUNK, wb), lambda b, j: (rev(b, j), 0))
    return pl.pallas_call(
        _rwkv_scan_kernel,
        grid=(nb, nch),
        in_specs=[fs] * 6 + [rs] * 6,
        out_specs=[fs, rs],
        out_shape=[jax.ShapeDtypeStruct((rows, wb), F32)] * 2,
        scratch_shapes=[pltpu.VMEM((2 * (wb // A_HD), A_HD, A_HD), F32)],
        compiler_params=_cp("arbitrary", "arbitrary"),
        name="rwkv_scan",
    )(r, v, kk, lwf, ktf, bf, r, v, kk, lwb, ktb, bb)


def _mlstm_scan_kernel(qf, kf, vf, gf, qb, kb, vb, gb, gbias_ref, hf, hb, c_ref, n_ref, m_ref):
    @pl.when(pl.program_id(1) == 0)
    def _():
        c_ref[...] = jnp.zeros_like(c_ref)
        n_ref[...] = jnp.zeros_like(n_ref)
        m_ref[...] = jnp.zeros_like(m_ref)

    n = qf.shape[0]
    nh = qf.shape[1] // C_HD
    heads = []
    for d, (q_r, k_r, v_r, g_r, h_r) in enumerate(((qf, kf, vf, gf, hf), (qb, kb, vb, gb, hb))):
        rev = d == 1
        gates = g_r[...] + gbias_ref[...]
        fg = _log_sigmoid(gates)
        mask = _tri(n, rev, False)
        mi = mask.astype(F32)
        bcol = _mm(mi, fg, HIGHEST)
        gates_t = gates.T
        brow = _nt(fg.T, mi, HIGHEST)
        q_all = q_r[...]
        k_all = k_r[...]
        v_all = v_r[...]
        for h in range(nh):
            sl = slice(h * C_HD, (h + 1) * C_HD)
            ii = d * 2 * nh + h
            fi = ii + nh
            idx = d * nh + h
            b_c = bcol[:, fi:fi + 1]
            heads.append(dict(q=q_all[:, sl], k=k_all[:, sl], v=v_all[:, sl], b_c=b_c, b_r=brow[fi:fi + 1, :],
                              i_c=gates[:, ii:ii + 1], i_r=gates_t[ii:ii + 1, :], m=m_ref[idx][:, 0:1],
                              cm=c_ref[idx], nn=n_ref[idx], mask=mask, b_l=b_c[0:1] if rev else b_c[n - 1:n],
                              idx=idx, h_ref=h_r, sl=sl))
    for t in heads:
        t["qk"] = _nt(t["q"], t["k"])
        t["qc"] = _nt(t["q"], t["cm"])
    for t in heads:
        dlog = jnp.where(t["mask"], t["b_c"] - t["b_r"] + t["i_r"], -jnp.inf)
        inter = t["b_c"] + t["m"]
        m_t = jnp.maximum(inter, jnp.max(dlog, axis=1, keepdims=True))
        t["iw"] = jnp.exp(inter - m_t)
        t["m_t"] = m_t
        t["s"] = t["qk"] * jnp.exp(dlog - m_t)
        gl = t["b_l"] - t["b_c"] + t["i_c"]
        m_new = jnp.maximum(t["b_l"] + t["m"], jnp.max(gl, axis=0, keepdims=True))
        t["sw"] = jnp.exp(gl - m_new)
        t["dec"] = jnp.exp(t["b_l"] + t["m"] - m_new)
        t["m_new"] = m_new
    for t in heads:
        t["sv"] = _mm(t["s"], t["v"])
        t["vk"] = _tn(t["v"] * t["sw"], t["k"])
    for t in heads:
        t["rs"] = jnp.sum(t["s"], axis=1, keepdims=True)
        t["qn"] = jnp.sum(t["q"] * t["nn"], axis=1, keepdims=True)
        t["ks"] = jnp.sum(t["sw"] * t["k"], axis=0, keepdims=True)
    for t in heads:
        iw = t["iw"]
        num = t["sv"] + iw * t["qc"]
        den = jnp.maximum(jnp.abs(t["rs"] + iw * t["qn"]), jnp.exp(-t["m_t"]))
        t["h_ref"][:, t["sl"]] = num / den
        idx = t["idx"]
        c_ref[idx] = t["dec"] * t["cm"] + t["vk"]
        n_ref[idx] = t["dec"] * t["nn"] + t["ks"]
        m_ref[idx] = jnp.broadcast_to(t["m_new"], m_ref.shape[1:])


def _mlstm_scan(qk, p, gbias, col_v, col_g, nb, t_lat, t_ctx, wb):
    rows = p.shape[0]
    ck = C_CHUNK if t_ctx % C_CHUNK == 0 else CHUNK
    fwd, rev, nch = _chunk_maps(nb, t_lat, t_ctx, ck)
    nh = wb // C_HD

    def specs(cm):
        return [pl.BlockSpec((ck, wb), lambda b, j: (cm(b, j), 0)),
                pl.BlockSpec((ck, wb), lambda b, j: (cm(b, j), 1)),
                pl.BlockSpec((ck, wb), lambda b, j: (cm(b, j), col_v // wb)),
                pl.BlockSpec((ck, 128), lambda b, j: (cm(b, j), col_g // 128))]

    return pl.pallas_call(
        _mlstm_scan_kernel,
        grid=(nb, nch),
        in_specs=specs(fwd) + specs(rev) + [pl.BlockSpec((1, 128), lambda b, j: (0, 0))],
        out_specs=[pl.BlockSpec((ck, wb), lambda b, j: (fwd(b, j), 0)),
                   pl.BlockSpec((ck, wb), lambda b, j: (rev(b, j), 0))],
        out_shape=[jax.ShapeDtypeStruct((rows, wb), F32)] * 2,
        scratch_shapes=[pltpu.VMEM((2 * nh, C_HD, C_HD), F32), pltpu.VMEM((2 * nh, 1, C_HD), F32),
                        pltpu.VMEM((2 * nh, 1, 128), F32)],
        compiler_params=_cp("arbitrary", "arbitrary"),
        name="mlstm_scan",
    )(qk, qk, p, p, qk, qk, p, p, gbias)


def _gla_scan_kernel(qf, kf, vf, af, qb, kb, vb, ab, aup_ref, abias_ref, of, ob, s_ref):
    @pl.when(pl.program_id(1) == 0)
    def _():
        s_ref[...] = jnp.zeros_like(s_ref)

    n = qf.shape[0]
    nh = qf.shape[1] // D_DK
    nsub = n // D_SUB
    pw = 2 * D_DK
    row = lax.broadcasted_iota(jnp.int32, (n, pw), 0)
    colx = lax.broadcasted_iota(jnp.int32, (n, pw), 1) & (D_DK - 1)
    rel = colx - (row & -D_SUB)
    rin = row & (D_SUB - 1)
    same_head = ((lax.broadcasted_iota(jnp.int32, (pw, pw), 0) & D_DK)
                 == (lax.broadcasted_iota(jnp.int32, (pw, pw), 1) & D_DK))
    ones_pair = jnp.where(same_head, 1.0, 0.0).astype(BF16)
    heads, pairs = [], []
    for d, (q_r, k_r, v_r, a_r, o_r) in enumerate(((qf, kf, vf, af, of), (qb, kb, vb, ab, ob))):
        rev = d == 1
        la = _log_sigmoid(_mm(a_r[...], aup_ref[d]) + abias_ref[d]) * (1.0 / D_TAU)
        bc = _mm(_tri(n, rev, False).astype(F32), la, HIGHEST)
        q_all = q_r[...] * (D_DK ** -0.5)
        k_all = k_r[...]
        v_all = v_r[...]
        for h in range(nh):
            ksl = slice(h * D_DK, (h + 1) * D_DK)
            vsl = slice(h * D_DV, (h + 1) * D_DV)
            heads.append(dict(q=q_all[:, ksl], k=k_all[:, ksl], v=v_all[:, vsl], bc=bc[:, ksl], rev=rev,
                              s=s_ref[d * nh + h], idx=d * nh + h, o_ref=o_r, vsl=vsl))
        for p in range(nh // 2):
            psl = slice(p * pw, (p + 1) * pw)
            pairs.append(dict(q=q_all[:, psl], k=k_all[:, psl], bc=bc[:, psl], rev=rev, heads=(d * nh + 2 * p,
                                                                                              d * nh + 2 * p + 1)))
    for t in heads:
        t["o"] = _nt(t["q"] * jnp.exp(t["bc"]), t["s"])
    for t in heads:
        q, k, bc, rev = t["q"], t["k"], t["bc"], t["rev"]
        pieces = []
        for blk in range(nsub):
            r0 = blk * D_SUB
            edge = bc[r0 + D_SUB - 1:r0 + D_SUB] if rev else bc[r0:r0 + 1]
            has_other = blk < nsub - 1 if rev else blk > 0
            if has_other:
                qs = q[r0:r0 + D_SUB] * jnp.exp(bc[r0:r0 + D_SUB] - edge)
                ks = k * jnp.exp(jnp.minimum(edge - bc, 0.0))
                pieces.append(_nt(qs, ks))
            else:
                pieces.append(jnp.zeros((D_SUB, n), F32))
        t["sc"] = jnp.concatenate(pieces, axis=0)
    for t in pairs:
        q, k, bc = t["q"], t["k"], t["bc"]
        es = []
        for j in range(D_SUB):
            kj = jnp.concatenate([jnp.broadcast_to(k[b * D_SUB + j:b * D_SUB + j + 1], (D_SUB, pw))
                                  for b in range(nsub)], axis=0)
            bj = jnp.concatenate([jnp.broadcast_to(bc[b * D_SUB + j:b * D_SUB + j + 1], (D_SUB, pw))
                                  for b in range(nsub)], axis=0)
            es.append(q * kj * jnp.exp(bc - bj))
        t["e"] = jnp.concatenate(es, axis=0)
    for t in pairs:
        t["c"] = _seg_sum(t["e"], ones_pair)
    for t in pairs:
        rev = t["rev"]
        sc = jnp.concatenate([heads[t["heads"][0]]["sc"], heads[t["heads"][1]]["sc"]], axis=1)
        a = jnp.where((rel >= D_SUB) if rev else (rel < 0), sc, 0.0)
        for j in range(D_SUB):
            keep = (rel == j) & ((rin <= j) if rev else (rin >= j))
            a = jnp.where(keep, t["c"][j * n:(j + 1) * n], a)
        heads[t["heads"][0]]["a"] = a[:, 0:D_DK]
        heads[t["heads"][1]]["a"] = a[:, D_DK:pw]
    for t in heads:
        t["o_ref"][:, t["vsl"]] = t["o"] + _mm(t["a"], t["v"])
        bc = t["bc"]
        b_l = bc[0:1] if t["rev"] else bc[n - 1:n]
        s_ref[t["idx"]] = t["s"] * jnp.exp(b_l) + _tn(t["v"], t["k"] * jnp.exp(b_l - bc))


def _gla_scan(p, aupp, abias, col_q, col_k, col_v, col_a, nb, t_lat, t_ctx, wb):
    rows = p.shape[0]
    fwd, rev, nch = _chunk_maps(nb, t_lat, t_ctx)
    wk = aupp.shape[-1]
    nh = wk // D_DK

    def specs(cm):
        return [pl.BlockSpec((CHUNK, wk), lambda b, j: (cm(b, j), col_q // wk)),
                pl.BlockSpec((CHUNK, wk), lambda b, j: (cm(b, j), col_k // wk)),
                pl.BlockSpec((CHUNK, wb), lambda b, j: (cm(b, j), col_v // wb)),
                pl.BlockSpec((CHUNK, 128), lambda b, j: (cm(b, j), col_a // 128))]

    return pl.pallas_call(
        _gla_scan_kernel,
        grid=(nb, nch),
        in_specs=specs(fwd) + specs(rev) + [pl.BlockSpec(aupp.shape, lambda b, j: (0, 0, 0)),
                                           pl.BlockSpec(abias.shape, lambda b, j: (0, 0, 0))],
        out_specs=[pl.BlockSpec((CHUNK, wb), lambda b, j: (fwd(b, j), 0)),
                   pl.BlockSpec((CHUNK, wb), lambda b, j: (rev(b, j), 0))],
        out_shape=[jax.ShapeDtypeStruct((rows, wb), F32)] * 2,
        scratch_shapes=[pltpu.VMEM((2 * nh, D_DV, D_DK), F32)],
        compiler_params=_cp("arbitrary", "arbitrary"),
        name="gla_scan",
    )(p, p, p, p, p, p, p, p, aupp, abias)


def _branch_out_kernel(ya_f, ya_b, bonus, ga, pb, hc_f, hc_b, oc, od_f, od_b, gdd,
                       a_lng, a_lnb, ones, b_ws, b_bias, b_lng, b_lnb, c_lng, d_lng,
                       hs_a, hs_b, hs_c, hs_d):
    wb = hs_a.shape[1]
    y = ya_f[...] + ya_b[...]
    on = ones[...]
    mu = _seg_sum(y, on) * (1.0 / A_HD)
    yc = y - mu
    var = _seg_sum(yc * yc, on) * (1.0 / A_HD)
    yn = yc * lax.rsqrt(var + A_LN_EPS) * a_lng[...] + a_lnb[...]
    hs_a[...] = ((yn + bonus[...]) * ga[...]).astype(hs_a.dtype)
    z = jax.nn.gelu(pb[...])
    u = z[:, 0:wb]
    vv = z[:, wb:2 * wb]
    mu = jnp.mean(vv, axis=-1, keepdims=True)
    vc = vv - mu
    vn = (vc * lax.rsqrt(jnp.mean(vc * vc, axis=-1, keepdims=True) + 1e-5) * b_lng[...] + b_lnb[...])
    vn = vn.astype(BF16)
    for ck in range(u.shape[0] // B_CHUNK):
        rs = slice(ck * B_CHUNK, (ck + 1) * B_CHUNK)
        for g in range(wb // B_CHUNK):
            cs = slice(g * B_CHUNK, (g + 1) * B_CHUNK)
            s = _mm(b_ws[g].astype(BF16), vn[rs, cs]) + b_bias[:, cs]
            hs_b[rs, cs] = (u[rs, cs] * s).astype(hs_b.dtype)
    hc = hc_f[...] + hc_b[...]
    ogate = jax.nn.sigmoid(oc[...])
    gc = c_lng[...]
    for h in range(wb // C_HD):
        sl = slice(h * C_HD, (h + 1) * C_HD)
        x = hc[:, sl]
        xc = x - jnp.mean(x, axis=-1, keepdims=True)
        xn = xc * lax.rsqrt(jnp.mean(xc * xc, axis=-1, keepdims=True) + C_LN_EPS) * gc[:, sl]
        hs_c[:, sl] = (xn * ogate[:, sl]).astype(hs_c.dtype)
    od = od_f[...] + od_b[...]
    gg = gdd[...]
    gg = gg * jax.nn.sigmoid(gg)
    gd_ = d_lng[...]
    for h in range(wb // D_DV):
        sl = slice(h * D_DV, (h + 1) * D_DV)
        x = od[:, sl]
        xn = x * lax.rsqrt(jnp.mean(x * x, axis=-1, keepdims=True) + D_LN_EPS) * gd_[:, sl]
        hs_d[:, sl] = (xn * gg[:, sl]).astype(hs_d.dtype)


def _branch_out(ya_f, ya_b, bonus, ga, p, hc_f, hc_b, od_f, od_b, prm, cols, rows, tm):
    wb = ya_f.shape[1]
    row = lambda w, cb: pl.BlockSpec((tm, w), lambda i: (i, cb))
    small = [prm["a_lng"], prm["a_lnb"], prm["ones"], prm["b_ws"], prm["b_bias"], prm["b_lng"], prm["b_lnb"],
             prm["c_lng"], prm["d_lng"]]
    full = lambda a: pl.BlockSpec(a.shape, lambda i: (0,) * a.ndim)
    return pl.pallas_call(
        _branch_out_kernel,
        grid=(rows // tm,),
        in_specs=[row(wb, 0)] * 4 + [row(2 * wb, cols["b"] // (2 * wb))] + [row(wb, 0)] * 2
                 + [row(wb, cols["c_o"] // wb)] + [row(wb, 0)] * 2 + [row(wb, cols["d_g"] // wb)]
                 + [full(a) for a in small],
        out_specs=[row(wb, 0)] * 4,
        out_shape=[jax.ShapeDtypeStruct((rows, wb), BF16)] * 4,
        compiler_params=_cp("arbitrary"),
        name="branch_out",
    )(ya_f, ya_b, bonus, ga, p, hc_f, hc_b, p, od_f, od_b, p, *small)


def _pick_tile(*sizes):
    for t in (1024, 512, 256, 128):
        if all(s % t == 0 for s in sizes):
            return t
    raise ValueError("token counts must be multiples of 128")


def _pick_cols(n, cands):
    for t in cands:
        if n % t == 0:
            return t
    raise ValueError(f"no column tile for {n}")


def _proj_cols(d_model):
    wb = d_model // 4
    col = {"a": 0, "b": d_model, "c_qk": d_model + 2 * wb}
    col["c_v"] = col["c_qk"] + 2 * wb
    col["c_o"] = col["c_v"] + wb
    col["d_q"] = col["c_o"] + wb
    col["d_k"] = col["d_q"] + wb // 2
    col["d_v"] = col["d_k"] + wb // 2
    col["d_g"] = col["d_v"] + wb
    col["c_gate"] = col["d_g"] + wb
    col["d_a"] = col["c_gate"] + 128
    assert 3 * wb + 384 <= d_model
    return col


def _permute_proj(a, d_model):
    wb = d_model // 4
    a_cols = 3 * wb + 384
    o_b = a_cols
    o_c = o_b + 2 * wb
    o_d = o_c + 4 * wb + 16
    o_g = o_d + 3 * wb + 32
    z = lambda k: jnp.zeros(a.shape[:-1] + (k,), a.dtype)
    mix = jnp.concatenate([
        a[..., 0:a_cols], z(d_model - a_cols), a[..., o_b:o_b + 2 * wb], a[..., o_c:o_c + 4 * wb],
        a[..., o_d:o_d + 3 * wb], a[..., o_c + 4 * wb:o_c + 4 * wb + 16], z(112),
        a[..., o_d + 3 * wb:o_d + 3 * wb + 32], z(96)], axis=-1)
    return mix, a[..., o_g:o_g + 4 * d_model]


def _mixers(p, l, nb, t_lat, t_ctx, rows, w):
    wb = w["a_kk"].shape[1]
    col = _proj_cols(4 * wb)
    a_cols = 3 * wb + 384
    row = lambda a: a.reshape(1, -1)
    ones_bd = jnp.kron(jnp.eye(wb // A_HD, dtype=F32), jnp.ones((A_HD, A_HD), F32)).astype(BF16)

    a_wup, a_aup = w["a_wup"], w["a_aup"]
    wupp = jnp.zeros((2, 128, wb), F32).at[0, 0:64].set(a_wup[l, 0]).at[1, 64:128].set(a_wup[l, 1])
    aupp = jnp.zeros((2, 128, wb), F32).at[0, 0:64].set(a_aup[l, 0]).at[1, 64:128].set(a_aup[l, 1])
    small_a = [row(w["a_mu"][l]), row(w["a_kk"][l]), row(w["a_ka"][l]), row(w["a_rk"][l]), w["a_w0"][l],
               w["a_a0"][l], wupp, aupp, w["a_gup"][l], ones_bd]
    r_, v_, kk_, lwf, lwb, ktf, ktb, bf_, bb_, bonus, ga = _sequence_tiles(
        _rwkv_pre_kernel, p, col["a"], a_cols, GRID_W, small_a, [wb] * 11, nb, t_lat, t_ctx, "rwkv_pre")
    ya_f, ya_b = _rwkv_scan(r_, v_, kk_, lwf, lwb, ktf, ktb, bf_, bb_, nb, t_lat, t_ctx)

    kscale = jnp.concatenate([jnp.ones((wb,), F32), jnp.full((wb,), C_HD ** -0.5, F32)]).reshape(1, -1)
    qk, = _sequence_tiles(_conv_kernel, p, col["c_qk"], 2 * wb, 8,
                          [w["c_conv_w"][l], row(w["c_conv_b"][l]), kscale], [2 * wb], nb, t_lat, t_ctx,
                          "mlstm_conv")
    gbias = jnp.zeros((1, 128), F32).at[0, 0:16].set(w["c_gate_b"][l].reshape(-1))
    hc_f, hc_b = _mlstm_scan(qk, p, gbias, col["c_v"], col["c_gate"], nb, t_lat, t_ctx, wb)

    d_aup = w["d_aup"]
    rk_d = d_aup.shape[2]
    aupp_d = (jnp.zeros((2, 128, wb // 2), F32).at[0, 0:rk_d].set(d_aup[l, 0])
              .at[1, rk_d:2 * rk_d].set(d_aup[l, 1]))
    od_f, od_b = _gla_scan(p, aupp_d, w["d_ab"][l].reshape(2, 1, -1), col["d_q"], col["d_k"], col["d_v"],
                           col["d_a"], nb, t_lat, t_ctx, wb)

    b_bias = jnp.repeat(w["b_bs"][l].T, B_CHUNK, axis=1)
    prm_o = {"a_lng": row(w["a_ln_g"][l]), "a_lnb": row(w["a_ln_b"][l]), "ones": ones_bd, "b_ws": w["b_ws"][l],
             "b_bias": b_bias, "b_lng": row(w["b_ln_g"][l]), "b_lnb": row(w["b_ln_b"][l]),
             "c_lng": row(w["c_ln_g"][l]), "d_lng": row(w["d_ln_g"][l])}
    return _branch_out(ya_f, ya_b, bonus, ga, p, hc_f, hc_b, od_f, od_b, prm_o, col, rows, 256)


def kernel(x, c, ctx, c_ctx, ada_w, ada_b, norm_g, ffn_w1, ffn_w3, ffn_w2, in_w, in_b, a_mu, a_w0, a_wup, a_a0, a_aup, a_gup, a_kk, a_ka, a_rk, a_ln_g, a_ln_b, b_ws, b_bs, b_ln_g, b_ln_b, c_conv_w, c_conv_b, c_gate_b, c_ln_g, d_aup, d_ab, d_ln_g, br_w, out_w, final_g):
    weights = dict(a_mu=a_mu, a_w0=a_w0, a_wup=a_wup, a_a0=a_a0, a_aup=a_aup, a_gup=a_gup, a_kk=a_kk, a_ka=a_ka,
                   a_rk=a_rk.reshape(a_rk.shape[0], -1), a_ln_g=a_ln_g, a_ln_b=a_ln_b, b_ws=b_ws, b_bs=b_bs,
                   b_ln_g=b_ln_g, b_ln_b=b_ln_b, c_conv_w=c_conv_w, c_conv_b=c_conv_b, c_gate_b=c_gate_b,
                   c_ln_g=c_ln_g, d_aup=d_aup, d_ab=d_ab, d_ln_g=d_ln_g)
    nb, t_lat, d_model = x.shape
    t_ctx = ctx.shape[1]
    depth = ada_w.shape[0]
    wb = d_model // 4
    d_ff = ffn_w1.shape[-1]
    n_lat = nb * t_lat
    n_ctx = nb * t_ctx
    rows_all = n_lat + n_ctx
    assert t_lat % (GRID_W * 2) == 0 and t_ctx % B_CHUNK == 0 and n_lat % t_ctx == 0
    assert wb == 512 and a_wup.shape[2] == 64 and a_aup.shape[2] == 64 and a_gup.shape[1] == 128

    tm = _pick_tile(t_lat, n_ctx)

    def grp(i):
        return jnp.where(i < n_lat // tm, 1 + i // (t_lat // tm), 0)

    tm_s = _pick_tile(t_lat, n_ctx, 512)
    tf = _pick_cols(d_ff, (512, 256, 128))

    h = jnp.concatenate([x.reshape(n_lat, d_model), ctx.reshape(n_ctx, d_model)], axis=0)

    m_pad = -(-(nb + 1) // 8) * 8
    cpad = jnp.zeros((m_pad, d_model), F32).at[0].set(c_ctx).at[1:nb + 1].set(c)
    mod_all = _ada_mod(cpad, ada_w, ada_b, _pick_cols(ada_w.shape[-1], (1024, 512, 256, 128)))
    mod_all = mod_all.reshape(depth, m_pad, N_MOD, 1, d_model)

    for l in range(depth):
        modp = mod_all[l]
        last = l == depth - 1

        hn = _norm_mod(h, norm_g[l, 0], modp, 0, rows_all, tm_s, lambda i: jnp.where(
            i < n_lat // tm_s, 1 + i // (t_lat // tm_s), 0))
        gact = _ffn_up(hn, ffn_w1, ffn_w3, l, 0, tm, tf)
        h = _ffn_down(gact, ffn_w2, h, modp, 2, l, 0, tm, 256, grp)

        hn = _norm_mod(h, norm_g[l, 1], modp, 3, rows_all, tm_s, lambda i: jnp.where(
            i < n_lat // tm_s, 1 + i // (t_lat // tm_s), 0))
        w_mix, w_gate = _permute_proj(in_w[l], d_model)
        b_mix, b_gate = _permute_proj(in_b[l], d_model)
        p = _in_proj(hn, w_mix.astype(BF16), b_mix.reshape(1, -1), tm, 1152)
        gates = _in_proj(hn, w_gate.astype(BF16), b_gate.reshape(1, -1), tm, 1024, gates=True)

        rows = n_lat if last else rows_all
        hs = _mixers(p, l, nb, t_lat, t_ctx, rows, weights)
        y = _merge(hs, gates, br_w, l, rows, tm, 512)
        h = _out_proj(y, out_w, h, modp, 5, l, tm, 512, grp)

        hn = _norm_mod(h, norm_g[l, 2], modp, 6, rows, tm_s, lambda i: jnp.where(
            i < n_lat // tm_s, 1 + i // (t_lat // tm_s), 0))
        gact = _ffn_up(hn, ffn_w1, ffn_w3, l, 1, tm, tf)
        h = _ffn_down(gact, ffn_w2, h, modp, 8, l, 1, tm, 256, grp)

    out = _final_norm(h, final_g, tm_s)
    return out.reshape(nb, t_lat, d_model)
```

```python
import functools

import jax
import jax.numpy as jnp
from jax import lax
from jax.experimental import pallas as pl
from jax.experimental.pallas import tpu as pltpu

F32 = jnp.float32
BF16 = jnp.bfloat16
HIGHEST = lax.Precision.HIGHEST

EPS = 1e-6
GRID_W = 64
N_MOD = 9
CHUNK = 64
A_HD = 64
A_LN_EPS = 64e-5
B_CHUNK = 128
C_HD = 128
C_CHUNK = 128
C_LN_EPS = 1e-5
D_DK = 64
D_DV = 128
D_TAU = 16.0
D_SUB = 16
D_LN_EPS = 1e-6
VMEM_LIMIT = 56 * 1024 * 1024


def _cp(*sem):
    return pltpu.CompilerParams(dimension_semantics=sem, vmem_limit_bytes=VMEM_LIMIT)


def _mm(a, b, precision=None):
    return jnp.dot(a, b, precision=precision, preferred_element_type=F32)


def _nt(a, b, precision=None):
    return lax.dot_general(a, b, (((1,), (1,)), ((), ())), precision=precision,
                           preferred_element_type=F32)


def _tn(a, b, precision=None):
    return lax.dot_general(a, b, (((0,), (0,)), ((), ())), precision=precision,
                           preferred_element_type=F32)


def _log_sigmoid(x):
    return jnp.minimum(x, 0.0) - jnp.log(1.0 + jnp.exp(-jnp.abs(x)))


def _seg_sum(x, ones_blockdiag):
    hi = x.astype(BF16)
    lo = (x - hi.astype(F32)).astype(BF16)
    return _mm(hi, ones_blockdiag) + _mm(lo, ones_blockdiag)


def _tri(n, rev, strict):
    row = lax.broadcasted_iota(jnp.int32, (n, n), 0)
    col = lax.broadcasted_iota(jnp.int32, (n, n), 1)
    if rev:
        return (col > row) if strict else (col >= row)
    return (col < row) if strict else (col <= row)


def _norm_mod_kernel(h_ref, g_ref, sh_ref, sc_ref, o_ref):
    x = h_ref[...]
    y = x * lax.rsqrt(jnp.mean(x * x, axis=-1, keepdims=True) + EPS) * g_ref[...]
    o_ref[...] = (y * (1.0 + sc_ref[...]) + sh_ref[...]).astype(o_ref.dtype)


def _norm_mod(h, g, modp, k_shift, rows, tm, grp):
    d = h.shape[1]
    return pl.pallas_call(
        _norm_mod_kernel,
        grid=(rows // tm,),
        in_specs=[pl.BlockSpec((tm, d), lambda i: (i, 0)),
                  pl.BlockSpec((1, d), lambda i: (0, 0)),
                  pl.BlockSpec((None, None, 1, d), lambda i: (grp(i), k_shift, 0, 0)),
                  pl.BlockSpec((None, None, 1, d), lambda i: (grp(i), k_shift + 1, 0, 0))],
        out_specs=pl.BlockSpec((tm, d), lambda i: (i, 0)),
        out_shape=jax.ShapeDtypeStruct((rows, d), BF16),
        compiler_params=_cp("arbitrary"),
        name="norm_mod",
    )(h, g.reshape(1, d), modp, modp)


def _final_norm_kernel(h_ref, g_ref, o_ref):
    x = h_ref[...]
    o_ref[...] = x * lax.rsqrt(jnp.mean(x * x, axis=-1, keepdims=True) + EPS) * g_ref[...]


def _final_norm(h, g, tm):
    rows, d = h.shape
    return pl.pallas_call(
        _final_norm_kernel,
        grid=(rows // tm,),
        in_specs=[pl.BlockSpec((tm, d), lambda i: (i, 0)), pl.BlockSpec((1, d), lambda i: (0, 0))],
        out_specs=pl.BlockSpec((tm, d), lambda i: (i, 0)),
        out_shape=jax.ShapeDtypeStruct((rows, d), F32),
        compiler_params=_cp("arbitrary"),
        name="final_norm",
    )(h, g.reshape(1, d))


def _ada_kernel(c_ref, w_ref, b_ref, o_ref):
    @pl.when(pl.program_id(1) == 0)
    def _():
        o_ref[...] = jnp.broadcast_to(b_ref[...], o_ref.shape)

    c = c_ref[...]
    cond = (c * jax.nn.sigmoid(c)).astype(BF16)
    o_ref[...] += _mm(cond, w_ref[...].astype(BF16))


def _ada_mod(cpad, ada_w, ada_b, tk):
    depth, d, n = ada_w.shape
    m = cpad.shape[0]
    return pl.pallas_call(
        _ada_kernel,
        grid=(depth, d // tk),
        in_specs=[pl.BlockSpec((m, tk), lambda l, k: (0, k)),
                  pl.BlockSpec((None, tk, n), lambda l, k: (l, k, 0)),
                  pl.BlockSpec((None, 1, n), lambda l, k: (l, 0, 0))],
        out_specs=pl.BlockSpec((None, m, n), lambda l, k: (l, 0, 0)),
        out_shape=jax.ShapeDtypeStruct((depth, m, n), F32),
        compiler_params=_cp("arbitrary", "arbitrary"),
        name="ada_mod",
    )(cpad, ada_w, ada_b.reshape(depth, 1, n))


def _ffn_up_kernel(x_ref, w1_ref, w3_ref, o_ref):
    x = x_ref[...]
    a = _mm(x, w1_ref[...].astype(BF16))
    b = _mm(x, w3_ref[...].astype(BF16))
    o_ref[...] = (a * jax.nn.sigmoid(a) * b).astype(o_ref.dtype)


def _ffn_up(x, w1, w3, l, s, tm, tf):
    rows, d = x.shape
    f = w1.shape[-1]
    wspec = pl.BlockSpec((None, None, d, tf), lambda i, j: (l, s, 0, j))
    return pl.pallas_call(
        _ffn_up_kernel,
        grid=(rows // tm, f // tf),
        in_specs=[pl.BlockSpec((tm, d), lambda i, j: (i, 0)), wspec, wspec],
        out_specs=pl.BlockSpec((tm, tf), lambda i, j: (i, j)),
        out_shape=jax.ShapeDtypeStruct((rows, f), BF16),
        compiler_params=_cp("arbitrary", "arbitrary"),
        name="ffn_up",
    )(x, w1, w3)


def _ffn_down_kernel(g_ref, w_ref, h_ref, gate_ref, o_ref):
    acc = _mm(g_ref[...], w_ref[...].astype(BF16))
    o_ref[...] = h_ref[...] + (0.5 * acc) * gate_ref[...]


def _ffn_down(gact, w2, h, modp, k_gate, l, s, tm, tn, grp):
    rows, f = gact.shape
    d = h.shape[1]
    return pl.pallas_call(
        _ffn_down_kernel,
        grid=(rows // tm, d // tn),
        in_specs=[pl.BlockSpec((tm, f), lambda i, j: (i, 0)),
                  pl.BlockSpec((None, None, f, tn), lambda i, j: (l, s, 0, j)),
                  pl.BlockSpec((tm, tn), lambda i, j: (i, j)),
                  pl.BlockSpec((None, None, 1, tn), lambda i, j: (grp(i), k_gate, 0, j))],
        out_specs=pl.BlockSpec((tm, tn), lambda i, j: (i, j)),
        out_shape=jax.ShapeDtypeStruct((rows, d), F32),
        compiler_params=_cp("arbitrary", "arbitrary"),
        name="ffn_down",
    )(gact, w2, h, modp)


def _in_proj_kernel(x_ref, w_ref, b_ref, o_ref):
    o_ref[...] = _mm(x_ref[...], w_ref[...]) + b_ref[...]


def _gate_proj_kernel(x_ref, w_ref, b_ref, o_ref):
    o_ref[...] = jax.nn.sigmoid(_mm(x_ref[...], w_ref[...]) + b_ref[...]).astype(o_ref.dtype)


def _in_proj(x, w, b, tm, tn, gates=False):
    rows, d = x.shape
    n = w.shape[1]
    return pl.pallas_call(
        _gate_proj_kernel if gates else _in_proj_kernel,
        grid=(rows // tm, n // tn),
        in_specs=[pl.BlockSpec((tm, d), lambda i, j: (i, 0)),
                  pl.BlockSpec((d, tn), lambda i, j: (0, j)),
                  pl.BlockSpec((1, tn), lambda i, j: (0, j))],
        out_specs=pl.BlockSpec((tm, tn), lambda i, j: (i, j)),
        out_shape=jax.ShapeDtypeStruct((rows, n), BF16 if gates else F32),
        compiler_params=_cp("arbitrary", "arbitrary"),
        name="gate_proj" if gates else "in_proj",
    )(x, w, b)


def _merge_kernel(ha, hb, hc, hd, ga, gb, gc, gd, wa, wb, wc, wd, o_ref):
    y = ga[...].astype(F32) * _mm(ha[...], wa[...].astype(BF16))
    y = y + gb[...].astype(F32) * _mm(hb[...], wb[...].astype(BF16))
    y = y + gc[...].astype(F32) * _mm(hc[...], wc[...].astype(BF16))
    y = y + gd[...].astype(F32) * _mm(hd[...], wd[...].astype(BF16))
    o_ref[...] = y.astype(o_ref.dtype)


def _merge(hs, gates, br_w, l, rows, tm, tn):
    wbr = hs[0].shape[1]
    d = br_w.shape[-1]
    nj = d // tn
    hspec = pl.BlockSpec((tm, wbr), lambda i, j: (i, 0))
    gspecs = [pl.BlockSpec((tm, tn), functools.partial(lambda i, j, n: (i, n * nj + j), n=n))
              for n in range(4)]
    wspecs = [pl.BlockSpec((None, None, wbr, tn), functools.partial(lambda i, j, n: (l, n, 0, j), n=n))
              for n in range(4)]
    return pl.pallas_call(
        _merge_kernel,
        grid=(rows // tm, nj),
        in_specs=[hspec] * 4 + gspecs + wspecs,
        out_specs=pl.BlockSpec((tm, tn), lambda i, j: (i, j)),
        out_shape=jax.ShapeDtypeStruct((rows, d), BF16),
        compiler_params=_cp("arbitrary", "arbitrary"),
        name="merge",
    )(*hs, gates, gates, gates, gates, br_w, br_w, br_w, br_w)


def _out_proj_kernel(y_ref, w_ref, h_ref, gate_ref, o_ref):
    acc = _mm(y_ref[...], w_ref[...].astype(BF16))
    o_ref[...] = h_ref[...] + acc * gate_ref[...]


def _out_proj(y, out_w, h, modp, k_gate, l, tm, tn, grp):
    rows, d = y.shape
    return pl.pallas_call(
        _out_proj_kernel,
        grid=(rows // tm, d // tn),
        in_specs=[pl.BlockSpec((tm, d), lambda i, j: (i, 0)),
                  pl.BlockSpec((None, d, tn), lambda i, j: (l, 0, j)),
                  pl.BlockSpec((tm, tn), lambda i, j: (i, j)),
                  pl.BlockSpec((None, None, 1, tn), lambda i, j: (grp(i), k_gate, 0, j))],
        out_specs=pl.BlockSpec((tm, tn), lambda i, j: (i, j)),
        out_shape=jax.ShapeDtypeStruct((rows, d), F32),
        compiler_params=_cp("arbitrary", "arbitrary"),
        name="out_proj",
    )(y, out_w, h, modp)


def _tile_place(geom):
    n_lat_tiles, lat_tiles, ctx_tiles = geom
    i = pl.program_id(0)
    is_ctx = i >= n_lat_tiles
    per_seq = jnp.where(is_ctx, ctx_tiles, lat_tiles)
    pos = lax.rem(jnp.where(is_ctx, i - n_lat_tiles, i), per_seq)
    return is_ctx, pos == 0, pos == per_seq - 1


def _row_neighbours(xp_ref, x, xn_ref, first, last):
    ts = x.shape[0]
    hp = xp_ref.shape[0]
    t = lax.broadcasted_iota(jnp.int32, x.shape, 0)
    prv = jnp.where(t == 0, jnp.where(first, 0.0, xp_ref[hp - 1:hp, :]), pltpu.roll(x, 1, 0))
    nxt = jnp.where(t == ts - 1, jnp.where(last, 0.0, xn_ref[0:1, :]), pltpu.roll(x, ts - 1, 0))
    return prv, nxt


def _token_shift(xp_ref, x_ref, xn_ref, mu_ref, geom):
    is_ctx, first, last = _tile_place(geom)
    x = x_ref[...]
    ts = x.shape[0]
    t = lax.broadcasted_iota(jnp.int32, x.shape, 0)
    lane = lax.broadcasted_iota(jnp.int32, x.shape, 1)
    prv, nxt = _row_neighbours(xp_ref, x, xn_ref, first, last)
    sh_ctx = jnp.where((lane & 1) == 0, prv, nxt)
    tw = t & (GRID_W - 1)
    left = jnp.where(tw == 0, 0.0, prv)
    right = jnp.where(tw == GRID_W - 1, 0.0, nxt)
    up = jnp.concatenate([jnp.where(first, 0.0, xp_ref[...]), x[0:ts - GRID_W]], axis=0)
    down = jnp.concatenate([x[GRID_W:ts], jnp.where(last, 0.0, xn_ref[...])], axis=0)
    c4 = lane & 3
    sh_lat = jnp.where(c4 == 0, left, jnp.where(c4 == 1, right, jnp.where(c4 == 2, up, down)))
    sh = jnp.where(is_ctx, sh_ctx, sh_lat)
    return x + (sh - x) * mu_ref[...]


def _conv_kernel(xp_ref, x_ref, xn_ref, w_ref, b_ref, s_ref, o_ref, *, geom):
    _, first, last = _tile_place(geom)
    x = x_ref[...]
    prv, nxt = _row_neighbours(xp_ref, x, xn_ref, first, last)
    w = w_ref[...]
    y = prv * w[0:1] + x * w[1:2] + nxt * w[2:3] + b_ref[...]
    o_ref[...] = y * jax.nn.sigmoid(y) * s_ref[...]


def _sequence_tiles(body, p, col0, width, halo, small, out_widths, nb, t_lat, t_ctx, name):
    rows = p.shape[0]
    ts = min(256, t_ctx)
    assert t_lat % ts == 0 and t_ctx % ts == 0 and ts >= 2 * GRID_W and ts % halo == 0 and col0 % width == 0
    nt = rows // ts
    nh = rows // halo
    r = ts // halo
    geom = (nb * t_lat // ts, t_lat // ts, t_ctx // ts)
    cb = col0 // width
    full = lambda a: pl.BlockSpec(a.shape, lambda i: (0,) * a.ndim)
    return pl.pallas_call(
        functools.partial(body, geom=geom),
        grid=(nt,),
        in_specs=[pl.BlockSpec((halo, width), lambda i: (jnp.maximum(i * r - 1, 0), cb)),
                  pl.BlockSpec((ts, width), lambda i: (i, cb)),
                  pl.BlockSpec((halo, width), lambda i: (jnp.minimum((i + 1) * r, nh - 1), cb))]
                 + [full(a) for a in small],
        out_specs=[pl.BlockSpec((ts, ow), lambda i: (i, 0)) for ow in out_widths],
        out_shape=[jax.ShapeDtypeStruct((rows, ow), F32) for ow in out_widths],
        compiler_params=_cp("arbitrary"),
        name=name,
    )(p, p, p, *small)


def _rwkv_pre_kernel(xp_ref, x_ref, xn_ref, mu_ref, kk_ref, ka_ref, rk_ref, w0_ref, a0_ref, wup_ref, aup_ref,
                     gup_ref, ones_ref, r_o, v_o, kk_o, lwf_o, lwb_o, ktf_o, ktb_o, bf_o, bb_o, bonus_o, g_o, *,
                     geom):
    wb = r_o.shape[1]
    za = _token_shift(xp_ref, x_ref, xn_ref, mu_ref, geom)
    r = za[:, 0:wb]
    k = za[:, wb:2 * wb]
    v = za[:, 2 * wb:3 * wb]
    wd = jnp.tanh(za[:, 3 * wb:3 * wb + 128])
    ad = za[:, 3 * wb + 128:3 * wb + 256]
    gd = jax.nn.sigmoid(za[:, 3 * wb + 256:3 * wb + 384])
    ones = ones_ref[...]
    kq = k * kk_ref[...]
    kk = kq * lax.rsqrt(jnp.maximum(_seg_sum(kq * kq, ones), 1e-24))
    r_o[...] = r
    v_o[...] = v
    kk_o[...] = kk
    ka = ka_ref[...]
    for d, (lw_o, kt_o, b_o) in enumerate(((lwf_o, ktf_o, bf_o), (lwb_o, ktb_o, bb_o))):
        xw = w0_ref[d:d + 1] + _mm(wd, wup_ref[d])
        lw_o[...] = -jax.nn.sigmoid(xw) * 0.6065306597126334
        a = jax.nn.sigmoid(a0_ref[d:d + 1] + _mm(ad, aup_ref[d]))
        kt_o[...] = k * (1.0 + (a - 1.0) * ka)
        b_o[...] = kk * a
    bonus_o[...] = _seg_sum(r * k * rk_ref[...], ones) * v
    g_o[...] = _mm(gd, gup_ref[...])


def _scan_geometry(nb, t_lat, t_ctx, chunk):
    nlc = t_lat // chunk
    ncc = t_ctx // chunk
    nch = nlc + ncc

    def fwd(j):
        return jnp.where(j < ncc, nlc + j, j - ncc)

    def rev(j):
        return nch - 1 - j

    def row(b, c):
        return jnp.where(c < nlc, b * nlc + c, nb * nlc + b * ncc + (c - nlc))

    return fwd, rev, row, nch


def _scan_in_specs(nb, row, cm, chunk, width, colblk):
    return [pl.BlockSpec((chunk, width), functools.partial(lambda j, b: (row(b, cm(j)), colblk), b=b))
            for b in range(nb)]


def _scan_out_spec(nb, cm, chunk, width):
    return pl.BlockSpec((None, nb, chunk, width), lambda j: (cm(j), 0, 0, 0))


def _rwkv_scan_kernel(*refs, nb):
    ins = refs[:12 * nb]
    yf, yb, s_ref = refs[12 * nb:]

    @pl.when(pl.program_id(0) == 0)
    def _():
        s_ref[...] = jnp.zeros_like(s_ref)

    n, wbw = ins[0].shape
    nh = wbw // A_HD
    heads = []
    for d, y_r in enumerate((yf, yb)):
        rev = d == 1
        arr = lambda a, b: ins[(d * 6 + a) * nb + b][...]
        lws = [arr(3, b) for b in range(nb)]
        c_all = _mm(_tri(n, rev, False).astype(F32), jnp.concatenate(lws, axis=1), HIGHEST)
        strict = _tri(n, rev, True)
        incl = _tri(n, rev, False)
        for b in range(nb):
            lw = lws[b]
            c = c_all[:, b * wbw:(b + 1) * wbw]
            eg = jnp.exp(c)
            ieg = jnp.exp(-c)
            r_all = arr(0, b) * eg
            kk_all = arr(2, b) * jnp.exp(c - lw)
            kt_all = arr(4, b) * ieg
            b_all = arr(5, b) * ieg
            v_all = arr(1, b)
            g_last = eg[0:1] if rev else eg[n - 1:n]
            for h in range(nh):
                sl = slice(h * A_HD, (h + 1) * A_HD)
                idx = (d * nb + b) * nh + h
                heads.append(dict(r=r_all[:, sl], kk=kk_all[:, sl], b=b_all[:, sl], kt=kt_all[:, sl],
                                  v=v_all[:, sl], s=s_ref[idx], g=g_last[:, sl], strict=strict, incl=incl,
                                  y_ref=y_r.at[b], sl=sl, idx=idx))
    for t in heads:
        t["z"] = _nt(jnp.concatenate([t["kk"], t["r"]], axis=0), jnp.concatenate([t["kt"], t["b"], t["s"]], axis=0))
    for t in heads:
        z = t["z"]
        t["a_kv"] = jnp.where(t["strict"], z[0:n, 0:n], 0.0)
        t["a_kb"] = jnp.where(t["strict"], z[0:n, n:2 * n], 0.0)
        t["r_kv"] = jnp.where(t["incl"], z[n:2 * n, 0:n], 0.0)
        t["r_kb"] = jnp.where(t["incl"], z[n:2 * n, n:2 * n], 0.0)
    for t in heads:
        t["av"] = _mm(jnp.concatenate([t["a_kv"], t["r_kv"]], axis=0), t["v"])
    left = lax.broadcasted_iota(jnp.int32, (n, 2 * n), 1) < n
    for t in heads:
        t["w"] = jnp.concatenate([t["a_kb"], t["z"][0:n, 2 * n:] + t["av"][0:n]], axis=1)
    for t in heads:
        r = _mm(t["a_kb"], t["w"])
        t["w"] = jnp.where(left, r, t["w"] - r)
    m = 2
    while m < n:
        for t in heads:
            r = _mm(t["w"][:, 0:n], t["w"])
            t["w"] = jnp.where(left, r, t["w"] + r)
        m *= 2
    for t in heads:
        t["u"] = t["w"][:, n:2 * n]
        y = t["z"][n:2 * n, 2 * n:] + t["av"][n:2 * n] - _mm(t["r_kb"], t["u"])
        t["y_ref"][:, t["sl"]] = y
        s_new = t["s"] + _tn(jnp.concatenate([t["v"], t["u"]], axis=0), jnp.concatenate([t["kt"], -t["b"]], axis=0))
        s_ref[t["idx"]] = s_new * t["g"]


def _rwkv_scan(r, v, kk, lwf, lwb, ktf, ktb, bf, bb, nb, t_lat, t_ctx):
    wb = r.shape[1]
    fwd, rev, row, nch = _scan_geometry(nb, t_lat, t_ctx, CHUNK)
    in_specs = [s for cm in (fwd, rev) for _ in range(6) for s in _scan_in_specs(nb, row, cm, CHUNK, wb, 0)]
    args = [a for grp in ((r, v, kk, lwf, ktf, bf), (r, v, kk, lwb, ktb, bb)) for a in grp for _ in range(nb)]
    return pl.pallas_call(
        functools.partial(_rwkv_scan_kernel, nb=nb),
        grid=(nch,),
        in_specs=in_specs,
        out_specs=[_scan_out_spec(nb, fwd, CHUNK, wb), _scan_out_spec(nb, rev, CHUNK, wb)],
        out_shape=[jax.ShapeDtypeStruct((nch, nb, CHUNK, wb), F32)] * 2,
        scratch_shapes=[pltpu.VMEM((2 * nb * (wb // A_HD), A_HD, A_HD), F32)],
        compiler_params=_cp("arbitrary"),
        name="rwkv_scan",
    )(*args)


def _mlstm_scan_kernel(*refs, nb):
    ins = refs[:8 * nb]
    gbias_ref, hf, hb, c_ref, n_ref, m_ref = refs[8 * nb:]

    @pl.when(pl.program_id(0) == 0)
    def _():
        c_ref[...] = jnp.zeros_like(c_ref)
        n_ref[...] = jnp.zeros_like(n_ref)
        m_ref[...] = jnp.zeros_like(m_ref)

    n = ins[0].shape[0]
    nh = ins[0].shape[1] // C_HD
    heads = []
    for d, h_r in enumerate((hf, hb)):
        rev = d == 1
        arr = lambda a, b: ins[(d * 4 + a) * nb + b][...]
        gates = [arr(3, b) + gbias_ref[...] for b in range(nb)]
        fgs = [_log_sigmoid(g) for g in gates]
        mask = _tri(n, rev, False)
        mi = mask.astype(F32)
        bcol_all = _mm(mi, jnp.concatenate(fgs, axis=1), HIGHEST)
        brow_all = _nt(jnp.concatenate([f.T for f in fgs], axis=0), mi, HIGHEST)
        for b in range(nb):
            bcol = bcol_all[:, b * 128:(b + 1) * 128]
            brow = brow_all[b * 128:(b + 1) * 128]
            gates_t = gates[b].T
            q_all, k_all, v_all = arr(0, b), arr(1, b), arr(2, b)
            for h in range(nh):
                sl = slice(h * C_HD, (h + 1) * C_HD)
                ii = d * 2 * nh + h
                fi = ii + nh
                idx = (d * nb + b) * nh + h
                b_c = bcol[:, fi:fi + 1]
                heads.append(dict(q=q_all[:, sl], k=k_all[:, sl], v=v_all[:, sl], b_c=b_c, b_r=brow[fi:fi + 1, :],
                                  i_c=gates[b][:, ii:ii + 1], i_r=gates_t[ii:ii + 1, :], m=m_ref[idx][:, 0:1],
                                  cm=c_ref[idx], nn=n_ref[idx], mask=mask,
                                  b_l=b_c[0:1] if rev else b_c[n - 1:n], idx=idx, h_ref=h_r.at[b], sl=sl))
    for t in heads:
        t["qk"] = _nt(t["q"], t["k"])
        t["qc"] = _nt(t["q"], t["cm"])
    for t in heads:
        t["dlog"] = jnp.where(t["mask"], t["b_c"] - t["b_r"] + t["i_r"], -jnp.inf)
        t["gl"] = t["b_l"] - t["b_c"] + t["i_c"]
    for t in heads:
        t["dmax"] = jnp.max(t["dlog"], axis=1, keepdims=True)
        t["gmax"] = jnp.max(t["gl"], axis=0, keepdims=True)
    for t in heads:
        inter = t["b_c"] + t["m"]
        m_t = jnp.maximum(inter, t["dmax"])
        t["iw"] = jnp.exp(inter - m_t)
        t["m_t"] = m_t
        t["s"] = t["qk"] * jnp.exp(t["dlog"] - m_t)
        m_new = jnp.maximum(t["b_l"] + t["m"], t["gmax"])
        t["sw"] = jnp.exp(t["gl"] - m_new)
        t["dec"] = jnp.exp(t["b_l"] + t["m"] - m_new)
        t["m_new"] = m_new
    for t in heads:
        t["sv"] = _mm(t["s"], t["v"])
        t["vk"] = _tn(t["v"] * t["sw"], t["k"])
    for t in heads:
        t["rs"] = jnp.sum(t["s"], axis=1, keepdims=True)
        t["qn"] = jnp.sum(t["q"] * t["nn"], axis=1, keepdims=True)
        t["ks"] = jnp.sum(t["sw"] * t["k"], axis=0, keepdims=True)
    for t in heads:
        iw = t["iw"]
        num = t["sv"] + iw * t["qc"]
        den = jnp.maximum(jnp.abs(t["rs"] + iw * t["qn"]), jnp.exp(-t["m_t"]))
        t["h_ref"][:, t["sl"]] = num / den
        idx = t["idx"]
        c_ref[idx] = t["dec"] * t["cm"] + t["vk"]
        n_ref[idx] = t["dec"] * t["nn"] + t["ks"]
        m_ref[idx] = jnp.broadcast_to(t["m_new"], m_ref.shape[1:])


def _mlstm_scan(qk, p, gbias, col_v, col_g, nb, t_lat, t_ctx, wb):
    ck = C_CHUNK
    fwd, rev, row, nch = _scan_geometry(nb, t_lat, t_ctx, ck)
    nh = wb // C_HD

    def specs(cm):
        return (_scan_in_specs(nb, row, cm, ck, wb, 0) + _scan_in_specs(nb, row, cm, ck, wb, 1)
                + _scan_in_specs(nb, row, cm, ck, wb, col_v // wb) + _scan_in_specs(nb, row, cm, ck, 128, col_g // 128))

    args = ([qk] * nb + [qk] * nb + [p] * nb + [p] * nb) * 2
    return pl.pallas_call(
        functools.partial(_mlstm_scan_kernel, nb=nb),
        grid=(nch,),
        in_specs=specs(fwd) + specs(rev) + [pl.BlockSpec((1, 128), lambda j: (0, 0))],
        out_specs=[_scan_out_spec(nb, fwd, ck, wb), _scan_out_spec(nb, rev, ck, wb)],
        out_shape=[jax.ShapeDtypeStruct((nch, nb, ck, wb), F32)] * 2,
        scratch_shapes=[pltpu.VMEM((2 * nb * nh, C_HD, C_HD), F32), pltpu.VMEM((2 * nb * nh, 1, C_HD), F32),
                        pltpu.VMEM((2 * nb * nh, 1, 128), F32)],
        compiler_params=_cp("arbitrary"),
        name="mlstm_scan",
    )(*args, gbias)


def _gla_scan_kernel(*refs, nb):
    ins = refs[:8 * nb]
    aup_ref, abias_ref, of, ob, s_ref = refs[8 * nb:]

    @pl.when(pl.program_id(0) == 0)
    def _():
        s_ref[...] = jnp.zeros_like(s_ref)

    n, wkw = ins[0].shape
    nh = wkw // D_DK
    nsub = n // D_SUB
    pw = 2 * D_DK
    row = lax.broadcasted_iota(jnp.int32, (n, pw), 0)
    colx = lax.broadcasted_iota(jnp.int32, (n, pw), 1) & (D_DK - 1)
    rel = colx - (row & -D_SUB)
    rin = row & (D_SUB - 1)
    same_head = ((lax.broadcasted_iota(jnp.int32, (pw, pw), 0) & D_DK)
                 == (lax.broadcasted_iota(jnp.int32, (pw, pw), 1) & D_DK))
    ones_pair = jnp.where(same_head, 1.0, 0.0).astype(BF16)
    heads, pairs = [], []
    for d, o_r in enumerate((of, ob)):
        rev = d == 1
        arr = lambda a, b: ins[(d * 4 + a) * nb + b][...]
        la_rows = _log_sigmoid(_mm(jnp.concatenate([arr(3, b) for b in range(nb)], axis=0), aup_ref[d])
                               + abias_ref[d]) * (1.0 / D_TAU)
        la_all = jnp.concatenate([la_rows[b * n:(b + 1) * n] for b in range(nb)], axis=1)
        bc_all = _mm(_tri(n, rev, False).astype(F32), la_all, HIGHEST)
        for b in range(nb):
            bc = bc_all[:, b * wkw:(b + 1) * wkw]
            q_all = arr(0, b) * (D_DK ** -0.5)
            k_all = arr(1, b)
            v_all = arr(2, b)
            base = (d * nb + b) * nh
            for h in range(nh):
                ksl = slice(h * D_DK, (h + 1) * D_DK)
                vsl = slice(h * D_DV, (h + 1) * D_DV)
                heads.append(dict(q=q_all[:, ksl], k=k_all[:, ksl], v=v_all[:, vsl], bc=bc[:, ksl], rev=rev,
                                  s=s_ref[base + h], idx=base + h, o_ref=o_r.at[b], vsl=vsl))
            for p in range(nh // 2):
                psl = slice(p * pw, (p + 1) * pw)
                pairs.append(dict(q=q_all[:, psl], k=k_all[:, psl], bc=bc[:, psl], rev=rev,
                                  heads=(base + 2 * p, base + 2 * p + 1)))
    for t in heads:
        t["o"] = _nt(t["q"] * jnp.exp(t["bc"]), t["s"])
    for t in heads:
        q, k, bc, rev = t["q"], t["k"], t["bc"], t["rev"]
        pieces = []
        for blk in range(nsub):
            r0 = blk * D_SUB
            edge = bc[r0 + D_SUB - 1:r0 + D_SUB] if rev else bc[r0:r0 + 1]
            has_other = blk < nsub - 1 if rev else blk > 0
            if has_other:
                qs = q[r0:r0 + D_SUB] * jnp.exp(bc[r0:r0 + D_SUB] - edge)
                ks = k * jnp.exp(jnp.minimum(edge - bc, 0.0))
                pieces.append(_nt(qs, ks))
            else:
                pieces.append(jnp.zeros((D_SUB, n), F32))
        t["sc"] = jnp.concatenate(pieces, axis=0)
    for t in pairs:
        q, k, bc = t["q"], t["k"], t["bc"]
        es = []
        for j in range(D_SUB):
            kj = jnp.concatenate([jnp.broadcast_to(k[b * D_SUB + j:b * D_SUB + j + 1], (D_SUB, pw))
                                  for b in range(nsub)], axis=0)
            bj = jnp.concatenate([jnp.broadcast_to(bc[b * D_SUB + j:b * D_SUB + j + 1], (D_SUB, pw))
                                  for b in range(nsub)], axis=0)
            es.append(q * kj * jnp.exp(bc - bj))
        t["e"] = jnp.concatenate(es, axis=0)
    for t in pairs:
        t["c"] = _seg_sum(t["e"], ones_pair)
    for t in pairs:
        rev = t["rev"]
        sc = jnp.concatenate([heads[t["heads"][0]]["sc"], heads[t["heads"][1]]["sc"]], axis=1)
        a = jnp.where((rel >= D_SUB) if rev else (rel < 0), sc, 0.0)
        for j in range(D_SUB):
            keep = (rel == j) & ((rin <= j) if rev else (rin >= j))
            a = jnp.where(keep, t["c"][j * n:(j + 1) * n], a)
        heads[t["heads"][0]]["a"] = a[:, 0:D_DK]
        heads[t["heads"][1]]["a"] = a[:, D_DK:pw]
    for t in heads:
        t["o_ref"][:, t["vsl"]] = t["o"] + _mm(t["a"], t["v"])
        bc = t["bc"]
        b_l = bc[0:1] if t["rev"] else bc[n - 1:n]
        s_ref[t["idx"]] = t["s"] * jnp.exp(b_l) + _tn(t["v"], t["k"] * jnp.exp(b_l - bc))


def _gla_scan(p, aupp, abias, col_q, col_k, col_v, col_a, nb, t_lat, t_ctx, wb):
    fwd, rev, row, nch = _scan_geometry(nb, t_lat, t_ctx, CHUNK)
    wk = aupp.shape[-1]
    nh = wk // D_DK

    def specs(cm):
        return (_scan_in_specs(nb, row, cm, CHUNK, wk, col_q // wk) + _scan_in_specs(nb, row, cm, CHUNK, wk, col_k // wk)
                + _scan_in_specs(nb, row, cm, CHUNK, wb, col_v // wb)
                + _scan_in_specs(nb, row, cm, CHUNK, 128, col_a // 128))

    return pl.pallas_call(
        functools.partial(_gla_scan_kernel, nb=nb),
        grid=(nch,),
        in_specs=specs(fwd) + specs(rev) + [pl.BlockSpec(aupp.shape, lambda j: (0, 0, 0)),
                                           pl.BlockSpec(abias.shape, lambda j: (0, 0, 0))],
        out_specs=[_scan_out_spec(nb, fwd, CHUNK, wb), _scan_out_spec(nb, rev, CHUNK, wb)],
        out_shape=[jax.ShapeDtypeStruct((nch, nb, CHUNK, wb), F32)] * 2,
        scratch_shapes=[pltpu.VMEM((2 * nb * nh, D_DV, D_DK), F32)],
        compiler_params=_cp("arbitrary"),
        name="gla_scan",
    )(*([p] * (8 * nb)), aupp, abias)


def _branch_out_kernel(ya_f, ya_b, bonus, ga, pb, hc_f, hc_b, oc, od_f, od_b, gdd,
                       a_lng, a_lnb, ones, b_ws, b_bias, b_lng, b_lnb, c_lng, d_lng,
                       hs_a, hs_b, hs_c, hs_d):
    wb = hs_a.shape[1]
    tm = hs_a.shape[0]
    y = ya_f[...].reshape(tm, wb) + ya_b[...].reshape(tm, wb)
    on = ones[...]
    mu = _seg_sum(y, on) * (1.0 / A_HD)
    yc = y - mu
    var = _seg_sum(yc * yc, on) * (1.0 / A_HD)
    yn = yc * lax.rsqrt(var + A_LN_EPS) * a_lng[...] + a_lnb[...]
    hs_a[...] = ((yn + bonus[...]) * ga[...]).astype(hs_a.dtype)
    z = jax.nn.gelu(pb[...])
    u = z[:, 0:wb]
    vv = z[:, wb:2 * wb]
    mu = jnp.mean(vv, axis=-1, keepdims=True)
    vc = vv - mu
    vn = (vc * lax.rsqrt(jnp.mean(vc * vc, axis=-1, keepdims=True) + 1e-5) * b_lng[...] + b_lnb[...])
    vn = vn.astype(BF16)
    for ck in range(u.shape[0] // B_CHUNK):
        rs = slice(ck * B_CHUNK, (ck + 1) * B_CHUNK)
        for g in range(wb // B_CHUNK):
            cs = slice(g * B_CHUNK, (g + 1) * B_CHUNK)
            s = _mm(b_ws[g].astype(BF16), vn[rs, cs]) + b_bias[:, cs]
            hs_b[rs, cs] = (u[rs, cs] * s).astype(hs_b.dtype)
    hc = hc_f[...].reshape(tm, wb) + hc_b[...].reshape(tm, wb)
    ogate = jax.nn.sigmoid(oc[...])
    gc = c_lng[...]
    for h in range(wb // C_HD):
        sl = slice(h * C_HD, (h + 1) * C_HD)
        x = hc[:, sl]
        xc = x - jnp.mean(x, axis=-1, keepdims=True)
        xn = xc * lax.rsqrt(jnp.mean(xc * xc, axis=-1, keepdims=True) + C_LN_EPS) * gc[:, sl]
        hs_c[:, sl] = (xn * ogate[:, sl]).astype(hs_c.dtype)
    od = od_f[...].reshape(tm, wb) + od_b[...].reshape(tm, wb)
    gg = gdd[...]
    gg = gg * jax.nn.sigmoid(gg)
    gd_ = d_lng[...]
    for h in range(wb // D_DV):
        sl = slice(h * D_DV, (h + 1) * D_DV)
        x = od[:, sl]
        xn = x * lax.rsqrt(jnp.mean(x * x, axis=-1, keepdims=True) + D_LN_EPS) * gd_[:, sl]
        hs_d[:, sl] = (xn * gg[:, sl]).astype(hs_d.dtype)


def _branch_out(ya_f, ya_b, bonus, ga, p, hc_f, hc_b, od_f, od_b, prm, cols, rows, nb, t_lat, t_ctx):
    wb = bonus.shape[1]
    tm = min(256, t_ctx)
    n_lat_tiles, lat_tiles, ctx_tiles = nb * t_lat // tm, t_lat // tm, t_ctx // tm
    row = lambda w, cb: pl.BlockSpec((tm, w), lambda i: (i, cb))

    def scan_out(a):
        per_tile = tm // a.shape[2]

        def imap(i):
            ic = i - n_lat_tiles
            seq = jnp.where(i < n_lat_tiles, i // lat_tiles, ic // ctx_tiles)
            blk = jnp.where(i < n_lat_tiles, lax.rem(i, lat_tiles), lat_tiles + lax.rem(ic, ctx_tiles))
            return blk, seq, 0, 0

        return pl.BlockSpec((per_tile, None, a.shape[2], wb), imap)

    small = [prm["a_lng"], prm["a_lnb"], prm["ones"], prm["b_ws"], prm["b_bias"], prm["b_lng"], prm["b_lnb"],
             prm["c_lng"], prm["d_lng"]]
    full = lambda a: pl.BlockSpec(a.shape, lambda i: (0,) * a.ndim)
    return pl.pallas_call(
        _branch_out_kernel,
        grid=(rows // tm,),
        in_specs=[scan_out(ya_f), scan_out(ya_b), row(wb, 0), row(wb, 0), row(2 * wb, cols["b"] // (2 * wb)),
                  scan_out(hc_f), scan_out(hc_b), row(wb, cols["c_o"] // wb), scan_out(od_f), scan_out(od_b),
                  row(wb, cols["d_g"] // wb)] + [full(a) for a in small],
        out_specs=[row(wb, 0)] * 4,
        out_shape=[jax.ShapeDtypeStruct((rows, wb), BF16)] * 4,
        compiler_params=_cp("arbitrary"),
        name="branch_out",
    )(ya_f, ya_b, bonus, ga, p, hc_f, hc_b, p, od_f, od_b, p, *small)


def _pick_tile(*sizes):
    for t in (1024, 512, 256, 128):
        if all(s % t == 0 for s in sizes):
            return t
    raise ValueError("token counts must be multiples of 128")


def _pick_cols(n, cands):
    for t in cands:
        if n % t == 0:
            return t
    raise ValueError(f"no column tile for {n}")


def _proj_cols(d_model):
    wb = d_model // 4
    col = {"a": 0, "b": d_model, "c_qk": d_model + 2 * wb}
    col["c_v"] = col["c_qk"] + 2 * wb
    col["c_o"] = col["c_v"] + wb
    col["d_q"] = col["c_o"] + wb
    col["d_k"] = col["d_q"] + wb // 2
    col["d_v"] = col["d_k"] + wb // 2
    col["d_g"] = col["d_v"] + wb
    col["c_gate"] = col["d_g"] + wb
    col["d_a"] = col["c_gate"] + 128
    assert 3 * wb + 384 <= d_model
    return col


def _permute_proj(a, d_model):
    wb = d_model // 4
    a_cols = 3 * wb + 384
    o_b = a_cols
    o_c = o_b + 2 * wb
    o_d = o_c + 4 * wb + 16
    o_g = o_d + 3 * wb + 32
    z = lambda k: jnp.zeros(a.shape[:-1] + (k,), a.dtype)
    mix = jnp.concatenate([
        a[..., 0:a_cols], z(d_model - a_cols), a[..., o_b:o_b + 2 * wb], a[..., o_c:o_c + 4 * wb],
        a[..., o_d:o_d + 3 * wb], a[..., o_c + 4 * wb:o_c + 4 * wb + 16], z(112),
        a[..., o_d + 3 * wb:o_d + 3 * wb + 32], z(96)], axis=-1)
    return mix, a[..., o_g:o_g + 4 * d_model]


def _mixers(p, l, nb, t_lat, t_ctx, rows, w):
    wb = w["a_kk"].shape[1]
    col = _proj_cols(4 * wb)
    a_cols = 3 * wb + 384
    row = lambda a: a.reshape(1, -1)
    ones_bd = jnp.kron(jnp.eye(wb // A_HD, dtype=F32), jnp.ones((A_HD, A_HD), F32)).astype(BF16)

    a_wup, a_aup = w["a_wup"], w["a_aup"]
    wupp = jnp.zeros((2, 128, wb), F32).at[0, 0:64].set(a_wup[l, 0]).at[1, 64:128].set(a_wup[l, 1])
    aupp = jnp.zeros((2, 128, wb), F32).at[0, 0:64].set(a_aup[l, 0]).at[1, 64:128].set(a_aup[l, 1])
    small_a = [row(w["a_mu"][l]), row(w["a_kk"][l]), row(w["a_ka"][l]), row(w["a_rk"][l]), w["a_w0"][l],
               w["a_a0"][l], wupp, aupp, w["a_gup"][l], ones_bd]
    r_, v_, kk_, lwf, lwb, ktf, ktb, bf_, bb_, bonus, ga = _sequence_tiles(
        _rwkv_pre_kernel, p, col["a"], a_cols, GRID_W, small_a, [wb] * 11, nb, t_lat, t_ctx, "rwkv_pre")
    ya_f, ya_b = _rwkv_scan(r_, v_, kk_, lwf, lwb, ktf, ktb, bf_, bb_, nb, t_lat, t_ctx)

    kscale = jnp.concatenate([jnp.ones((wb,), F32), jnp.full((wb,), C_HD ** -0.5, F32)]).reshape(1, -1)
    qk, = _sequence_tiles(_conv_kernel, p, col["c_qk"], 2 * wb, 8,
                          [w["c_conv_w"][l], row(w["c_conv_b"][l]), kscale], [2 * wb], nb, t_lat, t_ctx,
                          "mlstm_conv")
    gbias = jnp.zeros((1, 128), F32).at[0, 0:16].set(w["c_gate_b"][l].reshape(-1))
    hc_f, hc_b = _mlstm_scan(qk, p, gbias, col["c_v"], col["c_gate"], nb, t_lat, t_ctx, wb)

    d_aup = w["d_aup"]
    rk_d = d_aup.shape[2]
    aupp_d = (jnp.zeros((2, 128, wb // 2), F32).at[0, 0:rk_d].set(d_aup[l, 0])
              .at[1, rk_d:2 * rk_d].set(d_aup[l, 1]))
    od_f, od_b = _gla_scan(p, aupp_d, w["d_ab"][l].reshape(2, 1, -1), col["d_q"], col["d_k"], col["d_v"],
                           col["d_a"], nb, t_lat, t_ctx, wb)

    b_bias = jnp.repeat(w["b_bs"][l].T, B_CHUNK, axis=1)
    prm_o = {"a_lng": row(w["a_ln_g"][l]), "a_lnb": row(w["a_ln_b"][l]), "ones": ones_bd, "b_ws": w["b_ws"][l],
             "b_bias": b_bias, "b_lng": row(w["b_ln_g"][l]), "b_lnb": row(w["b_ln_b"][l]),
             "c_lng": row(w["c_ln_g"][l]), "d_lng": row(w["d_ln_g"][l])}
    return _branch_out(ya_f, ya_b, bonus, ga, p, hc_f, hc_b, od_f, od_b, prm_o, col, rows, nb, t_lat, t_ctx)


def kernel(x, c, ctx, c_ctx, ada_w, ada_b, norm_g, ffn_w1, ffn_w3, ffn_w2, in_w, in_b, a_mu, a_w0, a_wup, a_a0, a_aup, a_gup, a_kk, a_ka, a_rk, a_ln_g, a_ln_b, b_ws, b_bs, b_ln_g, b_ln_b, c_conv_w, c_conv_b, c_gate_b, c_ln_g, d_aup, d_ab, d_ln_g, br_w, out_w, final_g):
    weights = dict(a_mu=a_mu, a_w0=a_w0, a_wup=a_wup, a_a0=a_a0, a_aup=a_aup, a_gup=a_gup, a_kk=a_kk, a_ka=a_ka,
                   a_rk=a_rk.reshape(a_rk.shape[0], -1), a_ln_g=a_ln_g, a_ln_b=a_ln_b, b_ws=b_ws, b_bs=b_bs,
                   b_ln_g=b_ln_g, b_ln_b=b_ln_b, c_conv_w=c_conv_w, c_conv_b=c_conv_b, c_gate_b=c_gate_b,
                   c_ln_g=c_ln_g, d_aup=d_aup, d_ab=d_ab, d_ln_g=d_ln_g)
    nb, t_lat, d_model = x.shape
    t_ctx = ctx.shape[1]
    depth = ada_w.shape[0]
    wb = d_model // 4
    d_ff = ffn_w1.shape[-1]
    n_lat = nb * t_lat
    n_ctx = nb * t_ctx
    rows_all = n_lat + n_ctx
    assert t_lat % (GRID_W * 2) == 0 and t_ctx % B_CHUNK == 0 and n_lat % t_ctx == 0
    assert wb == 512 and a_wup.shape[2] == 64 and a_aup.shape[2] == 64 and a_gup.shape[1] == 128

    tm = _pick_tile(t_lat, n_ctx)

    def grp(i):
        return jnp.where(i < n_lat // tm, 1 + i // (t_lat // tm), 0)

    tm_s = _pick_tile(t_lat, n_ctx, 512)
    tf = _pick_cols(d_ff, (512, 256, 128))

    h = jnp.concatenate([x.reshape(n_lat, d_model), ctx.reshape(n_ctx, d_model)], axis=0)

    m_pad = -(-(nb + 1) // 8) * 8
    cpad = jnp.zeros((m_pad, d_model), F32).at[0].set(c_ctx).at[1:nb + 1].set(c)
    mod_all = _ada_mod(cpad, ada_w, ada_b, 128)
    mod_all = mod_all.reshape(depth, m_pad, N_MOD, 1, d_model)

    for l in range(depth):
        modp = mod_all[l]
        last = l == depth - 1

        hn = _norm_mod(h, norm_g[l, 0], modp, 0, rows_all, tm_s, lambda i: jnp.where(
            i < n_lat // tm_s, 1 + i // (t_lat // tm_s), 0))
        gact = _ffn_up(hn, ffn_w1, ffn_w3, l, 0, tm, tf)
        h = _ffn_down(gact, ffn_w2, h, modp, 2, l, 0, tm, 256, grp)

        hn = _norm_mod(h, norm_g[l, 1], modp, 3, rows_all, tm_s, lambda i: jnp.where(
            i < n_lat // tm_s, 1 + i // (t_lat // tm_s), 0))
        w_mix, w_gate = _permute_proj(in_w[l], d_model)
        b_mix, b_gate = _permute_proj(in_b[l], d_model)
        p = _in_proj(hn, w_mix.astype(BF16), b_mix.reshape(1, -1), tm, 1152)
        gates = _in_proj(hn, w_gate.astype(BF16), b_gate.reshape(1, -1), tm, 1024, gates=True)

        rows = n_lat if last else rows_all
        hs = _mixers(p, l, nb, t_lat, t_ctx, rows, weights)
        y = _merge(hs, gates, br_w, l, rows, tm, 512)
        h = _out_proj(y, out_w, h, modp, 5, l, tm, 512, grp)

        hn = _norm_mod(h, norm_g[l, 2], modp, 6, rows, tm_s, lambda i: jnp.where(
            i < n_lat // tm_s, 1 + i // (t_lat // tm_s), 0))
        gact = _ffn_up(hn, ffn_w1, ffn_w3, l, 1, tm, tf)
        h = _ffn_down(gact, ffn_w2, h, modp, 8, l, 1, tm, 256, grp)

    out = _final_norm(h, final_g, tm_s)
    return out.reshape(nb, t_lat, d_model)
```

```python
import functools

import jax
import jax.numpy as jnp
from jax import lax
from jax.experimental import pallas as pl
from jax.experimental.pallas import tpu as pltpu

F32 = jnp.float32
BF16 = jnp.bfloat16
HIGHEST = lax.Precision.HIGHEST

EPS = 1e-6
GRID_W = 64
N_MOD = 9
CHUNK = 64
A_HD = 64
A_LN_EPS = 64e-5
B_CHUNK = 128
C_HD = 128
C_CHUNK = 128
C_LN_EPS = 1e-5
D_DK = 64
D_DV = 128
D_TAU = 16.0
D_SUB = 16
D_LN_EPS = 1e-6
VMEM_LIMIT = 56 * 1024 * 1024


def _cp(*sem):
    return pltpu.CompilerParams(dimension_semantics=sem, vmem_limit_bytes=VMEM_LIMIT)


def _operands(a, b, precision):
    if precision is None:
        return a.astype(BF16), b.astype(BF16)
    return a, b


def _mm(a, b, precision=None):
    a, b = _operands(a, b, precision)
    return jnp.dot(a, b, precision=precision, preferred_element_type=F32)


def _nt(a, b, precision=None):
    a, b = _operands(a, b, precision)
    return lax.dot_general(a, b, (((1,), (1,)), ((), ())), precision=precision,
                           preferred_element_type=F32)


def _tn(a, b, precision=None):
    a, b = _operands(a, b, precision)
    return lax.dot_general(a, b, (((0,), (0,)), ((), ())), precision=precision,
                           preferred_element_type=F32)


def _log_sigmoid(x):
    return jnp.minimum(x, 0.0) - jnp.log(1.0 + jnp.exp(-jnp.abs(x)))


def _seg_sum(x, ones_blockdiag):
    hi = x.astype(BF16)
    lo = (x - hi.astype(F32)).astype(BF16)
    return _mm(hi, ones_blockdiag) + _mm(lo, ones_blockdiag)


def _tri(n, rev, strict):
    row = lax.broadcasted_iota(jnp.int32, (n, n), 0)
    col = lax.broadcasted_iota(jnp.int32, (n, n), 1)
    if rev:
        return (col > row) if strict else (col >= row)
    return (col < row) if strict else (col <= row)


def _norm_mod_kernel(h_ref, g_ref, sh_ref, sc_ref, o_ref):
    x = h_ref[...]
    y = x * lax.rsqrt(jnp.mean(x * x, axis=-1, keepdims=True) + EPS) * g_ref[...]
    o_ref[...] = (y * (1.0 + sc_ref[...]) + sh_ref[...]).astype(o_ref.dtype)


def _norm_mod(h, g, modp, k_shift, rows, tm, grp):
    d = h.shape[1]
    return pl.pallas_call(
        _norm_mod_kernel,
        grid=(rows // tm,),
        in_specs=[pl.BlockSpec((tm, d), lambda i: (i, 0)),
                  pl.BlockSpec((1, d), lambda i: (0, 0)),
                  pl.BlockSpec((None, None, 1, d), lambda i: (grp(i), k_shift, 0, 0)),
                  pl.BlockSpec((None, None, 1, d), lambda i: (grp(i), k_shift + 1, 0, 0))],
        out_specs=pl.BlockSpec((tm, d), lambda i: (i, 0)),
        out_shape=jax.ShapeDtypeStruct((rows, d), BF16),
        compiler_params=_cp("arbitrary"),
        name="norm_mod",
    )(h, g.reshape(1, d), modp, modp)


def _final_norm_kernel(h_ref, g_ref, o_ref):
    x = h_ref[...]
    o_ref[...] = x * lax.rsqrt(jnp.mean(x * x, axis=-1, keepdims=True) + EPS) * g_ref[...]


def _final_norm(h, g, tm):
    rows, d = h.shape
    return pl.pallas_call(
        _final_norm_kernel,
        grid=(rows // tm,),
        in_specs=[pl.BlockSpec((tm, d), lambda i: (i, 0)), pl.BlockSpec((1, d), lambda i: (0, 0))],
        out_specs=pl.BlockSpec((tm, d), lambda i: (i, 0)),
        out_shape=jax.ShapeDtypeStruct((rows, d), F32),
        compiler_params=_cp("arbitrary"),
        name="final_norm",
    )(h, g.reshape(1, d))


def _ada_kernel(c_ref, wa_ref, wb_ref, b_ref, o_ref):
    @pl.when(pl.program_id(1) == 0)
    def _():
        o_ref[...] = jnp.broadcast_to(b_ref[...], o_ref.shape)

    c = c_ref[...]
    cond = (c * jax.nn.sigmoid(c)).astype(BF16)
    half = wa_ref.shape[1]
    o_ref[:, 0:half] += _mm(cond, wa_ref[...].astype(BF16))
    o_ref[:, half:] += _mm(cond, wb_ref[...].astype(BF16))


def _ada_mod(cpad, ada_w, ada_b, tk):
    depth, d, n = ada_w.shape
    m = cpad.shape[0]
    return pl.pallas_call(
        _ada_kernel,
        grid=(depth, d // tk),
        in_specs=[pl.BlockSpec((m, tk), lambda l, k: (0, k)),
                  pl.BlockSpec((None, tk, n // 2), lambda l, k: (l, k, 0)),
                  pl.BlockSpec((None, tk, n // 2), lambda l, k: (l, k, 1)),
                  pl.BlockSpec((None, 1, n), lambda l, k: (l, 0, 0))],
        out_specs=pl.BlockSpec((None, m, n), lambda l, k: (l, 0, 0)),
        out_shape=jax.ShapeDtypeStruct((depth, m, n), F32),
        compiler_params=_cp("arbitrary", "arbitrary"),
        name="ada_mod",
    )(cpad, ada_w, ada_w, ada_b.reshape(depth, 1, n))


def _ffn_up_kernel(x_ref, w1_ref, w3_ref, o_ref, w1b, w3b):
    @pl.when(pl.program_id(1) == 0)
    def _():
        w1b[...] = w1_ref[...].astype(BF16)
        w3b[...] = w3_ref[...].astype(BF16)

    x = x_ref[...]
    a = _mm(x, w1b[...])
    b = _mm(x, w3b[...])
    o_ref[...] = (a * jax.nn.sigmoid(a) * b).astype(o_ref.dtype)


def _ffn_up(x, w1, w3, l, s, tm, tf):
    rows, d = x.shape
    f = w1.shape[-1]
    wspec = pl.BlockSpec((None, None, d, tf), lambda j, i: (l, s, 0, j))
    return pl.pallas_call(
        _ffn_up_kernel,
        grid=(f // tf, rows // tm),
        in_specs=[pl.BlockSpec((tm, d), lambda j, i: (i, 0)), wspec, wspec],
        out_specs=pl.BlockSpec((tm, tf), lambda j, i: (i, j)),
        out_shape=jax.ShapeDtypeStruct((rows, f), BF16),
        scratch_shapes=[pltpu.VMEM((d, tf), BF16), pltpu.VMEM((d, tf), BF16)],
        compiler_params=_cp("arbitrary", "arbitrary"),
        name="ffn_up",
    )(x, w1, w3)


def _ffn_down_kernel(g_ref, w_ref, h_ref, gate_ref, o_ref):
    acc = _mm(g_ref[...], w_ref[...])
    o_ref[...] = h_ref[...] + (0.5 * acc) * gate_ref[...]


def _ffn_down(gact, w2, h, modp, k_gate, tm, tn, grp):
    rows, f = gact.shape
    d = h.shape[1]
    return pl.pallas_call(
        _ffn_down_kernel,
        grid=(rows // tm, d // tn),
        in_specs=[pl.BlockSpec((tm, f), lambda i, j: (i, 0)),
                  pl.BlockSpec((f, tn), lambda i, j: (0, j)),
                  pl.BlockSpec((tm, tn), lambda i, j: (i, j)),
                  pl.BlockSpec((None, None, 1, tn), lambda i, j: (grp(i), k_gate, 0, j))],
        out_specs=pl.BlockSpec((tm, tn), lambda i, j: (i, j)),
        out_shape=jax.ShapeDtypeStruct((rows, d), F32),
        compiler_params=_cp("arbitrary", "arbitrary"),
        name="ffn_down",
    )(gact, w2, h, modp)


def _in_proj_kernel(x_ref, w_ref, b_ref, o_ref):
    o_ref[...] = _mm(x_ref[...], w_ref[...]) + b_ref[...]


def _gate_proj_kernel(x_ref, w_ref, b_ref, o_ref):
    o_ref[...] = jax.nn.sigmoid(_mm(x_ref[...], w_ref[...]) + b_ref[...]).astype(o_ref.dtype)


def _in_proj(x, w, b, tm, tn, gates=False):
    rows, d = x.shape
    n = w.shape[1]
    return pl.pallas_call(
        _gate_proj_kernel if gates else _in_proj_kernel,
        grid=(rows // tm, n // tn),
        in_specs=[pl.BlockSpec((tm, d), lambda i, j: (i, 0)),
                  pl.BlockSpec((d, tn), lambda i, j: (0, j)),
                  pl.BlockSpec((1, tn), lambda i, j: (0, j))],
        out_specs=pl.BlockSpec((tm, tn), lambda i, j: (i, j)),
        out_shape=jax.ShapeDtypeStruct((rows, n), BF16 if gates else F32),
        compiler_params=_cp("arbitrary", "arbitrary"),
        name="gate_proj" if gates else "in_proj",
    )(x, w, b)


def _merge_out_kernel(ha, hb, hc, hd, ga, gb, gc, gd, wa, wb, wc, wd, ow_ref, h_ref, gate_ref, o_ref, y_scr, *, nj):
    j = pl.program_id(1)

    @pl.when(j < nj)
    def _():
        y = ga[...].astype(F32) * _mm(ha[...], wa[...])
        y = y + gb[...].astype(F32) * _mm(hb[...], wb[...])
        y = y + gc[...].astype(F32) * _mm(hc[...], wc[...])
        y = y + gd[...].astype(F32) * _mm(hd[...], wd[...])
        y_scr[j] = y.astype(y_scr.dtype)

    @pl.when(j >= nj)
    def _():
        tk = y_scr.shape[2]
        w = ow_ref[...].astype(BF16)
        acc = _mm(y_scr[0], w[0:tk])
        for k in range(1, nj):
            acc = acc + _mm(y_scr[k], w[k * tk:(k + 1) * tk])
        o_ref[...] = h_ref[...] + acc * gate_ref[...]


def _merge_out(hs, gates, br_w, out_w, h, modp, k_gate, l, rows, tm, tn, grp):
    wbr = hs[0].shape[1]
    d = br_w.shape[-1]
    nj = d // tn
    mj = lambda j: jnp.minimum(j, nj - 1)
    oj = lambda j: jnp.maximum(j - nj, 0)
    hspec = pl.BlockSpec((tm, wbr), lambda i, j: (i, 0))
    gspecs = [pl.BlockSpec((tm, tn), functools.partial(lambda i, j, n: (i, n * nj + mj(j)), n=n))
              for n in range(4)]
    wspecs = [pl.BlockSpec((None, None, wbr, tn), functools.partial(lambda i, j, n: (l, n, 0, mj(j)), n=n))
              for n in range(4)]
    return pl.pallas_call(
        functools.partial(_merge_out_kernel, nj=nj),
        grid=(rows // tm, 2 * nj),
        in_specs=[hspec] * 4 + gspecs + wspecs
                 + [pl.BlockSpec((None, d, tn), lambda i, j: (l, 0, oj(j))),
                    pl.BlockSpec((tm, tn), lambda i, j: (i, oj(j))),
                    pl.BlockSpec((None, None, 1, tn), lambda i, j: (grp(i), k_gate, 0, oj(j)))],
        out_specs=pl.BlockSpec((tm, tn), lambda i, j: (i, oj(j))),
        out_shape=jax.ShapeDtypeStruct((rows, d), F32),
        scratch_shapes=[pltpu.VMEM((nj, tm, tn), BF16)],
        compiler_params=_cp("arbitrary", "arbitrary"),
        name="merge_out",
    )(*hs, gates, gates, gates, gates, br_w, br_w, br_w, br_w, out_w, h, modp)


def _tile_place(geom):
    n_lat_tiles, lat_tiles, ctx_tiles = geom
    i = pl.program_id(0)
    is_ctx = i >= n_lat_tiles
    per_seq = jnp.where(is_ctx, ctx_tiles, lat_tiles)
    pos = lax.rem(jnp.where(is_ctx, i - n_lat_tiles, i), per_seq)
    return is_ctx, pos == 0, pos == per_seq - 1


def _row_neighbours(xp_ref, x, xn_ref, first, last):
    ts = x.shape[0]
    hp = xp_ref.shape[0]
    t = lax.broadcasted_iota(jnp.int32, x.shape, 0)
    prv = jnp.where(t == 0, jnp.where(first, 0.0, xp_ref[hp - 1:hp, :]), pltpu.roll(x, 1, 0))
    nxt = jnp.where(t == ts - 1, jnp.where(last, 0.0, xn_ref[0:1, :]), pltpu.roll(x, ts - 1, 0))
    return prv, nxt


def _token_shift(xp_ref, x_ref, xn_ref, mu_ref, geom):
    is_ctx, first, last = _tile_place(geom)
    x = x_ref[...]
    ts = x.shape[0]
    t = lax.broadcasted_iota(jnp.int32, x.shape, 0)
    lane = lax.broadcasted_iota(jnp.int32, x.shape, 1)
    prv, nxt = _row_neighbours(xp_ref, x, xn_ref, first, last)
    sh_ctx = jnp.where((lane & 1) == 0, prv, nxt)
    tw = t & (GRID_W - 1)
    left = jnp.where(tw == 0, 0.0, prv)
    right = jnp.where(tw == GRID_W - 1, 0.0, nxt)
    up = jnp.concatenate([jnp.where(first, 0.0, xp_ref[...]), x[0:ts - GRID_W]], axis=0)
    down = jnp.concatenate([x[GRID_W:ts], jnp.where(last, 0.0, xn_ref[...])], axis=0)
    c4 = lane & 3
    sh_lat = jnp.where(c4 == 0, left, jnp.where(c4 == 1, right, jnp.where(c4 == 2, up, down)))
    sh = jnp.where(is_ctx, sh_ctx, sh_lat)
    return x + (sh - x) * mu_ref[...]


def _conv_kernel(xp_ref, x_ref, xn_ref, w_ref, b_ref, s_ref, o_ref, *, geom):
    _, first, last = _tile_place(geom)
    x = x_ref[...]
    prv, nxt = _row_neighbours(xp_ref, x, xn_ref, first, last)
    w = w_ref[...]
    y = prv * w[0:1] + x * w[1:2] + nxt * w[2:3] + b_ref[...]
    o_ref[...] = y * jax.nn.sigmoid(y) * s_ref[...]


def _sequence_tiles(body, p, col0, width, halo, small, out_widths, nb, t_lat, t_ctx, name):
    rows = p.shape[0]
    ts = min(256, t_ctx)
    assert t_lat % ts == 0 and t_ctx % ts == 0 and ts >= 2 * GRID_W and ts % halo == 0 and col0 % width == 0
    nt = rows // ts
    nh = rows // halo
    r = ts // halo
    geom = (nb * t_lat // ts, t_lat // ts, t_ctx // ts)
    cb = col0 // width
    full = lambda a: pl.BlockSpec(a.shape, lambda i: (0,) * a.ndim)
    return pl.pallas_call(
        functools.partial(body, geom=geom),
        grid=(nt,),
        in_specs=[pl.BlockSpec((halo, width), lambda i: (jnp.maximum(i * r - 1, 0), cb)),
                  pl.BlockSpec((ts, width), lambda i: (i, cb)),
                  pl.BlockSpec((halo, width), lambda i: (jnp.minimum((i + 1) * r, nh - 1), cb))]
                 + [full(a) for a in small],
        out_specs=[pl.BlockSpec((ts, ow), lambda i: (i, 0)) for ow in out_widths],
        out_shape=[jax.ShapeDtypeStruct((rows, ow), F32) for ow in out_widths],
        compiler_params=_cp("arbitrary"),
        name=name,
    )(p, p, p, *small)


def _rwkv_pre_kernel(xp_ref, x_ref, xn_ref, mu_ref, kk_ref, ka_ref, rk_ref, w0_ref, a0_ref, wup_ref, aup_ref,
                     gup_ref, ones_ref, r_o, v_o, kk_o, lwf_o, lwb_o, ktf_o, ktb_o, bf_o, bb_o, bonus_o, g_o, *,
                     geom):
    wb = r_o.shape[1]
    za = _token_shift(xp_ref, x_ref, xn_ref, mu_ref, geom)
    r = za[:, 0:wb]
    k = za[:, wb:2 * wb]
    v = za[:, 2 * wb:3 * wb]
    wd = jnp.tanh(za[:, 3 * wb:3 * wb + 128])
    ad = za[:, 3 * wb + 128:3 * wb + 256]
    gd = jax.nn.sigmoid(za[:, 3 * wb + 256:3 * wb + 384])
    ones = ones_ref[...]
    kq = k * kk_ref[...]
    kk = kq * lax.rsqrt(jnp.maximum(_seg_sum(kq * kq, ones), 1e-24))
    r_o[...] = r
    v_o[...] = v
    kk_o[...] = kk
    ka = ka_ref[...]
    for d, (lw_o, kt_o, b_o) in enumerate(((lwf_o, ktf_o, bf_o), (lwb_o, ktb_o, bb_o))):
        xw = w0_ref[d:d + 1] + _mm(wd, wup_ref[d])
        lw_o[...] = -jax.nn.sigmoid(xw) * 0.6065306597126334
        a = jax.nn.sigmoid(a0_ref[d:d + 1] + _mm(ad, aup_ref[d]))
        kt_o[...] = k * (1.0 + (a - 1.0) * ka)
        b_o[...] = kk * a
    bonus_o[...] = _seg_sum(r * k * rk_ref[...], ones) * v
    g_o[...] = _mm(gd, gup_ref[...])


def _scan_geometry(nb, t_lat, t_ctx, chunk):
    nlc = t_lat // chunk
    ncc = t_ctx // chunk
    nch = nlc + ncc

    def fwd(j):
        return jnp.where(j < ncc, nlc + j, j - ncc)

    def rev(j):
        return nch - 1 - j

    def row(b, c):
        return jnp.where(c < nlc, b * nlc + c, nb * nlc + b * ncc + (c - nlc))

    return fwd, rev, row, nch


def _scan_in_specs(nb, row, cm, chunk, width, colblk):
    return [pl.BlockSpec((chunk, width), functools.partial(lambda j, b: (row(b, cm(j)), colblk), b=b))
            for b in range(nb)]


def _scan_out_spec(nb, cm, chunk, width):
    return pl.BlockSpec((None, nb, chunk, width), lambda j: (cm(j), 0, 0, 0))


def _rwkv_scan_kernel(*refs, nb):
    ins = refs[:12 * nb]
    yf, yb, s_ref = refs[12 * nb:]

    @pl.when(pl.program_id(0) == 0)
    def _():
        s_ref[...] = jnp.zeros_like(s_ref)

    n, wbw = ins[0].shape
    nh = wbw // A_HD
    heads = []
    for d, y_r in enumerate((yf, yb)):
        rev = d == 1
        arr = lambda a, b: ins[(d * 6 + a) * nb + b][...]
        lws = [arr(3, b) for b in range(nb)]
        c_all = _mm(_tri(n, rev, False).astype(F32), jnp.concatenate(lws, axis=1), HIGHEST)
        strict = _tri(n, rev, True)
        incl = _tri(n, rev, False)
        for b in range(nb):
            lw = lws[b]
            c = c_all[:, b * wbw:(b + 1) * wbw]
            eg = jnp.exp(c)
            ieg = jnp.exp(-c)
            r_all = arr(0, b) * eg
            kk_all = arr(2, b) * jnp.exp(c - lw)
            kt_all = arr(4, b) * ieg
            b_all = arr(5, b) * ieg
            v_all = arr(1, b)
            g_last = eg[0:1] if rev else eg[n - 1:n]
            for h in range(nh):
                sl = slice(h * A_HD, (h + 1) * A_HD)
                idx = (d * nb + b) * nh + h
                heads.append(dict(r=r_all[:, sl], kk=kk_all[:, sl], b=b_all[:, sl], kt=kt_all[:, sl],
                                  v=v_all[:, sl], s=s_ref[idx], g=g_last[:, sl], strict=strict, incl=incl,
                                  y_ref=y_r.at[b], sl=sl, idx=idx))
    for t in heads:
        t["z"] = _nt(jnp.concatenate([t["kk"], t["r"]], axis=0), jnp.concatenate([t["kt"], t["b"], t["s"]], axis=0))
    for t in heads:
        z = t["z"]
        t["a_kv"] = jnp.where(t["strict"], z[0:n, 0:n], 0.0)
        t["a_kb"] = jnp.where(t["strict"], z[0:n, n:2 * n], 0.0)
        t["r_kv"] = jnp.where(t["incl"], z[n:2 * n, 0:n], 0.0)
        t["r_kb"] = jnp.where(t["incl"], z[n:2 * n, n:2 * n], 0.0)
    for t in heads:
        t["av"] = _mm(jnp.concatenate([t["a_kv"], t["r_kv"]], axis=0), t["v"])
    left = lax.broadcasted_iota(jnp.int32, (n, 2 * n), 1) < n
    for t in heads:
        t["w"] = jnp.concatenate([t["a_kb"], t["z"][0:n, 2 * n:] + t["av"][0:n]], axis=1)
    for t in heads:
        r = _mm(t["a_kb"], t["w"])
        t["w"] = jnp.where(left, r, t["w"] - r)
    m = 2
    while m < n:
        for t in heads:
            r = _mm(t["w"][:, 0:n], t["w"])
            t["w"] = jnp.where(left, r, t["w"] + r)
        m *= 2
    for t in heads:
        t["u"] = t["w"][:, n:2 * n]
        y = t["z"][n:2 * n, 2 * n:] + t["av"][n:2 * n] - _mm(t["r_kb"], t["u"])
        t["y_ref"][:, t["sl"]] = y
        s_new = t["s"] + _tn(jnp.concatenate([t["v"], t["u"]], axis=0), jnp.concatenate([t["kt"], -t["b"]], axis=0))
        s_ref[t["idx"]] = s_new * t["g"]


def _rwkv_scan(r, v, kk, lwf, lwb, ktf, ktb, bf, bb, nb, t_lat, t_ctx):
    wb = r.shape[1]
    fwd, rev, row, nch = _scan_geometry(nb, t_lat, t_ctx, CHUNK)
    in_specs = [s for cm in (fwd, rev) for _ in range(6) for s in _scan_in_specs(nb, row, cm, CHUNK, wb, 0)]
    args = [a for grp in ((r, v, kk, lwf, ktf, bf), (r, v, kk, lwb, ktb, bb)) for a in grp for _ in range(nb)]
    return pl.pallas_call(
        functools.partial(_rwkv_scan_kernel, nb=nb),
        grid=(nch,),
        in_specs=in_specs,
        out_specs=[_scan_out_spec(nb, fwd, CHUNK, wb), _scan_out_spec(nb, rev, CHUNK, wb)],
        out_shape=[jax.ShapeDtypeStruct((nch, nb, CHUNK, wb), F32)] * 2,
        scratch_shapes=[pltpu.VMEM((2 * nb * (wb // A_HD), A_HD, A_HD), F32)],
        compiler_params=_cp("arbitrary"),
        name="rwkv_scan",
    )(*args)


def _mlstm_scan_kernel(*refs, nb):
    ins = refs[:8 * nb]
    gbias_ref, hf, hb, c_ref, n_ref, m_ref = refs[8 * nb:]

    @pl.when(pl.program_id(0) == 0)
    def _():
        c_ref[...] = jnp.zeros_like(c_ref)
        n_ref[...] = jnp.zeros_like(n_ref)
        m_ref[...] = jnp.zeros_like(m_ref)

    n = ins[0].shape[0]
    nh = ins[0].shape[1] // C_HD
    heads = []
    for d, h_r in enumerate((hf, hb)):
        rev = d == 1
        arr = lambda a, b: ins[(d * 4 + a) * nb + b][...]
        gates = [arr(3, b) + gbias_ref[...] for b in range(nb)]
        fgs = [_log_sigmoid(g) for g in gates]
        mask = _tri(n, rev, False)
        mi = mask.astype(F32)
        bcol_all = _mm(mi, jnp.concatenate(fgs, axis=1), HIGHEST)
        brow_all = _nt(jnp.concatenate([f.T for f in fgs], axis=0), mi, HIGHEST)
        for b in range(nb):
            bcol = bcol_all[:, b * 128:(b + 1) * 128]
            brow = brow_all[b * 128:(b + 1) * 128]
            gates_t = gates[b].T
            q_all, k_all, v_all = arr(0, b), arr(1, b), arr(2, b)
            for h in range(nh):
                sl = slice(h * C_HD, (h + 1) * C_HD)
                ii = d * 2 * nh + h
                fi = ii + nh
                idx = (d * nb + b) * nh + h
                b_c = bcol[:, fi:fi + 1]
                heads.append(dict(q=q_all[:, sl], k=k_all[:, sl], v=v_all[:, sl], b_c=b_c, b_r=brow[fi:fi + 1, :],
                                  i_c=gates[b][:, ii:ii + 1], i_r=gates_t[ii:ii + 1, :], m=m_ref[idx][:, 0:1],
                                  cm=c_ref[idx], nn=n_ref[idx], mask=mask,
                                  b_l=b_c[0:1] if rev else b_c[n - 1:n], idx=idx, h_ref=h_r.at[b], sl=sl))
    for t in heads:
        t["qk"] = _nt(t["q"], t["k"])
        t["qc"] = _nt(t["q"], t["cm"])
    for t in heads:
        t["dlog"] = jnp.where(t["mask"], t["b_c"] - t["b_r"] + t["i_r"], -jnp.inf)
        t["gl"] = t["b_l"] - t["b_c"] + t["i_c"]
    for t in heads:
        t["dmax"] = jnp.max(t["dlog"], axis=1, keepdims=True)
        t["gmax"] = jnp.max(t["gl"], axis=0, keepdims=True)
    for t in heads:
        inter = t["b_c"] + t["m"]
        m_t = jnp.maximum(inter, t["dmax"])
        t["iw"] = jnp.exp(inter - m_t)
        t["m_t"] = m_t
        t["s"] = t["qk"] * jnp.exp(t["dlog"] - m_t)
        m_new = jnp.maximum(t["b_l"] + t["m"], t["gmax"])
        t["sw"] = jnp.exp(t["gl"] - m_new)
        t["dec"] = jnp.exp(t["b_l"] + t["m"] - m_new)
        t["m_new"] = m_new
    for t in heads:
        t["sv"] = _mm(t["s"], t["v"])
        t["vk"] = _tn(t["v"] * t["sw"], t["k"])
    for t in heads:
        t["rs"] = jnp.sum(t["s"], axis=1, keepdims=True)
        t["qn"] = jnp.sum(t["q"] * t["nn"], axis=1, keepdims=True)
        t["ks"] = jnp.sum(t["sw"] * t["k"], axis=0, keepdims=True)
    for t in heads:
        iw = t["iw"]
        num = t["sv"] + iw * t["qc"]
        den = jnp.maximum(jnp.abs(t["rs"] + iw * t["qn"]), jnp.exp(-t["m_t"]))
        t["h_ref"][:, t["sl"]] = num / den
        idx = t["idx"]
        c_ref[idx] = t["dec"] * t["cm"] + t["vk"]
        n_ref[idx] = t["dec"] * t["nn"] + t["ks"]
        m_ref[idx] = jnp.broadcast_to(t["m_new"], m_ref.shape[1:])


def _mlstm_scan(qk, p, gbias, col_v, col_g, nb, t_lat, t_ctx, wb):
    ck = C_CHUNK
    fwd, rev, row, nch = _scan_geometry(nb, t_lat, t_ctx, ck)
    nh = wb // C_HD

    def specs(cm):
        return (_scan_in_specs(nb, row, cm, ck, wb, 0) + _scan_in_specs(nb, row, cm, ck, wb, 1)
                + _scan_in_specs(nb, row, cm, ck, wb, col_v // wb) + _scan_in_specs(nb, row, cm, ck, 128, col_g // 128))

    args = ([qk] * nb + [qk] * nb + [p] * nb + [p] * nb) * 2
    return pl.pallas_call(
        functools.partial(_mlstm_scan_kernel, nb=nb),
        grid=(nch,),
        in_specs=specs(fwd) + specs(rev) + [pl.BlockSpec((1, 128), lambda j: (0, 0))],
        out_specs=[_scan_out_spec(nb, fwd, ck, wb), _scan_out_spec(nb, rev, ck, wb)],
        out_shape=[jax.ShapeDtypeStruct((nch, nb, ck, wb), F32)] * 2,
        scratch_shapes=[pltpu.VMEM((2 * nb * nh, C_HD, C_HD), F32), pltpu.VMEM((2 * nb * nh, 1, C_HD), F32),
                        pltpu.VMEM((2 * nb * nh, 1, 128), F32)],
        compiler_params=_cp("arbitrary"),
        name="mlstm_scan",
    )(*args, gbias)


def _gla_scan_kernel(*refs, nb):
    ins = refs[:8 * nb]
    aup_ref, abias_ref, of, ob, s_ref = refs[8 * nb:]

    @pl.when(pl.program_id(0) == 0)
    def _():
        s_ref[...] = jnp.zeros_like(s_ref)

    n, wkw = ins[0].shape
    nh = wkw // D_DK
    nsub = n // D_SUB
    pw = 2 * D_DK
    row = lax.broadcasted_iota(jnp.int32, (n, pw), 0)
    colx = lax.broadcasted_iota(jnp.int32, (n, pw), 1) & (D_DK - 1)
    rel = colx - (row & -D_SUB)
    rin = row & (D_SUB - 1)
    same_head = ((lax.broadcasted_iota(jnp.int32, (pw, pw), 0) & D_DK)
                 == (lax.broadcasted_iota(jnp.int32, (pw, pw), 1) & D_DK))
    ones_pair = jnp.where(same_head, 1.0, 0.0).astype(BF16)
    heads, pairs = [], []
    for d, o_r in enumerate((of, ob)):
        rev = d == 1
        arr = lambda a, b: ins[(d * 4 + a) * nb + b][...]
        la_rows = _log_sigmoid(_mm(jnp.concatenate([arr(3, b) for b in range(nb)], axis=0), aup_ref[d])
                               + abias_ref[d]) * (1.0 / D_TAU)
        la_all = jnp.concatenate([la_rows[b * n:(b + 1) * n] for b in range(nb)], axis=1)
        bc_all = _mm(_tri(n, rev, False).astype(F32), la_all, HIGHEST)
        for b in range(nb):
            bc = bc_all[:, b * wkw:(b + 1) * wkw]
            q_all = arr(0, b) * (D_DK ** -0.5)
            k_all = arr(1, b)
            v_all = arr(2, b)
            base = (d * nb + b) * nh
            for h in range(nh):
                ksl = slice(h * D_DK, (h + 1) * D_DK)
                vsl = slice(h * D_DV, (h + 1) * D_DV)
                heads.append(dict(q=q_all[:, ksl], k=k_all[:, ksl], v=v_all[:, vsl], bc=bc[:, ksl], rev=rev,
                                  s=s_ref[base + h], idx=base + h, o_ref=o_r.at[b], vsl=vsl))
            for p in range(nh // 2):
                psl = slice(p * pw, (p + 1) * pw)
                pairs.append(dict(q=q_all[:, psl], k=k_all[:, psl], bc=bc[:, psl], rev=rev,
                                  heads=(base + 2 * p, base + 2 * p + 1)))
    for t in heads:
        t["o"] = _nt(t["q"] * jnp.exp(t["bc"]), t["s"])
    for t in heads:
        q, k, bc, rev = t["q"], t["k"], t["bc"], t["rev"]
        pieces = []
        for blk in range(nsub):
            r0 = blk * D_SUB
            edge = bc[r0 + D_SUB - 1:r0 + D_SUB] if rev else bc[r0:r0 + 1]
            has_other = blk < nsub - 1 if rev else blk > 0
            if has_other:
                qs = q[r0:r0 + D_SUB] * jnp.exp(bc[r0:r0 + D_SUB] - edge)
                ks = k * jnp.exp(jnp.minimum(edge - bc, 0.0))
                pieces.append(_nt(qs, ks))
            else:
                pieces.append(jnp.zeros((D_SUB, n), F32))
        t["sc"] = jnp.concatenate(pieces, axis=0)
    for t in pairs:
        q, k, bc = t["q"], t["k"], t["bc"]
        es = []
        for j in range(D_SUB):
            kj = jnp.concatenate([jnp.broadcast_to(k[b * D_SUB + j:b * D_SUB + j + 1], (D_SUB, pw))
                                  for b in range(nsub)], axis=0)
            bj = jnp.concatenate([jnp.broadcast_to(bc[b * D_SUB + j:b * D_SUB + j + 1], (D_SUB, pw))
                                  for b in range(nsub)], axis=0)
            es.append(q * kj * jnp.exp(bc - bj))
        t["e"] = jnp.concatenate(es, axis=0)
    for t in pairs:
        t["c"] = _seg_sum(t["e"], ones_pair)
    for t in pairs:
        rev = t["rev"]
        sc = jnp.concatenate([heads[t["heads"][0]]["sc"], heads[t["heads"][1]]["sc"]], axis=1)
        a = jnp.where((rel >= D_SUB) if rev else (rel < 0), sc, 0.0)
        for j in range(D_SUB):
            keep = (rel == j) & ((rin <= j) if rev else (rin >= j))
            a = jnp.where(keep, t["c"][j * n:(j + 1) * n], a)
        heads[t["heads"][0]]["a"] = a[:, 0:D_DK]
        heads[t["heads"][1]]["a"] = a[:, D_DK:pw]
    for t in heads:
        t["o_ref"][:, t["vsl"]] = t["o"] + _mm(t["a"], t["v"])
        bc = t["bc"]
        b_l = bc[0:1] if t["rev"] else bc[n - 1:n]
        s_ref[t["idx"]] = t["s"] * jnp.exp(b_l) + _tn(t["v"], t["k"] * jnp.exp(b_l - bc))


def _gla_scan(p, aupp, abias, col_q, col_k, col_v, col_a, nb, t_lat, t_ctx, wb):
    fwd, rev, row, nch = _scan_geometry(nb, t_lat, t_ctx, CHUNK)
    wk = aupp.shape[-1]
    nh = wk // D_DK

    def specs(cm):
        return (_scan_in_specs(nb, row, cm, CHUNK, wk, col_q // wk) + _scan_in_specs(nb, row, cm, CHUNK, wk, col_k // wk)
                + _scan_in_specs(nb, row, cm, CHUNK, wb, col_v // wb)
                + _scan_in_specs(nb, row, cm, CHUNK, 128, col_a // 128))

    return pl.pallas_call(
        functools.partial(_gla_scan_kernel, nb=nb),
        grid=(nch,),
        in_specs=specs(fwd) + specs(rev) + [pl.BlockSpec(aupp.shape, lambda j: (0, 0, 0)),
                                           pl.BlockSpec(abias.shape, lambda j: (0, 0, 0))],
        out_specs=[_scan_out_spec(nb, fwd, CHUNK, wb), _scan_out_spec(nb, rev, CHUNK, wb)],
        out_shape=[jax.ShapeDtypeStruct((nch, nb, CHUNK, wb), F32)] * 2,
        scratch_shapes=[pltpu.VMEM((2 * nb * nh, D_DV, D_DK), F32)],
        compiler_params=_cp("arbitrary"),
        name="gla_scan",
    )(*([p] * (8 * nb)), aupp, abias)


def _branch_out_kernel(ya_f, ya_b, bonus, ga, pb, hc_f, hc_b, oc, od_f, od_b, gdd,
                       a_lng, a_lnb, ones, b_ws, b_bias, b_lng, b_lnb, c_lng, d_lng,
                       hs_a, hs_b, hs_c, hs_d):
    wb = hs_a.shape[1]
    tm = hs_a.shape[0]
    y = ya_f[...].reshape(tm, wb) + ya_b[...].reshape(tm, wb)
    on = ones[...]
    mu = _seg_sum(y, on) * (1.0 / A_HD)
    yc = y - mu
    var = _seg_sum(yc * yc, on) * (1.0 / A_HD)
    yn = yc * lax.rsqrt(var + A_LN_EPS) * a_lng[...] + a_lnb[...]
    hs_a[...] = ((yn + bonus[...]) * ga[...]).astype(hs_a.dtype)
    z = jax.nn.gelu(pb[...])
    u = z[:, 0:wb]
    vv = z[:, wb:2 * wb]
    mu = jnp.mean(vv, axis=-1, keepdims=True)
    vc = vv - mu
    vn = (vc * lax.rsqrt(jnp.mean(vc * vc, axis=-1, keepdims=True) + 1e-5) * b_lng[...] + b_lnb[...])
    vn = vn.astype(BF16)
    for ck in range(u.shape[0] // B_CHUNK):
        rs = slice(ck * B_CHUNK, (ck + 1) * B_CHUNK)
        for g in range(wb // B_CHUNK):
            cs = slice(g * B_CHUNK, (g + 1) * B_CHUNK)
            s = _mm(b_ws[g].astype(BF16), vn[rs, cs]) + b_bias[:, cs]
            hs_b[rs, cs] = (u[rs, cs] * s).astype(hs_b.dtype)
    hc = hc_f[...].reshape(tm, wb) + hc_b[...].reshape(tm, wb)
    ogate = jax.nn.sigmoid(oc[...])
    gc = c_lng[...]
    for h in range(wb // C_HD):
        sl = slice(h * C_HD, (h + 1) * C_HD)
        x = hc[:, sl]
        xc = x - jnp.mean(x, axis=-1, keepdims=True)
        xn = xc * lax.rsqrt(jnp.mean(xc * xc, axis=-1, keepdims=True) + C_LN_EPS) * gc[:, sl]
        hs_c[:, sl] = (xn * ogate[:, sl]).astype(hs_c.dtype)
    od = od_f[...].reshape(tm, wb) + od_b[...].reshape(tm, wb)
    gg = gdd[...]
    gg = gg * jax.nn.sigmoid(gg)
    gd_ = d_lng[...]
    for h in range(wb // D_DV):
        sl = slice(h * D_DV, (h + 1) * D_DV)
        x = od[:, sl]
        xn = x * lax.rsqrt(jnp.mean(x * x, axis=-1, keepdims=True) + D_LN_EPS) * gd_[:, sl]
        hs_d[:, sl] = (xn * gg[:, sl]).astype(hs_d.dtype)


def _branch_out(ya_f, ya_b, bonus, ga, p, hc_f, hc_b, od_f, od_b, prm, cols, rows, nb, t_lat, t_ctx):
    wb = bonus.shape[1]
    tm = min(256, t_ctx)
    n_lat_tiles, lat_tiles, ctx_tiles = nb * t_lat // tm, t_lat // tm, t_ctx // tm
    row = lambda w, cb: pl.BlockSpec((tm, w), lambda i: (i, cb))

    def scan_out(a):
        per_tile = tm // a.shape[2]

        def imap(i):
            ic = i - n_lat_tiles
            seq = jnp.where(i < n_lat_tiles, i // lat_tiles, ic // ctx_tiles)
            blk = jnp.where(i < n_lat_tiles, lax.rem(i, lat_tiles), lat_tiles + lax.rem(ic, ctx_tiles))
            return blk, seq, 0, 0

        return pl.BlockSpec((per_tile, None, a.shape[2], wb), imap)

    small = [prm["a_lng"], prm["a_lnb"], prm["ones"], prm["b_ws"], prm["b_bias"], prm["b_lng"], prm["b_lnb"],
             prm["c_lng"], prm["d_lng"]]
    full = lambda a: pl.BlockSpec(a.shape, lambda i: (0,) * a.ndim)
    return pl.pallas_call(
        _branch_out_kernel,
        grid=(rows // tm,),
        in_specs=[scan_out(ya_f), scan_out(ya_b), row(wb, 0), row(wb, 0), row(2 * wb, cols["b"] // (2 * wb)),
                  scan_out(hc_f), scan_out(hc_b), row(wb, cols["c_o"] // wb), scan_out(od_f), scan_out(od_b),
                  row(wb, cols["d_g"] // wb)] + [full(a) for a in small],
        out_specs=[row(wb, 0)] * 4,
        out_shape=[jax.ShapeDtypeStruct((rows, wb), BF16)] * 4,
        compiler_params=_cp("arbitrary"),
        name="branch_out",
    )(ya_f, ya_b, bonus, ga, p, hc_f, hc_b, p, od_f, od_b, p, *small)


def _pick_tile(*sizes):
    for t in (1024, 512, 256, 128):
        if all(s % t == 0 for s in sizes):
            return t
    raise ValueError("token counts must be multiples of 128")


def _pick_cols(n, cands):
    for t in cands:
        if n % t == 0:
            return t
    raise ValueError(f"no column tile for {n}")


def _proj_cols(d_model):
    wb = d_model // 4
    col = {"a": 0, "b": d_model, "c_qk": d_model + 2 * wb}
    col["c_v"] = col["c_qk"] + 2 * wb
    col["c_o"] = col["c_v"] + wb
    col["d_q"] = col["c_o"] + wb
    col["d_k"] = col["d_q"] + wb // 2
    col["d_v"] = col["d_k"] + wb // 2
    col["d_g"] = col["d_v"] + wb
    col["c_gate"] = col["d_g"] + wb
    col["d_a"] = col["c_gate"] + 128
    assert 3 * wb + 384 <= d_model
    return col


def _permute_proj(a, d_model):
    wb = d_model // 4
    a_cols = 3 * wb + 384
    o_b = a_cols
    o_c = o_b + 2 * wb
    o_d = o_c + 4 * wb + 16
    o_g = o_d + 3 * wb + 32
    z = lambda k: jnp.zeros(a.shape[:-1] + (k,), a.dtype)
    mix = jnp.concatenate([
        a[..., 0:a_cols], z(d_model - a_cols), a[..., o_b:o_b + 2 * wb], a[..., o_c:o_c + 4 * wb],
        a[..., o_d:o_d + 3 * wb], a[..., o_c + 4 * wb:o_c + 4 * wb + 16], z(112),
        a[..., o_d + 3 * wb:o_d + 3 * wb + 32], z(96)], axis=-1)
    return mix, a[..., o_g:o_g + 4 * d_model]


def _mixers(p, l, nb, t_lat, t_ctx, rows, w):
    wb = w["a_kk"].shape[1]
    col = _proj_cols(4 * wb)
    a_cols = 3 * wb + 384
    row = lambda a: a.reshape(1, -1)
    ones_bd = jnp.kron(jnp.eye(wb // A_HD, dtype=F32), jnp.ones((A_HD, A_HD), F32)).astype(BF16)

    a_wup, a_aup = w["a_wup"], w["a_aup"]
    wupp = jnp.zeros((2, 128, wb), F32).at[0, 0:64].set(a_wup[l, 0]).at[1, 64:128].set(a_wup[l, 1])
    aupp = jnp.zeros((2, 128, wb), F32).at[0, 0:64].set(a_aup[l, 0]).at[1, 64:128].set(a_aup[l, 1])
    small_a = [row(w["a_mu"][l]), row(w["a_kk"][l]), row(w["a_ka"][l]), row(w["a_rk"][l]), w["a_w0"][l],
               w["a_a0"][l], wupp, aupp, w["a_gup"][l], ones_bd]
    r_, v_, kk_, lwf, lwb, ktf, ktb, bf_, bb_, bonus, ga = _sequence_tiles(
        _rwkv_pre_kernel, p, col["a"], a_cols, GRID_W, small_a, [wb] * 11, nb, t_lat, t_ctx, "rwkv_pre")
    ya_f, ya_b = _rwkv_scan(r_, v_, kk_, lwf, lwb, ktf, ktb, bf_, bb_, nb, t_lat, t_ctx)

    kscale = jnp.concatenate([jnp.ones((wb,), F32), jnp.full((wb,), C_HD ** -0.5, F32)]).reshape(1, -1)
    qk, = _sequence_tiles(_conv_kernel, p, col["c_qk"], 2 * wb, 8,
                          [w["c_conv_w"][l], row(w["c_conv_b"][l]), kscale], [2 * wb], nb, t_lat, t_ctx,
                          "mlstm_conv")
    gbias = jnp.zeros((1, 128), F32).at[0, 0:16].set(w["c_gate_b"][l].reshape(-1))
    hc_f, hc_b = _mlstm_scan(qk, p, gbias, col["c_v"], col["c_gate"], nb, t_lat, t_ctx, wb)

    d_aup = w["d_aup"]
    rk_d = d_aup.shape[2]
    aupp_d = (jnp.zeros((2, 128, wb // 2), F32).at[0, 0:rk_d].set(d_aup[l, 0])
              .at[1, rk_d:2 * rk_d].set(d_aup[l, 1]))
    od_f, od_b = _gla_scan(p, aupp_d, w["d_ab"][l].reshape(2, 1, -1), col["d_q"], col["d_k"], col["d_v"],
                           col["d_a"], nb, t_lat, t_ctx, wb)

    b_bias = jnp.repeat(w["b_bs"][l].T, B_CHUNK, axis=1)
    prm_o = {"a_lng": row(w["a_ln_g"][l]), "a_lnb": row(w["a_ln_b"][l]), "ones": ones_bd, "b_ws": w["b_ws"][l],
             "b_bias": b_bias, "b_lng": row(w["b_ln_g"][l]), "b_lnb": row(w["b_ln_b"][l]),
             "c_lng": row(w["c_ln_g"][l]), "d_lng": row(w["d_ln_g"][l])}
    return _branch_out(ya_f, ya_b, bonus, ga, p, hc_f, hc_b, od_f, od_b, prm_o, col, rows, nb, t_lat, t_ctx)


def kernel(x, c, ctx, c_ctx, ada_w, ada_b, norm_g, ffn_w1, ffn_w3, ffn_w2, in_w, in_b, a_mu, a_w0, a_wup, a_a0, a_aup, a_gup, a_kk, a_ka, a_rk, a_ln_g, a_ln_b, b_ws, b_bs, b_ln_g, b_ln_b, c_conv_w, c_conv_b, c_gate_b, c_ln_g, d_aup, d_ab, d_ln_g, br_w, out_w, final_g):
    weights = dict(a_mu=a_mu, a_w0=a_w0, a_wup=a_wup, a_a0=a_a0, a_aup=a_aup, a_gup=a_gup, a_kk=a_kk, a_ka=a_ka,
                   a_rk=a_rk.reshape(a_rk.shape[0], -1), a_ln_g=a_ln_g, a_ln_b=a_ln_b, b_ws=b_ws, b_bs=b_bs,
                   b_ln_g=b_ln_g, b_ln_b=b_ln_b, c_conv_w=c_conv_w, c_conv_b=c_conv_b, c_gate_b=c_gate_b,
                   c_ln_g=c_ln_g, d_aup=d_aup, d_ab=d_ab, d_ln_g=d_ln_g)
    nb, t_lat, d_model = x.shape
    t_ctx = ctx.shape[1]
    depth = ada_w.shape[0]
    wb = d_model // 4
    d_ff = ffn_w1.shape[-1]
    n_lat = nb * t_lat
    n_ctx = nb * t_ctx
    rows_all = n_lat + n_ctx
    assert t_lat % (GRID_W * 2) == 0 and t_ctx % B_CHUNK == 0 and n_lat % t_ctx == 0
    assert wb == 512 and a_wup.shape[2] == 64 and a_aup.shape[2] == 64 and a_gup.shape[1] == 128

    tm = _pick_tile(t_lat, n_ctx)

    def grp(i):
        return jnp.where(i < n_lat // tm, 1 + i // (t_lat // tm), 0)

    tm_s = _pick_tile(t_lat, n_ctx, 512)
    tf = _pick_cols(d_ff, (512, 256, 128))

    h = jnp.concatenate([x.reshape(n_lat, d_model), ctx.reshape(n_ctx, d_model)], axis=0)

    m_pad = -(-(nb + 1) // 8) * 8
    cpad = jnp.zeros((m_pad, d_model), F32).at[0].set(c_ctx).at[1:nb + 1].set(c)
    mod_all = _ada_mod(cpad, ada_w, ada_b, 128)
    mod_all = mod_all.reshape(depth, m_pad, N_MOD, 1, d_model)

    for l in range(depth):
        modp = mod_all[l]
        last = l == depth - 1

        hn = _norm_mod(h, norm_g[l, 0], modp, 0, rows_all, tm_s, lambda i: jnp.where(
            i < n_lat // tm_s, 1 + i // (t_lat // tm_s), 0))
        gact = _ffn_up(hn, ffn_w1, ffn_w3, l, 0, tm, tf)
        h = _ffn_down(gact, ffn_w2[l, 0].astype(BF16), h, modp, 2, tm, 512, grp)

        hn = _norm_mod(h, norm_g[l, 1], modp, 3, rows_all, tm_s, lambda i: jnp.where(
            i < n_lat // tm_s, 1 + i // (t_lat // tm_s), 0))
        w_mix, w_gate = _permute_proj(in_w[l], d_model)
        b_mix, b_gate = _permute_proj(in_b[l], d_model)
        p = _in_proj(hn, w_mix.astype(BF16), b_mix.reshape(1, -1), tm, 1152)
        gates = _in_proj(hn, w_gate.astype(BF16), b_gate.reshape(1, -1), tm, 1024, gates=True)

        rows = n_lat if last else rows_all
        hs = _mixers(p, l, nb, t_lat, t_ctx, rows, weights)
        h = _merge_out(hs, gates, br_w, out_w, h, modp, 5, l, rows, tm, 512, grp)

        hn = _norm_mod(h, norm_g[l, 2], modp, 6, rows, tm_s, lambda i: jnp.where(
            i < n_lat // tm_s, 1 + i // (t_lat // tm_s), 0))
        gact = _ffn_up(hn, ffn_w1, ffn_w3, l, 1, tm, tf)
        h = _ffn_down(gact, ffn_w2[l, 1].astype(BF16), h, modp, 8, tm, 512, grp)

    out = _final_norm(h, final_g, tm_s)
    return out.reshape(nb, t_lat, d_model)
```

```python
import functools

import jax
import jax.numpy as jnp
from jax import lax
from jax.experimental import pallas as pl
from jax.experimental.pallas import tpu as pltpu

F32 = jnp.float32
BF16 = jnp.bfloat16
HIGHEST = lax.Precision.HIGHEST

EPS = 1e-6
GRID_W = 64
N_MOD = 9
CHUNK = 64
A_HD = 64
A_LN_EPS = 64e-5
B_CHUNK = 128
C_HD = 128
C_CHUNK = 128
C_LN_EPS = 1e-5
D_DK = 64
D_DV = 128
D_TAU = 16.0
D_SUB = 16
D_LN_EPS = 1e-6
VMEM_LIMIT = 56 * 1024 * 1024


def _cp(*sem):
    return pltpu.CompilerParams(dimension_semantics=sem, vmem_limit_bytes=VMEM_LIMIT)


def _operands(a, b, precision):
    if precision is None:
        return a.astype(BF16), b.astype(BF16)
    return a, b


def _mm(a, b, precision=None):
    a, b = _operands(a, b, precision)
    return jnp.dot(a, b, precision=precision, preferred_element_type=F32)


def _nt(a, b, precision=None):
    a, b = _operands(a, b, precision)
    return lax.dot_general(a, b, (((1,), (1,)), ((), ())), precision=precision,
                           preferred_element_type=F32)


def _tn(a, b, precision=None):
    a, b = _operands(a, b, precision)
    return lax.dot_general(a, b, (((0,), (0,)), ((), ())), precision=precision,
                           preferred_element_type=F32)


def _log_sigmoid(x):
    return jnp.minimum(x, 0.0) - jnp.log(1.0 + jnp.exp(-jnp.abs(x)))


def _seg_sum(x, ones_blockdiag):
    hi = x.astype(BF16)
    lo = (x - hi.astype(F32)).astype(BF16)
    return _mm(hi, ones_blockdiag) + _mm(lo, ones_blockdiag)


def _tri(n, rev, strict):
    row = lax.broadcasted_iota(jnp.int32, (n, n), 0)
    col = lax.broadcasted_iota(jnp.int32, (n, n), 1)
    if rev:
        return (col > row) if strict else (col >= row)
    return (col < row) if strict else (col <= row)


def _norm_mod_kernel(h_ref, g_ref, sh_ref, sc_ref, o_ref):
    x = h_ref[...]
    y = x * lax.rsqrt(jnp.mean(x * x, axis=-1, keepdims=True) + EPS) * g_ref[...]
    o_ref[...] = (y * (1.0 + sc_ref[...]) + sh_ref[...]).astype(o_ref.dtype)


def _norm_mod(h, g, modp, k_shift, rows, tm, grp):
    d = h.shape[1]
    return pl.pallas_call(
        _norm_mod_kernel,
        grid=(rows // tm,),
        in_specs=[pl.BlockSpec((tm, d), lambda i: (i, 0)),
                  pl.BlockSpec((1, d), lambda i: (0, 0)),
                  pl.BlockSpec((None, None, 1, d), lambda i: (grp(i), k_shift, 0, 0)),
                  pl.BlockSpec((None, None, 1, d), lambda i: (grp(i), k_shift + 1, 0, 0))],
        out_specs=pl.BlockSpec((tm, d), lambda i: (i, 0)),
        out_shape=jax.ShapeDtypeStruct((rows, d), BF16),
        compiler_params=_cp("arbitrary"),
        name="norm_mod",
    )(h, g.reshape(1, d), modp, modp)


def _final_norm_kernel(h_ref, g_ref, o_ref):
    x = h_ref[...]
    o_ref[...] = x * lax.rsqrt(jnp.mean(x * x, axis=-1, keepdims=True) + EPS) * g_ref[...]


def _final_norm(h, g, tm):
    rows, d = h.shape
    return pl.pallas_call(
        _final_norm_kernel,
        grid=(rows // tm,),
        in_specs=[pl.BlockSpec((tm, d), lambda i: (i, 0)), pl.BlockSpec((1, d), lambda i: (0, 0))],
        out_specs=pl.BlockSpec((tm, d), lambda i: (i, 0)),
        out_shape=jax.ShapeDtypeStruct((rows, d), F32),
        compiler_params=_cp("arbitrary"),
        name="final_norm",
    )(h, g.reshape(1, d))


def _ada_kernel(c_ref, wa_ref, wb_ref, b_ref, o_ref):
    @pl.when(pl.program_id(1) == 0)
    def _():
        o_ref[...] = jnp.broadcast_to(b_ref[...], o_ref.shape)

    c = c_ref[...]
    cond = (c * jax.nn.sigmoid(c)).astype(BF16)
    half = wa_ref.shape[1]
    o_ref[:, 0:half] += _mm(cond, wa_ref[...].astype(BF16))
    o_ref[:, half:] += _mm(cond, wb_ref[...].astype(BF16))


def _ada_mod(cpad, ada_w, ada_b, tk):
    depth, d, n = ada_w.shape
    m = cpad.shape[0]
    return pl.pallas_call(
        _ada_kernel,
        grid=(depth, d // tk),
        in_specs=[pl.BlockSpec((m, tk), lambda l, k: (0, k)),
                  pl.BlockSpec((None, tk, n // 2), lambda l, k: (l, k, 0)),
                  pl.BlockSpec((None, tk, n // 2), lambda l, k: (l, k, 1)),
                  pl.BlockSpec((None, 1, n), lambda l, k: (l, 0, 0))],
        out_specs=pl.BlockSpec((None, m, n), lambda l, k: (l, 0, 0)),
        out_shape=jax.ShapeDtypeStruct((depth, m, n), F32),
        compiler_params=_cp("arbitrary", "arbitrary"),
        name="ada_mod",
    )(cpad, ada_w, ada_w, ada_b.reshape(depth, 1, n))


def _ffn_up_kernel(x_ref, w1_ref, w3_ref, o_ref, w1b, w3b):
    @pl.when(pl.program_id(1) == 0)
    def _():
        w1b[...] = w1_ref[...].astype(BF16)
        w3b[...] = w3_ref[...].astype(BF16)

    x = x_ref[...]
    a = _mm(x, w1b[...])
    b = _mm(x, w3b[...])
    o_ref[...] = (a * jax.nn.sigmoid(a) * b).astype(o_ref.dtype)


def _ffn_up(x, w1, w3, l, s, tm, tf):
    rows, d = x.shape
    f = w1.shape[-1]
    wspec = pl.BlockSpec((None, None, d, tf), lambda j, i: (l, s, 0, j))
    return pl.pallas_call(
        _ffn_up_kernel,
        grid=(f // tf, rows // tm),
        in_specs=[pl.BlockSpec((tm, d), lambda j, i: (i, 0)), wspec, wspec],
        out_specs=pl.BlockSpec((tm, tf), lambda j, i: (i, j)),
        out_shape=jax.ShapeDtypeStruct((rows, f), BF16),
        scratch_shapes=[pltpu.VMEM((d, tf), BF16), pltpu.VMEM((d, tf), BF16)],
        compiler_params=_cp("arbitrary", "arbitrary"),
        name="ffn_up",
    )(x, w1, w3)


def _ffn_down_kernel(g_ref, w_ref, h_ref, gate_ref, o_ref):
    acc = _mm(g_ref[...], w_ref[...])
    o_ref[...] = h_ref[...] + (0.5 * acc) * gate_ref[...]


def _ffn_down(gact, w2, h, modp, k_gate, tm, tn, grp):
    rows, f = gact.shape
    d = h.shape[1]
    return pl.pallas_call(
        _ffn_down_kernel,
        grid=(rows // tm, d // tn),
        in_specs=[pl.BlockSpec((tm, f), lambda i, j: (i, 0)),
                  pl.BlockSpec((f, tn), lambda i, j: (0, j)),
                  pl.BlockSpec((tm, tn), lambda i, j: (i, j)),
                  pl.BlockSpec((None, None, 1, tn), lambda i, j: (grp(i), k_gate, 0, j))],
        out_specs=pl.BlockSpec((tm, tn), lambda i, j: (i, j)),
        out_shape=jax.ShapeDtypeStruct((rows, d), F32),
        compiler_params=_cp("arbitrary", "arbitrary"),
        name="ffn_down",
    )(gact, w2, h, modp)


def _in_proj_kernel(x_ref, w_ref, b_ref, o_ref):
    o_ref[...] = _mm(x_ref[...], w_ref[...]) + b_ref[...]


def _gate_proj_kernel(x_ref, w_ref, b_ref, o_ref):
    o_ref[...] = jax.nn.sigmoid(_mm(x_ref[...], w_ref[...]) + b_ref[...]).astype(o_ref.dtype)


def _in_proj(x, w, b, rows, tm, tn, gates=False):
    d = x.shape[1]
    n = w.shape[1]
    return pl.pallas_call(
        _gate_proj_kernel if gates else _in_proj_kernel,
        grid=(rows // tm, n // tn),
        in_specs=[pl.BlockSpec((tm, d), lambda i, j: (i, 0)),
                  pl.BlockSpec((d, tn), lambda i, j: (0, j)),
                  pl.BlockSpec((1, tn), lambda i, j: (0, j))],
        out_specs=pl.BlockSpec((tm, tn), lambda i, j: (i, j)),
        out_shape=jax.ShapeDtypeStruct((rows, n), BF16 if gates else F32),
        compiler_params=_cp("arbitrary", "arbitrary"),
        name="gate_proj" if gates else "in_proj",
    )(x, w, b)


def _merge_out_kernel(ha, hb, hc, hd, ga, gb, gc, gd, wa, wb, wc, wd, ow_ref, h_ref, gate_ref, o_ref, y_scr, *, nj):
    j = pl.program_id(1)

    @pl.when(j < nj)
    def _():
        y = ga[...].astype(F32) * _mm(ha[...], wa[...])
        y = y + gb[...].astype(F32) * _mm(hb[...], wb[...])
        y = y + gc[...].astype(F32) * _mm(hc[...], wc[...])
        y = y + gd[...].astype(F32) * _mm(hd[...], wd[...])
        y_scr[j] = y.astype(y_scr.dtype)

    @pl.when(j >= nj)
    def _():
        tk = y_scr.shape[2]
        w = ow_ref[...]
        acc = _mm(y_scr[0], w[0:tk])
        for k in range(1, nj):
            acc = acc + _mm(y_scr[k], w[k * tk:(k + 1) * tk])
        o_ref[...] = h_ref[...] + acc * gate_ref[...]


def _merge_out(hs, gates, br_w, out_w, h, modp, k_gate, l, rows, tm, tn, grp):
    wbr = hs[0].shape[1]
    d = br_w.shape[-1]
    nj = d // tn
    mj = lambda j: jnp.minimum(j, nj - 1)
    oj = lambda j: jnp.maximum(j - nj, 0)
    hspec = pl.BlockSpec((tm, wbr), lambda i, j: (i, 0))
    gspecs = [pl.BlockSpec((tm, tn), functools.partial(lambda i, j, n: (i, n * nj + mj(j)), n=n))
              for n in range(4)]
    wspecs = [pl.BlockSpec((None, None, wbr, tn), functools.partial(lambda i, j, n: (l, n, 0, mj(j)), n=n))
              for n in range(4)]
    return pl.pallas_call(
        functools.partial(_merge_out_kernel, nj=nj),
        grid=(rows // tm, 2 * nj),
        in_specs=[hspec] * 4 + gspecs + wspecs
                 + [pl.BlockSpec((None, d, tn), lambda i, j: (l, 0, oj(j))),
                    pl.BlockSpec((tm, tn), lambda i, j: (i, oj(j))),
                    pl.BlockSpec((None, None, 1, tn), lambda i, j: (grp(i), k_gate, 0, oj(j)))],
        out_specs=pl.BlockSpec((tm, tn), lambda i, j: (i, oj(j))),
        out_shape=jax.ShapeDtypeStruct((rows, d), F32),
        scratch_shapes=[pltpu.VMEM((nj, tm, tn), BF16)],
        compiler_params=_cp("arbitrary", "arbitrary"),
        name="merge_out",
    )(*hs, gates, gates, gates, gates, br_w, br_w, br_w, br_w, out_w, h, modp)


def _tile_place(geom):
    n_lat_tiles, lat_tiles, ctx_tiles = geom
    i = pl.program_id(0)
    is_ctx = i >= n_lat_tiles
    per_seq = jnp.where(is_ctx, ctx_tiles, lat_tiles)
    pos = lax.rem(jnp.where(is_ctx, i - n_lat_tiles, i), per_seq)
    return is_ctx, pos == 0, pos == per_seq - 1


def _row_neighbours(xp_ref, x, xn_ref, first, last):
    ts = x.shape[0]
    hp = xp_ref.shape[0]
    t = lax.broadcasted_iota(jnp.int32, x.shape, 0)
    prv = jnp.where(t == 0, jnp.where(first, 0.0, xp_ref[hp - 1:hp, :]), pltpu.roll(x, 1, 0))
    nxt = jnp.where(t == ts - 1, jnp.where(last, 0.0, xn_ref[0:1, :]), pltpu.roll(x, ts - 1, 0))
    return prv, nxt


def _token_shift(xp_ref, x_ref, xn_ref, mu_ref, geom):
    is_ctx, first, last = _tile_place(geom)
    x = x_ref[...]
    ts = x.shape[0]
    t = lax.broadcasted_iota(jnp.int32, x.shape, 0)
    lane = lax.broadcasted_iota(jnp.int32, x.shape, 1)
    prv, nxt = _row_neighbours(xp_ref, x, xn_ref, first, last)
    sh_ctx = jnp.where((lane & 1) == 0, prv, nxt)
    tw = t & (GRID_W - 1)
    left = jnp.where(tw == 0, 0.0, prv)
    right = jnp.where(tw == GRID_W - 1, 0.0, nxt)
    up = jnp.concatenate([jnp.where(first, 0.0, xp_ref[...]), x[0:ts - GRID_W]], axis=0)
    down = jnp.concatenate([x[GRID_W:ts], jnp.where(last, 0.0, xn_ref[...])], axis=0)
    c4 = lane & 3
    sh_lat = jnp.where(c4 == 0, left, jnp.where(c4 == 1, right, jnp.where(c4 == 2, up, down)))
    sh = jnp.where(is_ctx, sh_ctx, sh_lat)
    return x + (sh - x) * mu_ref[...]


def _conv_kernel(xp_ref, x_ref, xn_ref, w_ref, b_ref, s_ref, o_ref, *, geom):
    _, first, last = _tile_place(geom)
    x = x_ref[...]
    prv, nxt = _row_neighbours(xp_ref, x, xn_ref, first, last)
    w = w_ref[...]
    y = prv * w[0:1] + x * w[1:2] + nxt * w[2:3] + b_ref[...]
    o_ref[...] = y * jax.nn.sigmoid(y) * s_ref[...]


def _sequence_tiles(body, p, col0, width, halo, small, out_widths, nb, t_lat, t_ctx, name):
    rows = p.shape[0]
    ts = min(256, t_ctx)
    assert t_lat % ts == 0 and t_ctx % ts == 0 and ts >= 2 * GRID_W and ts % halo == 0 and col0 % width == 0
    nt = rows // ts
    nh = rows // halo
    r = ts // halo
    geom = (nb * t_lat // ts, t_lat // ts, t_ctx // ts)
    cb = col0 // width
    full = lambda a: pl.BlockSpec(a.shape, lambda i: (0,) * a.ndim)
    return pl.pallas_call(
        functools.partial(body, geom=geom),
        grid=(nt,),
        in_specs=[pl.BlockSpec((halo, width), lambda i: (jnp.maximum(i * r - 1, 0), cb)),
                  pl.BlockSpec((ts, width), lambda i: (i, cb)),
                  pl.BlockSpec((halo, width), lambda i: (jnp.minimum((i + 1) * r, nh - 1), cb))]
                 + [full(a) for a in small],
        out_specs=[pl.BlockSpec((ts, ow), lambda i: (i, 0)) for ow in out_widths],
        out_shape=[jax.ShapeDtypeStruct((rows, ow), F32) for ow in out_widths],
        compiler_params=_cp("arbitrary"),
        name=name,
    )(p, p, p, *small)


def _rwkv_pre_kernel(xp_ref, x_ref, xn_ref, mu_ref, kk_ref, ka_ref, rk_ref, w0_ref, a0_ref, wup_ref, aup_ref,
                     gup_ref, ones_ref, r_o, v_o, kk_o, lwf_o, lwb_o, ktf_o, ktb_o, bf_o, bb_o, bonus_o, g_o, *,
                     geom):
    wb = r_o.shape[1]
    za = _token_shift(xp_ref, x_ref, xn_ref, mu_ref, geom)
    r = za[:, 0:wb]
    k = za[:, wb:2 * wb]
    v = za[:, 2 * wb:3 * wb]
    wd = jnp.tanh(za[:, 3 * wb:3 * wb + 128])
    ad = za[:, 3 * wb + 128:3 * wb + 256]
    gd = jax.nn.sigmoid(za[:, 3 * wb + 256:3 * wb + 384])
    ones = ones_ref[...]
    kq = k * kk_ref[...]
    kk = kq * lax.rsqrt(jnp.maximum(_seg_sum(kq * kq, ones), 1e-24))
    r_o[...] = r
    v_o[...] = v
    kk_o[...] = kk
    ka = ka_ref[...]
    for d, (lw_o, kt_o, b_o) in enumerate(((lwf_o, ktf_o, bf_o), (lwb_o, ktb_o, bb_o))):
        xw = w0_ref[d:d + 1] + _mm(wd, wup_ref[d])
        lw_o[...] = -jax.nn.sigmoid(xw) * 0.6065306597126334
        a = jax.nn.sigmoid(a0_ref[d:d + 1] + _mm(ad, aup_ref[d]))
        kt_o[...] = k * (1.0 + (a - 1.0) * ka)
        b_o[...] = kk * a
    bonus_o[...] = _seg_sum(r * k * rk_ref[...], ones) * v
    g_o[...] = _mm(gd, gup_ref[...])


def _scan_geometry(nb, t_lat, t_ctx, chunk):
    nlc = t_lat // chunk
    ncc = t_ctx // chunk
    nch = nlc + ncc

    def fwd(j):
        return jnp.where(j < ncc, nlc + j, j - ncc)

    def rev(j):
        return nch - 1 - j

    def row(b, c):
        return jnp.where(c < nlc, b * nlc + c, nb * nlc + b * ncc + (c - nlc))

    return fwd, rev, row, nch


def _scan_in_specs(nb, row, cm, chunk, width, colblk):
    return [pl.BlockSpec((chunk, width), functools.partial(lambda j, b: (row(b, cm(j)), colblk), b=b))
            for b in range(nb)]


def _scan_out_spec(nb, cm, chunk, width):
    return pl.BlockSpec((None, nb, chunk, width), lambda j: (cm(j), 0, 0, 0))


def _rwkv_scan_kernel(*refs, nb):
    ins = refs[:12 * nb]
    yf, yb, s_ref = refs[12 * nb:]

    @pl.when(pl.program_id(0) == 0)
    def _():
        s_ref[...] = jnp.zeros_like(s_ref)

    n, wbw = ins[0].shape
    nh = wbw // A_HD
    heads = []
    for d, y_r in enumerate((yf, yb)):
        rev = d == 1
        arr = lambda a, b: ins[(d * 6 + a) * nb + b][...]
        lws = [arr(3, b) for b in range(nb)]
        c_all = _mm(_tri(n, rev, False).astype(F32), jnp.concatenate(lws, axis=1), HIGHEST)
        strict = _tri(n, rev, True)
        incl = _tri(n, rev, False)
        for b in range(nb):
            lw = lws[b]
            c = c_all[:, b * wbw:(b + 1) * wbw]
            eg = jnp.exp(c)
            ieg = jnp.exp(-c)
            r_all = arr(0, b) * eg
            kk_all = arr(2, b) * jnp.exp(c - lw)
            kt_all = arr(4, b) * ieg
            b_all = arr(5, b) * ieg
            v_all = arr(1, b)
            g_last = eg[0:1] if rev else eg[n - 1:n]
            for h in range(nh):
                sl = slice(h * A_HD, (h + 1) * A_HD)
                idx = (d * nb + b) * nh + h
                heads.append(dict(r=r_all[:, sl], kk=kk_all[:, sl], b=b_all[:, sl], kt=kt_all[:, sl],
                                  v=v_all[:, sl], s=s_ref[idx], g=g_last[:, sl], strict=strict, incl=incl,
                                  y_ref=y_r.at[b], sl=sl, idx=idx))
    for t in heads:
        t["z"] = _nt(jnp.concatenate([t["kk"], t["r"]], axis=0), jnp.concatenate([t["kt"], t["b"], t["s"]], axis=0))
    for t in heads:
        z = t["z"]
        t["a_kv"] = jnp.where(t["strict"], z[0:n, 0:n], 0.0)
        t["a_kb"] = jnp.where(t["strict"], z[0:n, n:2 * n], 0.0)
        t["r_kv"] = jnp.where(t["incl"], z[n:2 * n, 0:n], 0.0)
        t["r_kb"] = jnp.where(t["incl"], z[n:2 * n, n:2 * n], 0.0)
    for t in heads:
        t["av"] = _mm(jnp.concatenate([t["a_kv"], t["r_kv"]], axis=0), t["v"])
    left = lax.broadcasted_iota(jnp.int32, (n, 2 * n), 1) < n
    for t in heads:
        t["w"] = jnp.concatenate([t["a_kb"], t["z"][0:n, 2 * n:] + t["av"][0:n]], axis=1)
    for t in heads:
        r = _mm(t["a_kb"], t["w"])
        t["w"] = jnp.where(left, r, t["w"] - r)
    m = 2
    while m < n:
        for t in heads:
            r = _mm(t["w"][:, 0:n], t["w"])
            t["w"] = jnp.where(left, r, t["w"] + r)
        m *= 2
    for t in heads:
        t["u"] = t["w"][:, n:2 * n]
        y = t["z"][n:2 * n, 2 * n:] + t["av"][n:2 * n] - _mm(t["r_kb"], t["u"])
        t["y_ref"][:, t["sl"]] = y
        s_new = t["s"] + _tn(jnp.concatenate([t["v"], t["u"]], axis=0), jnp.concatenate([t["kt"], -t["b"]], axis=0))
        s_ref[t["idx"]] = s_new * t["g"]


def _rwkv_scan(r, v, kk, lwf, lwb, ktf, ktb, bf, bb, nb, t_lat, t_ctx):
    wb = r.shape[1]
    fwd, rev, row, nch = _scan_geometry(nb, t_lat, t_ctx, CHUNK)
    in_specs = [s for cm in (fwd, rev) for _ in range(6) for s in _scan_in_specs(nb, row, cm, CHUNK, wb, 0)]
    args = [a for grp in ((r, v, kk, lwf, ktf, bf), (r, v, kk, lwb, ktb, bb)) for a in grp for _ in range(nb)]
    return pl.pallas_call(
        functools.partial(_rwkv_scan_kernel, nb=nb),
        grid=(nch,),
        in_specs=in_specs,
        out_specs=[_scan_out_spec(nb, fwd, CHUNK, wb), _scan_out_spec(nb, rev, CHUNK, wb)],
        out_shape=[jax.ShapeDtypeStruct((nch, nb, CHUNK, wb), F32)] * 2,
        scratch_shapes=[pltpu.VMEM((2 * nb * (wb // A_HD), A_HD, A_HD), F32)],
        compiler_params=_cp("arbitrary"),
        name="rwkv_scan",
    )(*args)


def _mlstm_scan_kernel(*refs, nb):
    ins = refs[:8 * nb]
    gbias_ref, hf, hb, c_ref, n_ref, m_ref = refs[8 * nb:]

    @pl.when(pl.program_id(0) == 0)
    def _():
        c_ref[...] = jnp.zeros_like(c_ref)
        n_ref[...] = jnp.zeros_like(n_ref)
        m_ref[...] = jnp.zeros_like(m_ref)

    n = ins[0].shape[0]
    nh = ins[0].shape[1] // C_HD
    heads = []
    for d, h_r in enumerate((hf, hb)):
        rev = d == 1
        arr = lambda a, b: ins[(d * 4 + a) * nb + b][...]
        gates = [arr(3, b) + gbias_ref[...] for b in range(nb)]
        fgs = [_log_sigmoid(g) for g in gates]
        mask = _tri(n, rev, False)
        mi = mask.astype(F32)
        bcol_all = _mm(mi, jnp.concatenate(fgs, axis=1), HIGHEST)
        brow_all = _nt(jnp.concatenate([f.T for f in fgs], axis=0), mi, HIGHEST)
        for b in range(nb):
            bcol = bcol_all[:, b * 128:(b + 1) * 128]
            brow = brow_all[b * 128:(b + 1) * 128]
            gates_t = gates[b].T
            q_all, k_all, v_all = arr(0, b), arr(1, b), arr(2, b)
            for h in range(nh):
                sl = slice(h * C_HD, (h + 1) * C_HD)
                ii = d * 2 * nh + h
                fi = ii + nh
                idx = (d * nb + b) * nh + h
                b_c = bcol[:, fi:fi + 1]
                heads.append(dict(q=q_all[:, sl], k=k_all[:, sl], v=v_all[:, sl], b_c=b_c, b_r=brow[fi:fi + 1, :],
                                  i_c=gates[b][:, ii:ii + 1], i_r=gates_t[ii:ii + 1, :], m=m_ref[idx][:, 0:1],
                                  cm=c_ref[idx], nn=n_ref[idx], mask=mask,
                                  b_l=b_c[0:1] if rev else b_c[n - 1:n], idx=idx, h_ref=h_r.at[b], sl=sl))
    for t in heads:
        t["qk"] = _nt(t["q"], t["k"])
        t["qc"] = _nt(t["q"], t["cm"])
    for t in heads:
        t["dlog"] = jnp.where(t["mask"], t["b_c"] - t["b_r"] + t["i_r"], -jnp.inf)
        t["gl"] = t["b_l"] - t["b_c"] + t["i_c"]
    for t in heads:
        t["dmax"] = jnp.max(t["dlog"], axis=1, keepdims=True)
        t["gmax"] = jnp.max(t["gl"], axis=0, keepdims=True)
    for t in heads:
        inter = t["b_c"] + t["m"]
        m_t = jnp.maximum(inter, t["dmax"])
        t["iw"] = jnp.exp(inter - m_t)
        t["m_t"] = m_t
        t["s"] = t["qk"] * jnp.exp(t["dlog"] - m_t)
        m_new = jnp.maximum(t["b_l"] + t["m"], t["gmax"])
        t["sw"] = jnp.exp(t["gl"] - m_new)
        t["dec"] = jnp.exp(t["b_l"] + t["m"] - m_new)
        t["m_new"] = m_new
    for t in heads:
        t["sv"] = _mm(t["s"], t["v"])
        t["vk"] = _tn(t["v"] * t["sw"], t["k"])
    for t in heads:
        t["rs"] = jnp.sum(t["s"], axis=1, keepdims=True)
        t["qn"] = jnp.sum(t["q"] * t["nn"], axis=1, keepdims=True)
        t["ks"] = jnp.sum(t["sw"] * t["k"], axis=0, keepdims=True)
    for t in heads:
        iw = t["iw"]
        num = t["sv"] + iw * t["qc"]
        den = jnp.maximum(jnp.abs(t["rs"] + iw * t["qn"]), jnp.exp(-t["m_t"]))
        t["h_ref"][:, t["sl"]] = num / den
        idx = t["idx"]
        c_ref[idx] = t["dec"] * t["cm"] + t["vk"]
        n_ref[idx] = t["dec"] * t["nn"] + t["ks"]
        m_ref[idx] = jnp.broadcast_to(t["m_new"], m_ref.shape[1:])


def _mlstm_scan(qk, p, gbias, col_v, col_g, nb, t_lat, t_ctx, wb):
    ck = C_CHUNK
    fwd, rev, row, nch = _scan_geometry(nb, t_lat, t_ctx, ck)
    nh = wb // C_HD

    def specs(cm):
        return (_scan_in_specs(nb, row, cm, ck, wb, 0) + _scan_in_specs(nb, row, cm, ck, wb, 1)
                + _scan_in_specs(nb, row, cm, ck, wb, col_v // wb) + _scan_in_specs(nb, row, cm, ck, 128, col_g // 128))

    args = ([qk] * nb + [qk] * nb + [p] * nb + [p] * nb) * 2
    return pl.pallas_call(
        functools.partial(_mlstm_scan_kernel, nb=nb),
        grid=(nch,),
        in_specs=specs(fwd) + specs(rev) + [pl.BlockSpec((1, 128), lambda j: (0, 0))],
        out_specs=[_scan_out_spec(nb, fwd, ck, wb), _scan_out_spec(nb, rev, ck, wb)],
        out_shape=[jax.ShapeDtypeStruct((nch, nb, ck, wb), F32)] * 2,
        scratch_shapes=[pltpu.VMEM((2 * nb * nh, C_HD, C_HD), F32), pltpu.VMEM((2 * nb * nh, 1, C_HD), F32),
                        pltpu.VMEM((2 * nb * nh, 1, 128), F32)],
        compiler_params=_cp("arbitrary"),
        name="mlstm_scan",
    )(*args, gbias)


def _gla_scan_kernel(*refs, nb):
    ins = refs[:8 * nb]
    aup_ref, abias_ref, of, ob, s_ref = refs[8 * nb:]

    @pl.when(pl.program_id(0) == 0)
    def _():
        s_ref[...] = jnp.zeros_like(s_ref)

    n, wkw = ins[0].shape
    nh = wkw // D_DK
    nsub = n // D_SUB
    pw = 2 * D_DK
    row = lax.broadcasted_iota(jnp.int32, (n, pw), 0)
    colx = lax.broadcasted_iota(jnp.int32, (n, pw), 1) & (D_DK - 1)
    rel = colx - (row & -D_SUB)
    rin = row & (D_SUB - 1)
    same_head = ((lax.broadcasted_iota(jnp.int32, (pw, pw), 0) & D_DK)
                 == (lax.broadcasted_iota(jnp.int32, (pw, pw), 1) & D_DK))
    ones_pair = jnp.where(same_head, 1.0, 0.0).astype(BF16)
    heads, pairs = [], []
    for d, o_r in enumerate((of, ob)):
        rev = d == 1
        arr = lambda a, b: ins[(d * 4 + a) * nb + b][...]
        la_rows = _log_sigmoid(_mm(jnp.concatenate([arr(3, b) for b in range(nb)], axis=0), aup_ref[d])
                               + abias_ref[d]) * (1.0 / D_TAU)
        la_all = jnp.concatenate([la_rows[b * n:(b + 1) * n] for b in range(nb)], axis=1)
        bc_all = _mm(_tri(n, rev, False).astype(F32), la_all, HIGHEST)
        for b in range(nb):
            bc = bc_all[:, b * wkw:(b + 1) * wkw]
            q_all = arr(0, b) * (D_DK ** -0.5)
            k_all = arr(1, b)
            v_all = arr(2, b)
            base = (d * nb + b) * nh
            for h in range(nh):
                ksl = slice(h * D_DK, (h + 1) * D_DK)
                vsl = slice(h * D_DV, (h + 1) * D_DV)
                heads.append(dict(q=q_all[:, ksl], k=k_all[:, ksl], v=v_all[:, vsl], bc=bc[:, ksl], rev=rev,
                                  s=s_ref[base + h], idx=base + h, o_ref=o_r.at[b], vsl=vsl))
            for p in range(nh // 2):
                psl = slice(p * pw, (p + 1) * pw)
                pairs.append(dict(q=q_all[:, psl], k=k_all[:, psl], bc=bc[:, psl], rev=rev,
                                  heads=(base + 2 * p, base + 2 * p + 1)))
    for t in heads:
        t["o"] = _nt(t["q"] * jnp.exp(t["bc"]), t["s"])
    for t in heads:
        q, k, bc, rev = t["q"], t["k"], t["bc"], t["rev"]
        pieces = []
        for blk in range(nsub):
            r0 = blk * D_SUB
            edge = bc[r0 + D_SUB - 1:r0 + D_SUB] if rev else bc[r0:r0 + 1]
            has_other = blk < nsub - 1 if rev else blk > 0
            if has_other:
                qs = q[r0:r0 + D_SUB] * jnp.exp(bc[r0:r0 + D_SUB] - edge)
                ks = k * jnp.exp(jnp.minimum(edge - bc, 0.0))
                pieces.append(_nt(qs, ks))
            else:
                pieces.append(jnp.zeros((D_SUB, n), F32))
        t["sc"] = jnp.concatenate(pieces, axis=0)
    for t in pairs:
        q, k, bc = t["q"], t["k"], t["bc"]
        es = []
        for j in range(D_SUB):
            kj = jnp.concatenate([jnp.broadcast_to(k[b * D_SUB + j:b * D_SUB + j + 1], (D_SUB, pw))
                                  for b in range(nsub)], axis=0)
            bj = jnp.concatenate([jnp.broadcast_to(bc[b * D_SUB + j:b * D_SUB + j + 1], (D_SUB, pw))
                                  for b in range(nsub)], axis=0)
            es.append(q * kj * jnp.exp(bc - bj))
        t["e"] = jnp.concatenate(es, axis=0)
    for t in pairs:
        t["c"] = _seg_sum(t["e"], ones_pair)
    for t in pairs:
        rev = t["rev"]
        sc = jnp.concatenate([heads[t["heads"][0]]["sc"], heads[t["heads"][1]]["sc"]], axis=1)
        a = jnp.where((rel >= D_SUB) if rev else (rel < 0), sc, 0.0)
        for j in range(D_SUB):
            keep = (rel == j) & ((rin <= j) if rev else (rin >= j))
            a = jnp.where(keep, t["c"][j * n:(j + 1) * n], a)
        heads[t["heads"][0]]["a"] = a[:, 0:D_DK]
        heads[t["heads"][1]]["a"] = a[:, D_DK:pw]
    for t in heads:
        t["o_ref"][:, t["vsl"]] = t["o"] + _mm(t["a"], t["v"])
        bc = t["bc"]
        b_l = bc[0:1] if t["rev"] else bc[n - 1:n]
        s_ref[t["idx"]] = t["s"] * jnp.exp(b_l) + _tn(t["v"], t["k"] * jnp.exp(b_l - bc))


def _gla_scan(p, aupp, abias, col_q, col_k, col_v, col_a, nb, t_lat, t_ctx, wb):
    fwd, rev, row, nch = _scan_geometry(nb, t_lat, t_ctx, CHUNK)
    wk = aupp.shape[-1]
    nh = wk // D_DK

    def specs(cm):
        return (_scan_in_specs(nb, row, cm, CHUNK, wk, col_q // wk) + _scan_in_specs(nb, row, cm, CHUNK, wk, col_k // wk)
                + _scan_in_specs(nb, row, cm, CHUNK, wb, col_v // wb)
                + _scan_in_specs(nb, row, cm, CHUNK, 128, col_a // 128))

    return pl.pallas_call(
        functools.partial(_gla_scan_kernel, nb=nb),
        grid=(nch,),
        in_specs=specs(fwd) + specs(rev) + [pl.BlockSpec(aupp.shape, lambda j: (0, 0, 0)),
                                           pl.BlockSpec(abias.shape, lambda j: (0, 0, 0))],
        out_specs=[_scan_out_spec(nb, fwd, CHUNK, wb), _scan_out_spec(nb, rev, CHUNK, wb)],
        out_shape=[jax.ShapeDtypeStruct((nch, nb, CHUNK, wb), F32)] * 2,
        scratch_shapes=[pltpu.VMEM((2 * nb * nh, D_DV, D_DK), F32)],
        compiler_params=_cp("arbitrary"),
        name="gla_scan",
    )(*([p] * (8 * nb)), aupp, abias)


def _branch_out_kernel(ya_f, ya_b, bonus, ga, pb, hc_f, hc_b, oc, od_f, od_b, gdd,
                       a_lng, a_lnb, ones, b_ws, b_bias, b_lng, b_lnb, c_lng, d_lng,
                       hs_a, hs_b, hs_c, hs_d):
    wb = hs_a.shape[1]
    tm = hs_a.shape[0]
    y = ya_f[...].reshape(tm, wb) + ya_b[...].reshape(tm, wb)
    on = ones[...]
    mu = _seg_sum(y, on) * (1.0 / A_HD)
    yc = y - mu
    var = _seg_sum(yc * yc, on) * (1.0 / A_HD)
    yn = yc * lax.rsqrt(var + A_LN_EPS) * a_lng[...] + a_lnb[...]
    hs_a[...] = ((yn + bonus[...]) * ga[...]).astype(hs_a.dtype)
    z = jax.nn.gelu(pb[...])
    u = z[:, 0:wb]
    vv = z[:, wb:2 * wb]
    mu = jnp.mean(vv, axis=-1, keepdims=True)
    vc = vv - mu
    vn = (vc * lax.rsqrt(jnp.mean(vc * vc, axis=-1, keepdims=True) + 1e-5) * b_lng[...] + b_lnb[...])
    vn = vn.astype(BF16)
    for ck in range(u.shape[0] // B_CHUNK):
        rs = slice(ck * B_CHUNK, (ck + 1) * B_CHUNK)
        for g in range(wb // B_CHUNK):
            cs = slice(g * B_CHUNK, (g + 1) * B_CHUNK)
            s = _mm(b_ws[g].astype(BF16), vn[rs, cs]) + b_bias[:, cs]
            hs_b[rs, cs] = (u[rs, cs] * s).astype(hs_b.dtype)
    hc = hc_f[...].reshape(tm, wb) + hc_b[...].reshape(tm, wb)
    ogate = jax.nn.sigmoid(oc[...])
    gc = c_lng[...]
    for h in range(wb // C_HD):
        sl = slice(h * C_HD, (h + 1) * C_HD)
        x = hc[:, sl]
        xc = x - jnp.mean(x, axis=-1, keepdims=True)
        xn = xc * lax.rsqrt(jnp.mean(xc * xc, axis=-1, keepdims=True) + C_LN_EPS) * gc[:, sl]
        hs_c[:, sl] = (xn * ogate[:, sl]).astype(hs_c.dtype)
    od = od_f[...].reshape(tm, wb) + od_b[...].reshape(tm, wb)
    gg = gdd[...]
    gg = gg * jax.nn.sigmoid(gg)
    gd_ = d_lng[...]
    for h in range(wb // D_DV):
        sl = slice(h * D_DV, (h + 1) * D_DV)
        x = od[:, sl]
        xn = x * lax.rsqrt(jnp.mean(x * x, axis=-1, keepdims=True) + D_LN_EPS) * gd_[:, sl]
        hs_d[:, sl] = (xn * gg[:, sl]).astype(hs_d.dtype)


def _branch_out(ya_f, ya_b, bonus, ga, p, hc_f, hc_b, od_f, od_b, prm, cols, rows, nb, t_lat, t_ctx):
    wb = bonus.shape[1]
    tm = min(256, t_ctx)
    n_lat_tiles, lat_tiles, ctx_tiles = nb * t_lat // tm, t_lat // tm, t_ctx // tm
    row = lambda w, cb: pl.BlockSpec((tm, w), lambda i: (i, cb))

    def scan_out(a):
        per_tile = tm // a.shape[2]

        def imap(i):
            ic = i - n_lat_tiles
            seq = jnp.where(i < n_lat_tiles, i // lat_tiles, ic // ctx_tiles)
            blk = jnp.where(i < n_lat_tiles, lax.rem(i, lat_tiles), lat_tiles + lax.rem(ic, ctx_tiles))
            return blk, seq, 0, 0

        return pl.BlockSpec((per_tile, None, a.shape[2], wb), imap)

    small = [prm["a_lng"], prm["a_lnb"], prm["ones"], prm["b_ws"], prm["b_bias"], prm["b_lng"], prm["b_lnb"],
             prm["c_lng"], prm["d_lng"]]
    full = lambda a: pl.BlockSpec(a.shape, lambda i: (0,) * a.ndim)
    return pl.pallas_call(
        _branch_out_kernel,
        grid=(rows // tm,),
        in_specs=[scan_out(ya_f), scan_out(ya_b), row(wb, 0), row(wb, 0), row(2 * wb, cols["b"] // (2 * wb)),
                  scan_out(hc_f), scan_out(hc_b), row(wb, cols["c_o"] // wb), scan_out(od_f), scan_out(od_b),
                  row(wb, cols["d_g"] // wb)] + [full(a) for a in small],
        out_specs=[row(wb, 0)] * 4,
        out_shape=[jax.ShapeDtypeStruct((rows, wb), BF16)] * 4,
        compiler_params=_cp("arbitrary"),
        name="branch_out",
    )(ya_f, ya_b, bonus, ga, p, hc_f, hc_b, p, od_f, od_b, p, *small)


def _pick_tile(*sizes):
    for t in (1024, 512, 256, 128):
        if all(s % t == 0 for s in sizes):
            return t
    raise ValueError("token counts must be multiples of 128")


def _free_tile(rows):
    for t in (1536, 1024, 512, 256, 128):
        if rows % t == 0:
            return t
    raise ValueError("token counts must be multiples of 128")


def _pick_cols(n, cands):
    for t in cands:
        if n % t == 0:
            return t
    raise ValueError(f"no column tile for {n}")


def _proj_cols(d_model):
    wb = d_model // 4
    col = {"a": 0, "b": d_model, "c_qk": d_model + 2 * wb}
    col["c_v"] = col["c_qk"] + 2 * wb
    col["c_o"] = col["c_v"] + wb
    col["d_q"] = col["c_o"] + wb
    col["d_k"] = col["d_q"] + wb // 2
    col["d_v"] = col["d_k"] + wb // 2
    col["d_g"] = col["d_v"] + wb
    col["c_gate"] = col["d_g"] + wb
    col["d_a"] = col["c_gate"] + 128
    assert 3 * wb + 384 <= d_model
    return col


def _permute_proj(a, d_model):
    wb = d_model // 4
    a_cols = 3 * wb + 384
    o_b = a_cols
    o_c = o_b + 2 * wb
    o_d = o_c + 4 * wb + 16
    o_g = o_d + 3 * wb + 32
    z = lambda k: jnp.zeros(a.shape[:-1] + (k,), a.dtype)
    mix = jnp.concatenate([
        a[..., 0:a_cols], z(d_model - a_cols), a[..., o_b:o_b + 2 * wb], a[..., o_c:o_c + 4 * wb],
        a[..., o_d:o_d + 3 * wb], a[..., o_c + 4 * wb:o_c + 4 * wb + 16], z(112),
        a[..., o_d + 3 * wb:o_d + 3 * wb + 32], z(96)], axis=-1)
    return mix, a[..., o_g:o_g + 4 * d_model]


def _mixers(p, l, nb, t_lat, t_ctx, rows, w):
    wb = w["a_kk"].shape[1]
    col = _proj_cols(4 * wb)
    a_cols = 3 * wb + 384
    row = lambda a: a.reshape(1, -1)
    ones_bd = jnp.kron(jnp.eye(wb // A_HD, dtype=F32), jnp.ones((A_HD, A_HD), F32)).astype(BF16)

    a_wup, a_aup = w["a_wup"], w["a_aup"]
    wupp = jnp.zeros((2, 128, wb), F32).at[0, 0:64].set(a_wup[l, 0]).at[1, 64:128].set(a_wup[l, 1])
    aupp = jnp.zeros((2, 128, wb), F32).at[0, 0:64].set(a_aup[l, 0]).at[1, 64:128].set(a_aup[l, 1])
    small_a = [row(w["a_mu"][l]), row(w["a_kk"][l]), row(w["a_ka"][l]), row(w["a_rk"][l]), w["a_w0"][l],
               w["a_a0"][l], wupp, aupp, w["a_gup"][l], ones_bd]
    r_, v_, kk_, lwf, lwb, ktf, ktb, bf_, bb_, bonus, ga = _sequence_tiles(
        _rwkv_pre_kernel, p, col["a"], a_cols, GRID_W, small_a, [wb] * 11, nb, t_lat, t_ctx, "rwkv_pre")
    ya_f, ya_b = _rwkv_scan(r_, v_, kk_, lwf, lwb, ktf, ktb, bf_, bb_, nb, t_lat, t_ctx)

    kscale = jnp.concatenate([jnp.ones((wb,), F32), jnp.full((wb,), C_HD ** -0.5, F32)]).reshape(1, -1)
    qk, = _sequence_tiles(_conv_kernel, p, col["c_qk"], 2 * wb, 8,
                          [w["c_conv_w"][l], row(w["c_conv_b"][l]), kscale], [2 * wb], nb, t_lat, t_ctx,
                          "mlstm_conv")
    gbias = jnp.zeros((1, 128), F32).at[0, 0:16].set(w["c_gate_b"][l].reshape(-1))
    hc_f, hc_b = _mlstm_scan(qk, p, gbias, col["c_v"], col["c_gate"], nb, t_lat, t_ctx, wb)

    d_aup = w["d_aup"]
    rk_d = d_aup.shape[2]
    aupp_d = (jnp.zeros((2, 128, wb // 2), F32).at[0, 0:rk_d].set(d_aup[l, 0])
              .at[1, rk_d:2 * rk_d].set(d_aup[l, 1]))
    od_f, od_b = _gla_scan(p, aupp_d, w["d_ab"][l].reshape(2, 1, -1), col["d_q"], col["d_k"], col["d_v"],
                           col["d_a"], nb, t_lat, t_ctx, wb)

    b_bias = jnp.repeat(w["b_bs"][l].T, B_CHUNK, axis=1)
    prm_o = {"a_lng": row(w["a_ln_g"][l]), "a_lnb": row(w["a_ln_b"][l]), "ones": ones_bd, "b_ws": w["b_ws"][l],
             "b_bias": b_bias, "b_lng": row(w["b_ln_g"][l]), "b_lnb": row(w["b_ln_b"][l]),
             "c_lng": row(w["c_ln_g"][l]), "d_lng": row(w["d_ln_g"][l])}
    return _branch_out(ya_f, ya_b, bonus, ga, p, hc_f, hc_b, od_f, od_b, prm_o, col, rows, nb, t_lat, t_ctx)


def kernel(x, c, ctx, c_ctx, ada_w, ada_b, norm_g, ffn_w1, ffn_w3, ffn_w2, in_w, in_b, a_mu, a_w0, a_wup, a_a0, a_aup, a_gup, a_kk, a_ka, a_rk, a_ln_g, a_ln_b, b_ws, b_bs, b_ln_g, b_ln_b, c_conv_w, c_conv_b, c_gate_b, c_ln_g, d_aup, d_ab, d_ln_g, br_w, out_w, final_g):
    weights = dict(a_mu=a_mu, a_w0=a_w0, a_wup=a_wup, a_a0=a_a0, a_aup=a_aup, a_gup=a_gup, a_kk=a_kk, a_ka=a_ka,
                   a_rk=a_rk.reshape(a_rk.shape[0], -1), a_ln_g=a_ln_g, a_ln_b=a_ln_b, b_ws=b_ws, b_bs=b_bs,
                   b_ln_g=b_ln_g, b_ln_b=b_ln_b, c_conv_w=c_conv_w, c_conv_b=c_conv_b, c_gate_b=c_gate_b,
                   c_ln_g=c_ln_g, d_aup=d_aup, d_ab=d_ab, d_ln_g=d_ln_g)
    nb, t_lat, d_model = x.shape
    t_ctx = ctx.shape[1]
    depth = ada_w.shape[0]
    wb = d_model // 4
    d_ff = ffn_w1.shape[-1]
    n_lat = nb * t_lat
    n_ctx = nb * t_ctx
    rows_all = n_lat + n_ctx
    assert t_lat % (GRID_W * 2) == 0 and t_ctx % B_CHUNK == 0 and n_lat % t_ctx == 0
    assert wb == 512 and a_wup.shape[2] == 64 and a_aup.shape[2] == 64 and a_gup.shape[1] == 128

    tm = _pick_tile(t_lat, n_ctx)

    def grp(i):
        return jnp.where(i < n_lat // tm, 1 + i // (t_lat // tm), 0)

    tm_s = _pick_tile(t_lat, n_ctx, 512)
    tf = _pick_cols(d_ff, (512, 256, 128))

    h = jnp.concatenate([x.reshape(n_lat, d_model), ctx.reshape(n_ctx, d_model)], axis=0)

    m_pad = -(-(nb + 1) // 8) * 8
    cpad = jnp.zeros((m_pad, d_model), F32).at[0].set(c_ctx).at[1:nb + 1].set(c)
    mod_all = _ada_mod(cpad, ada_w, ada_b, 128)
    mod_all = mod_all.reshape(depth, m_pad, N_MOD, 1, d_model)

    for l in range(depth):
        modp = mod_all[l]
        last = l == depth - 1

        hn = _norm_mod(h, norm_g[l, 0], modp, 0, rows_all, tm_s, lambda i: jnp.where(
            i < n_lat // tm_s, 1 + i // (t_lat // tm_s), 0))
        gact = _ffn_up(hn, ffn_w1, ffn_w3, l, 0, _free_tile(rows_all), tf)
        h = _ffn_down(gact, ffn_w2[l, 0].astype(BF16), h, modp, 2, tm, 512, grp)

        hn = _norm_mod(h, norm_g[l, 1], modp, 3, rows_all, tm_s, lambda i: jnp.where(
            i < n_lat // tm_s, 1 + i // (t_lat // tm_s), 0))
        w_mix, w_gate = _permute_proj(in_w[l], d_model)
        b_mix, b_gate = _permute_proj(in_b[l], d_model)
        rows = n_lat if last else rows_all
        p = _in_proj(hn, w_mix.astype(BF16), b_mix.reshape(1, -1), rows_all, _free_tile(rows_all), 768)
        gates = _in_proj(hn, w_gate.astype(BF16), b_gate.reshape(1, -1), rows, _free_tile(rows), 1024, gates=True)

        hs = _mixers(p, l, nb, t_lat, t_ctx, rows, weights)
        h = _merge_out(hs, gates, br_w.astype(BF16), out_w.astype(BF16), h, modp, 5, l, rows, tm, 512, grp)

        hn = _norm_mod(h, norm_g[l, 2], modp, 6, rows, tm_s, lambda i: jnp.where(
            i < n_lat // tm_s, 1 + i // (t_lat // tm_s), 0))
        gact = _ffn_up(hn, ffn_w1, ffn_w3, l, 1, _free_tile(rows), tf)
        h = _ffn_down(gact, ffn_w2[l, 1].astype(BF16), h, modp, 8, tm, 512, grp)

    out = _final_norm(h, final_g, tm_s)
    return out.reshape(nb, t_lat, d_model)
```

```python
import functools

import jax
import jax.numpy as jnp
from jax import lax
from jax.experimental import pallas as pl
from jax.experimental.pallas import tpu as pltpu

F32 = jnp.float32
BF16 = jnp.bfloat16
HIGHEST = lax.Precision.HIGHEST

EPS = 1e-6
GRID_W = 64
N_MOD = 9
CHUNK = 64
A_HD = 64
A_LN_EPS = 64e-5
B_CHUNK = 128
C_HD = 128
C_CHUNK = 128
C_LN_EPS = 1e-5
D_DK = 64
D_DV = 128
D_TAU = 16.0
D_SUB = 16
D_LN_EPS = 1e-6
VMEM_LIMIT = 56 * 1024 * 1024


def _cp(*sem):
    return pltpu.CompilerParams(dimension_semantics=sem, vmem_limit_bytes=VMEM_LIMIT)


def _operands(a, b, precision):
    if precision is None:
        return a.astype(BF16), b.astype(BF16)
    return a, b


def _mm(a, b, precision=None):
    a, b = _operands(a, b, precision)
    return jnp.dot(a, b, precision=precision, preferred_element_type=F32)


def _nt(a, b, precision=None):
    a, b = _operands(a, b, precision)
    return lax.dot_general(a, b, (((1,), (1,)), ((), ())), precision=precision,
                           preferred_element_type=F32)


def _tn(a, b, precision=None):
    a, b = _operands(a, b, precision)
    return lax.dot_general(a, b, (((0,), (0,)), ((), ())), precision=precision,
                           preferred_element_type=F32)


def _log_sigmoid(x):
    return jnp.minimum(x, 0.0) - jnp.log(1.0 + jnp.exp(-jnp.abs(x)))


def _seg_sum(x, ones_blockdiag):
    hi = x.astype(BF16)
    lo = (x - hi.astype(F32)).astype(BF16)
    return _mm(hi, ones_blockdiag) + _mm(lo, ones_blockdiag)


def _tri(n, rev, strict):
    row = lax.broadcasted_iota(jnp.int32, (n, n), 0)
    col = lax.broadcasted_iota(jnp.int32, (n, n), 1)
    if rev:
        return (col > row) if strict else (col >= row)
    return (col < row) if strict else (col <= row)


def _norm_mod_kernel(h_ref, g_ref, sh_ref, sc_ref, o_ref):
    x = h_ref[...]
    y = x * lax.rsqrt(jnp.mean(x * x, axis=-1, keepdims=True) + EPS) * g_ref[...]
    o_ref[...] = (y * (1.0 + sc_ref[...]) + sh_ref[...]).astype(o_ref.dtype)


def _norm_mod(h, g, modp, k_shift, rows, tm, grp):
    d = h.shape[1]
    return pl.pallas_call(
        _norm_mod_kernel,
        grid=(rows // tm,),
        in_specs=[pl.BlockSpec((tm, d), lambda i: (i, 0)),
                  pl.BlockSpec((1, d), lambda i: (0, 0)),
                  pl.BlockSpec((None, None, 1, d), lambda i: (grp(i), k_shift, 0, 0)),
                  pl.BlockSpec((None, None, 1, d), lambda i: (grp(i), k_shift + 1, 0, 0))],
        out_specs=pl.BlockSpec((tm, d), lambda i: (i, 0)),
        out_shape=jax.ShapeDtypeStruct((rows, d), BF16),
        compiler_params=_cp("arbitrary"),
        name="norm_mod",
    )(h, g.reshape(1, d), modp, modp)


def _final_norm_kernel(h_ref, g_ref, o_ref):
    x = h_ref[...]
    o_ref[...] = x * lax.rsqrt(jnp.mean(x * x, axis=-1, keepdims=True) + EPS) * g_ref[...]


def _final_norm(h, g, tm):
    rows, d = h.shape
    return pl.pallas_call(
        _final_norm_kernel,
        grid=(rows // tm,),
        in_specs=[pl.BlockSpec((tm, d), lambda i: (i, 0)), pl.BlockSpec((1, d), lambda i: (0, 0))],
        out_specs=pl.BlockSpec((tm, d), lambda i: (i, 0)),
        out_shape=jax.ShapeDtypeStruct((rows, d), F32),
        compiler_params=_cp("arbitrary"),
        name="final_norm",
    )(h, g.reshape(1, d))


def _ada_kernel(c_ref, wa_ref, wb_ref, b_ref, o_ref):
    @pl.when(pl.program_id(1) == 0)
    def _():
        o_ref[...] = jnp.broadcast_to(b_ref[...], o_ref.shape)

    c = c_ref[...]
    cond = (c * jax.nn.sigmoid(c)).astype(BF16)
    half = wa_ref.shape[1]
    o_ref[:, 0:half] += _mm(cond, wa_ref[...].astype(BF16))
    o_ref[:, half:] += _mm(cond, wb_ref[...].astype(BF16))


def _ada_mod(cpad, ada_w, ada_b, tk):
    depth, d, n = ada_w.shape
    m = cpad.shape[0]
    return pl.pallas_call(
        _ada_kernel,
        grid=(depth, d // tk),
        in_specs=[pl.BlockSpec((m, tk), lambda l, k: (0, k)),
                  pl.BlockSpec((None, tk, n // 2), lambda l, k: (l, k, 0)),
                  pl.BlockSpec((None, tk, n // 2), lambda l, k: (l, k, 1)),
                  pl.BlockSpec((None, 1, n), lambda l, k: (l, 0, 0))],
        out_specs=pl.BlockSpec((None, m, n), lambda l, k: (l, 0, 0)),
        out_shape=jax.ShapeDtypeStruct((depth, m, n), F32),
        compiler_params=_cp("arbitrary", "arbitrary"),
        name="ada_mod",
    )(cpad, ada_w, ada_w, ada_b.reshape(depth, 1, n))


def _ffn_up_kernel(x_ref, w1_ref, w3_ref, o_ref, w1b, w3b):
    @pl.when(pl.program_id(1) == 0)
    def _():
        w1b[...] = w1_ref[...].astype(BF16)
        w3b[...] = w3_ref[...].astype(BF16)

    x = x_ref[...]
    a = _mm(x, w1b[...])
    b = _mm(x, w3b[...])
    o_ref[...] = (a * jax.nn.sigmoid(a) * b).astype(o_ref.dtype)


def _ffn_up(x, w1, w3, l, s, tm, tf):
    rows, d = x.shape
    f = w1.shape[-1]
    wspec = pl.BlockSpec((None, None, d, tf), lambda j, i: (l, s, 0, j))
    return pl.pallas_call(
        _ffn_up_kernel,
        grid=(f // tf, rows // tm),
        in_specs=[pl.BlockSpec((tm, d), lambda j, i: (i, 0)), wspec, wspec],
        out_specs=pl.BlockSpec((tm, tf), lambda j, i: (i, j)),
        out_shape=jax.ShapeDtypeStruct((rows, f), BF16),
        scratch_shapes=[pltpu.VMEM((d, tf), BF16), pltpu.VMEM((d, tf), BF16)],
        compiler_params=_cp("arbitrary", "arbitrary"),
        name="ffn_up",
    )(x, w1, w3)


def _ffn_down_kernel(g_ref, w_ref, h_ref, gate_ref, o_ref):
    acc = _mm(g_ref[...], w_ref[...])
    o_ref[...] = h_ref[...] + (0.5 * acc) * gate_ref[...]


def _ffn_down(gact, w2, h, modp, k_gate, tm, tn, grp):
    rows, f = gact.shape
    d = h.shape[1]
    return pl.pallas_call(
        _ffn_down_kernel,
        grid=(rows // tm, d // tn),
        in_specs=[pl.BlockSpec((tm, f), lambda i, j: (i, 0)),
                  pl.BlockSpec((f, tn), lambda i, j: (0, j)),
                  pl.BlockSpec((tm, tn), lambda i, j: (i, j)),
                  pl.BlockSpec((None, None, 1, tn), lambda i, j: (grp(i), k_gate, 0, j))],
        out_specs=pl.BlockSpec((tm, tn), lambda i, j: (i, j)),
        out_shape=jax.ShapeDtypeStruct((rows, d), F32),
        compiler_params=_cp("arbitrary", "arbitrary"),
        name="ffn_down",
    )(gact, w2, h, modp)


def _in_proj_kernel(x_ref, w_ref, b_ref, o_ref):
    o_ref[...] = _mm(x_ref[...], w_ref[...]) + b_ref[...]


def _gate_proj_kernel(x_ref, w_ref, b_ref, o_ref):
    o_ref[...] = jax.nn.sigmoid(_mm(x_ref[...], w_ref[...]) + b_ref[...]).astype(o_ref.dtype)


def _in_proj(x, w, b, rows, tm, tn, gates=False):
    d = x.shape[1]
    n = w.shape[1]
    return pl.pallas_call(
        _gate_proj_kernel if gates else _in_proj_kernel,
        grid=(rows // tm, n // tn),
        in_specs=[pl.BlockSpec((tm, d), lambda i, j: (i, 0)),
                  pl.BlockSpec((d, tn), lambda i, j: (0, j)),
                  pl.BlockSpec((1, tn), lambda i, j: (0, j))],
        out_specs=pl.BlockSpec((tm, tn), lambda i, j: (i, j)),
        out_shape=jax.ShapeDtypeStruct((rows, n), BF16 if gates else F32),
        compiler_params=_cp("arbitrary", "arbitrary"),
        name="gate_proj" if gates else "in_proj",
    )(x, w, b)


def _merge_out_kernel(ha, hb, hc, hd, ga, gb, gc, gd, wa, wb, wc, wd, ow_ref, h_ref, gate_ref, o_ref, y_scr, *, nj):
    j = pl.program_id(1)

    @pl.when(j < nj)
    def _():
        y = ga[...].astype(F32) * _mm(ha[...], wa[...])
        y = y + gb[...].astype(F32) * _mm(hb[...], wb[...])
        y = y + gc[...].astype(F32) * _mm(hc[...], wc[...])
        y = y + gd[...].astype(F32) * _mm(hd[...], wd[...])
        y_scr[j] = y.astype(y_scr.dtype)

    @pl.when(j >= nj)
    def _():
        tk = y_scr.shape[2]
        w = ow_ref[...]
        acc = _mm(y_scr[0], w[0:tk])
        for k in range(1, nj):
            acc = acc + _mm(y_scr[k], w[k * tk:(k + 1) * tk])
        o_ref[...] = h_ref[...] + acc * gate_ref[...]


def _merge_out(hs, gates, br_w, out_w, h, modp, k_gate, l, rows, tm, tn, grp):
    wbr = hs[0].shape[1]
    d = br_w.shape[-1]
    nj = d // tn
    mj = lambda j: jnp.minimum(j, nj - 1)
    oj = lambda j: jnp.maximum(j - nj, 0)
    hspec = pl.BlockSpec((tm, wbr), lambda i, j: (i, 0))
    gspecs = [pl.BlockSpec((tm, tn), functools.partial(lambda i, j, n: (i, n * nj + mj(j)), n=n))
              for n in range(4)]
    wspecs = [pl.BlockSpec((None, None, wbr, tn), functools.partial(lambda i, j, n: (l, n, 0, mj(j)), n=n))
              for n in range(4)]
    return pl.pallas_call(
        functools.partial(_merge_out_kernel, nj=nj),
        grid=(rows // tm, 2 * nj),
        in_specs=[hspec] * 4 + gspecs + wspecs
                 + [pl.BlockSpec((None, d, tn), lambda i, j: (l, 0, oj(j))),
                    pl.BlockSpec((tm, tn), lambda i, j: (i, oj(j))),
                    pl.BlockSpec((None, None, 1, tn), lambda i, j: (grp(i), k_gate, 0, oj(j)))],
        out_specs=pl.BlockSpec((tm, tn), lambda i, j: (i, oj(j))),
        out_shape=jax.ShapeDtypeStruct((rows, d), F32),
        scratch_shapes=[pltpu.VMEM((nj, tm, tn), BF16)],
        compiler_params=_cp("arbitrary", "arbitrary"),
        name="merge_out",
    )(*hs, gates, gates, gates, gates, br_w, br_w, br_w, br_w, out_w, h, modp)


def _tile_place(geom):
    n_lat_tiles, lat_tiles, ctx_tiles = geom
    i = pl.program_id(0)
    is_ctx = i >= n_lat_tiles
    per_seq = jnp.where(is_ctx, ctx_tiles, lat_tiles)
    pos = lax.rem(jnp.where(is_ctx, i - n_lat_tiles, i), per_seq)
    return is_ctx, pos == 0, pos == per_seq - 1


def _row_neighbours(xp_ref, x, xn_ref, first, last):
    ts = x.shape[0]
    hp = xp_ref.shape[0]
    t = lax.broadcasted_iota(jnp.int32, x.shape, 0)
    prv = jnp.where(t == 0, jnp.where(first, 0.0, xp_ref[hp - 1:hp, :]), pltpu.roll(x, 1, 0))
    nxt = jnp.where(t == ts - 1, jnp.where(last, 0.0, xn_ref[0:1, :]), pltpu.roll(x, ts - 1, 0))
    return prv, nxt


def _token_shift(xp_ref, x_ref, xn_ref, mu_ref, geom):
    is_ctx, first, last = _tile_place(geom)
    x = x_ref[...]
    ts = x.shape[0]
    t = lax.broadcasted_iota(jnp.int32, x.shape, 0)
    lane = lax.broadcasted_iota(jnp.int32, x.shape, 1)
    prv, nxt = _row_neighbours(xp_ref, x, xn_ref, first, last)
    sh_ctx = jnp.where((lane & 1) == 0, prv, nxt)
    tw = t & (GRID_W - 1)
    left = jnp.where(tw == 0, 0.0, prv)
    right = jnp.where(tw == GRID_W - 1, 0.0, nxt)
    up = jnp.concatenate([jnp.where(first, 0.0, xp_ref[...]), x[0:ts - GRID_W]], axis=0)
    down = jnp.concatenate([x[GRID_W:ts], jnp.where(last, 0.0, xn_ref[...])], axis=0)
    c4 = lane & 3
    sh_lat = jnp.where(c4 == 0, left, jnp.where(c4 == 1, right, jnp.where(c4 == 2, up, down)))
    sh = jnp.where(is_ctx, sh_ctx, sh_lat)
    return x + (sh - x) * mu_ref[...]


def _conv_kernel(xp_ref, x_ref, xn_ref, w_ref, b_ref, s_ref, o_ref, *, geom):
    _, first, last = _tile_place(geom)
    x = x_ref[...]
    prv, nxt = _row_neighbours(xp_ref, x, xn_ref, first, last)
    w = w_ref[...]
    y = prv * w[0:1] + x * w[1:2] + nxt * w[2:3] + b_ref[...]
    o_ref[...] = y * jax.nn.sigmoid(y) * s_ref[...]


def _sequence_tiles(body, p, col0, width, halo, small, out_widths, nb, t_lat, t_ctx, name):
    rows = p.shape[0]
    ts = min(256, t_ctx)
    assert t_lat % ts == 0 and t_ctx % ts == 0 and ts >= 2 * GRID_W and ts % halo == 0 and col0 % 128 == 0
    nt = rows // ts
    nh = rows // halo
    r = ts // halo
    geom = (nb * t_lat // ts, t_lat // ts, t_ctx // ts)
    full = lambda a: pl.BlockSpec(a.shape, lambda i: (0,) * a.ndim)
    window = lambda n: (pl.Element(n), pl.Element(width))
    return pl.pallas_call(
        functools.partial(body, geom=geom),
        grid=(nt,),
        in_specs=[pl.BlockSpec(window(halo), lambda i: (jnp.maximum(i * r - 1, 0) * halo, col0)),
                  pl.BlockSpec(window(ts), lambda i: (i * ts, col0)),
                  pl.BlockSpec(window(halo), lambda i: (jnp.minimum((i + 1) * r, nh - 1) * halo, col0))]
                 + [full(a) for a in small],
        out_specs=[pl.BlockSpec((ts, ow), lambda i: (i, 0)) for ow in out_widths],
        out_shape=[jax.ShapeDtypeStruct((rows, ow), F32) for ow in out_widths],
        compiler_params=_cp("arbitrary"),
        name=name,
    )(p, p, p, *small)


def _rwkv_pre_kernel(xp_ref, x_ref, xn_ref, mu_ref, kk_ref, ka_ref, rk_ref, w0_ref, a0_ref, wup_ref, aup_ref,
                     gup_ref, ones_ref, r_o, v_o, kk_o, lwf_o, lwb_o, ktf_o, ktb_o, bf_o, bb_o, bonus_o, g_o, *,
                     geom):
    wb = r_o.shape[1]
    za = _token_shift(xp_ref, x_ref, xn_ref, mu_ref, geom)
    r = za[:, 0:wb]
    k = za[:, wb:2 * wb]
    v = za[:, 2 * wb:3 * wb]
    wd = jnp.tanh(za[:, 3 * wb:3 * wb + 128])
    ad = za[:, 3 * wb + 128:3 * wb + 256]
    gd = jax.nn.sigmoid(za[:, 3 * wb + 256:3 * wb + 384])
    ones = ones_ref[...]
    kq = k * kk_ref[...]
    kk = kq * lax.rsqrt(jnp.maximum(_seg_sum(kq * kq, ones), 1e-24))
    r_o[...] = r
    v_o[...] = v
    kk_o[...] = kk
    ka = ka_ref[...]
    for d, (lw_o, kt_o, b_o) in enumerate(((lwf_o, ktf_o, bf_o), (lwb_o, ktb_o, bb_o))):
        xw = w0_ref[d:d + 1] + _mm(wd, wup_ref[d])
        lw_o[...] = -jax.nn.sigmoid(xw) * 0.6065306597126334
        a = jax.nn.sigmoid(a0_ref[d:d + 1] + _mm(ad, aup_ref[d]))
        kt_o[...] = k * (1.0 + (a - 1.0) * ka)
        b_o[...] = kk * a
    bonus_o[...] = _seg_sum(r * k * rk_ref[...], ones) * v
    g_o[...] = _mm(gd, gup_ref[...])


def _scan_geometry(nb, t_lat, t_ctx, chunk):
    nlc = t_lat // chunk
    ncc = t_ctx // chunk
    nch = nlc + ncc

    def fwd(j):
        return jnp.where(j < ncc, nlc + j, j - ncc)

    def rev(j):
        return nch - 1 - j

    def row(b, c):
        return jnp.where(c < nlc, b * nlc + c, nb * nlc + b * ncc + (c - nlc))

    return fwd, rev, row, nch


def _scan_in_specs(nb, row, cm, chunk, width, col):
    return [pl.BlockSpec((pl.Element(chunk), pl.Element(width)),
                         functools.partial(lambda j, b: (row(b, cm(j)) * chunk, col), b=b)) for b in range(nb)]


def _scan_out_spec(nb, cm, chunk, width):
    return pl.BlockSpec((None, nb, chunk, width), lambda j: (cm(j), 0, 0, 0))


def _rwkv_scan_kernel(*refs, nb):
    ins = refs[:12 * nb]
    yf, yb, s_ref = refs[12 * nb:]

    @pl.when(pl.program_id(0) == 0)
    def _():
        s_ref[...] = jnp.zeros_like(s_ref)

    n, wbw = ins[0].shape
    nh = wbw // A_HD
    heads = []
    for d, y_r in enumerate((yf, yb)):
        rev = d == 1
        arr = lambda a, b: ins[(d * 6 + a) * nb + b][...]
        lws = [arr(3, b) for b in range(nb)]
        c_all = _mm(_tri(n, rev, False).astype(F32), jnp.concatenate(lws, axis=1), HIGHEST)
        strict = _tri(n, rev, True)
        incl = _tri(n, rev, False)
        for b in range(nb):
            lw = lws[b]
            c = c_all[:, b * wbw:(b + 1) * wbw]
            eg = jnp.exp(c)
            ieg = jnp.exp(-c)
            r_all = arr(0, b) * eg
            kk_all = arr(2, b) * jnp.exp(c - lw)
            kt_all = arr(4, b) * ieg
            b_all = arr(5, b) * ieg
            v_all = arr(1, b)
            g_last = eg[0:1] if rev else eg[n - 1:n]
            for h in range(nh):
                sl = slice(h * A_HD, (h + 1) * A_HD)
                idx = (d * nb + b) * nh + h
                heads.append(dict(r=r_all[:, sl], kk=kk_all[:, sl], b=b_all[:, sl], kt=kt_all[:, sl],
                                  v=v_all[:, sl], s=s_ref[idx], g=g_last[:, sl], strict=strict, incl=incl,
                                  y_ref=y_r.at[b], sl=sl, idx=idx))
    for t in heads:
        t["z"] = _nt(jnp.concatenate([t["kk"], t["r"]], axis=0), jnp.concatenate([t["kt"], t["b"], t["s"]], axis=0))
    for t in heads:
        z = t["z"]
        t["a_kv"] = jnp.where(t["strict"], z[0:n, 0:n], 0.0)
        t["a_kb"] = jnp.where(t["strict"], z[0:n, n:2 * n], 0.0)
        t["r_kv"] = jnp.where(t["incl"], z[n:2 * n, 0:n], 0.0)
        t["r_kb"] = jnp.where(t["incl"], z[n:2 * n, n:2 * n], 0.0)
    for t in heads:
        t["av"] = _mm(jnp.concatenate([t["a_kv"], t["r_kv"]], axis=0), t["v"])
    left = lax.broadcasted_iota(jnp.int32, (n, 2 * n), 1) < n
    for t in heads:
        t["w"] = jnp.concatenate([t["a_kb"], t["z"][0:n, 2 * n:] + t["av"][0:n]], axis=1)
    for t in heads:
        r = _mm(t["a_kb"], t["w"])
        t["w"] = jnp.where(left, r, t["w"] - r)
    m = 2
    while m < n:
        for t in heads:
            r = _mm(t["w"][:, 0:n], t["w"])
            t["w"] = jnp.where(left, r, t["w"] + r)
        m *= 2
    for t in heads:
        t["u"] = t["w"][:, n:2 * n]
        y = t["z"][n:2 * n, 2 * n:] + t["av"][n:2 * n] - _mm(t["r_kb"], t["u"])
        t["y_ref"][:, t["sl"]] = y
        s_new = t["s"] + _tn(jnp.concatenate([t["v"], t["u"]], axis=0), jnp.concatenate([t["kt"], -t["b"]], axis=0))
        s_ref[t["idx"]] = s_new * t["g"]


def _rwkv_scan(r, v, kk, lwf, lwb, ktf, ktb, bf, bb, nb, t_lat, t_ctx):
    wb = r.shape[1]
    fwd, rev, row, nch = _scan_geometry(nb, t_lat, t_ctx, CHUNK)
    in_specs = [s for cm in (fwd, rev) for _ in range(6) for s in _scan_in_specs(nb, row, cm, CHUNK, wb, 0)]
    args = [a for grp in ((r, v, kk, lwf, ktf, bf), (r, v, kk, lwb, ktb, bb)) for a in grp for _ in range(nb)]
    return pl.pallas_call(
        functools.partial(_rwkv_scan_kernel, nb=nb),
        grid=(nch,),
        in_specs=in_specs,
        out_specs=[_scan_out_spec(nb, fwd, CHUNK, wb), _scan_out_spec(nb, rev, CHUNK, wb)],
        out_shape=[jax.ShapeDtypeStruct((nch, nb, CHUNK, wb), F32)] * 2,
        scratch_shapes=[pltpu.VMEM((2 * nb * (wb // A_HD), A_HD, A_HD), F32)],
        compiler_params=_cp("arbitrary"),
        name="rwkv_scan",
    )(*args)


def _mlstm_scan_kernel(*refs, nb):
    ins = refs[:8 * nb]
    gbias_ref, hf, hb, c_ref, n_ref, m_ref = refs[8 * nb:]

    @pl.when(pl.program_id(0) == 0)
    def _():
        c_ref[...] = jnp.zeros_like(c_ref)
        n_ref[...] = jnp.zeros_like(n_ref)
        m_ref[...] = jnp.zeros_like(m_ref)

    n = ins[0].shape[0]
    nh = ins[0].shape[1] // C_HD
    heads = []
    for d, h_r in enumerate((hf, hb)):
        rev = d == 1
        arr = lambda a, b: ins[(d * 4 + a) * nb + b][...]
        gates = [arr(3, b) + gbias_ref[...] for b in range(nb)]
        fgs = [_log_sigmoid(g) for g in gates]
        mask = _tri(n, rev, False)
        mi = mask.astype(F32)
        bcol_all = _mm(mi, jnp.concatenate(fgs, axis=1), HIGHEST)
        brow_all = _nt(jnp.concatenate([f.T for f in fgs], axis=0), mi, HIGHEST)
        for b in range(nb):
            bcol = bcol_all[:, b * 128:(b + 1) * 128]
            brow = brow_all[b * 128:(b + 1) * 128]
            gates_t = gates[b].T
            q_all, k_all, v_all = arr(0, b), arr(1, b), arr(2, b)
            for h in range(nh):
                sl = slice(h * C_HD, (h + 1) * C_HD)
                ii = d * 2 * nh + h
                fi = ii + nh
                idx = (d * nb + b) * nh + h
                b_c = bcol[:, fi:fi + 1]
                heads.append(dict(q=q_all[:, sl], k=k_all[:, sl], v=v_all[:, sl], b_c=b_c, b_r=brow[fi:fi + 1, :],
                                  i_c=gates[b][:, ii:ii + 1], i_r=gates_t[ii:ii + 1, :], m=m_ref[idx][:, 0:1],
                                  cm=c_ref[idx], nn=n_ref[idx], mask=mask,
                                  b_l=b_c[0:1] if rev else b_c[n - 1:n], idx=idx, h_ref=h_r.at[b], sl=sl))
    for t in heads:
        t["qk"] = _nt(t["q"], t["k"])
        t["qc"] = _nt(t["q"], t["cm"])
    for t in heads:
        t["dlog"] = jnp.where(t["mask"], t["b_c"] - t["b_r"] + t["i_r"], -jnp.inf)
        t["gl"] = t["b_l"] - t["b_c"] + t["i_c"]
    for t in heads:
        t["dmax"] = jnp.max(t["dlog"], axis=1, keepdims=True)
        t["gmax"] = jnp.max(t["gl"], axis=0, keepdims=True)
    for t in heads:
        inter = t["b_c"] + t["m"]
        m_t = jnp.maximum(inter, t["dmax"])
        t["iw"] = jnp.exp(inter - m_t)
        t["m_t"] = m_t
        t["s"] = t["qk"] * jnp.exp(t["dlog"] - m_t)
        m_new = jnp.maximum(t["b_l"] + t["m"], t["gmax"])
        t["sw"] = jnp.exp(t["gl"] - m_new)
        t["dec"] = jnp.exp(t["b_l"] + t["m"] - m_new)
        t["m_new"] = m_new
    for t in heads:
        t["sv"] = _mm(t["s"], t["v"])
        t["vk"] = _tn(t["v"] * t["sw"], t["k"])
    for t in heads:
        t["rs"] = jnp.sum(t["s"], axis=1, keepdims=True)
        t["qn"] = jnp.sum(t["q"] * t["nn"], axis=1, keepdims=True)
        t["ks"] = jnp.sum(t["sw"] * t["k"], axis=0, keepdims=True)
    for t in heads:
        iw = t["iw"]
        num = t["sv"] + iw * t["qc"]
        den = jnp.maximum(jnp.abs(t["rs"] + iw * t["qn"]), jnp.exp(-t["m_t"]))
        t["h_ref"][:, t["sl"]] = num / den
        idx = t["idx"]
        c_ref[idx] = t["dec"] * t["cm"] + t["vk"]
        n_ref[idx] = t["dec"] * t["nn"] + t["ks"]
        m_ref[idx] = jnp.broadcast_to(t["m_new"], m_ref.shape[1:])


def _mlstm_scan(qk, p, gbias, col_v, col_g, nb, t_lat, t_ctx, wb):
    ck = C_CHUNK
    fwd, rev, row, nch = _scan_geometry(nb, t_lat, t_ctx, ck)
    nh = wb // C_HD

    def specs(cm):
        return (_scan_in_specs(nb, row, cm, ck, wb, 0) + _scan_in_specs(nb, row, cm, ck, wb, wb)
                + _scan_in_specs(nb, row, cm, ck, wb, col_v) + _scan_in_specs(nb, row, cm, ck, 128, col_g))

    args = ([qk] * nb + [qk] * nb + [p] * nb + [p] * nb) * 2
    return pl.pallas_call(
        functools.partial(_mlstm_scan_kernel, nb=nb),
        grid=(nch,),
        in_specs=specs(fwd) + specs(rev) + [pl.BlockSpec((1, 128), lambda j: (0, 0))],
        out_specs=[_scan_out_spec(nb, fwd, ck, wb), _scan_out_spec(nb, rev, ck, wb)],
        out_shape=[jax.ShapeDtypeStruct((nch, nb, ck, wb), F32)] * 2,
        scratch_shapes=[pltpu.VMEM((2 * nb * nh, C_HD, C_HD), F32), pltpu.VMEM((2 * nb * nh, 1, C_HD), F32),
                        pltpu.VMEM((2 * nb * nh, 1, 128), F32)],
        compiler_params=_cp("arbitrary"),
        name="mlstm_scan",
    )(*args, gbias)


def _gla_scan_kernel(*refs, nb):
    ins = refs[:8 * nb]
    aup_ref, abias_ref, of, ob, s_ref = refs[8 * nb:]

    @pl.when(pl.program_id(0) == 0)
    def _():
        s_ref[...] = jnp.zeros_like(s_ref)

    n, wkw = ins[0].shape
    nh = wkw // D_DK
    nsub = n // D_SUB
    pw = 2 * D_DK
    row = lax.broadcasted_iota(jnp.int32, (n, pw), 0)
    colx = lax.broadcasted_iota(jnp.int32, (n, pw), 1) & (D_DK - 1)
    rel = colx - (row & -D_SUB)
    rin = row & (D_SUB - 1)
    same_head = ((lax.broadcasted_iota(jnp.int32, (pw, pw), 0) & D_DK)
                 == (lax.broadcasted_iota(jnp.int32, (pw, pw), 1) & D_DK))
    ones_pair = jnp.where(same_head, 1.0, 0.0).astype(BF16)
    heads, pairs = [], []
    for d, o_r in enumerate((of, ob)):
        rev = d == 1
        arr = lambda a, b: ins[(d * 4 + a) * nb + b][...]
        la_rows = _log_sigmoid(_mm(jnp.concatenate([arr(3, b) for b in range(nb)], axis=0), aup_ref[d])
                               + abias_ref[d]) * (1.0 / D_TAU)
        la_all = jnp.concatenate([la_rows[b * n:(b + 1) * n] for b in range(nb)], axis=1)
        bc_all = _mm(_tri(n, rev, False).astype(F32), la_all, HIGHEST)
        for b in range(nb):
            bc = bc_all[:, b * wkw:(b + 1) * wkw]
            q_all = arr(0, b) * (D_DK ** -0.5)
            k_all = arr(1, b)
            v_all = arr(2, b)
            base = (d * nb + b) * nh
            for h in range(nh):
                ksl = slice(h * D_DK, (h + 1) * D_DK)
                vsl = slice(h * D_DV, (h + 1) * D_DV)
                heads.append(dict(q=q_all[:, ksl], k=k_all[:, ksl], v=v_all[:, vsl], bc=bc[:, ksl], rev=rev,
                                  s=s_ref[base + h], idx=base + h, o_ref=o_r.at[b], vsl=vsl))
            for p in range(nh // 2):
                psl = slice(p * pw, (p + 1) * pw)
                pairs.append(dict(q=q_all[:, psl], k=k_all[:, psl], bc=bc[:, psl], rev=rev,
                                  heads=(base + 2 * p, base + 2 * p + 1)))
    for t in heads:
        t["o"] = _nt(t["q"] * jnp.exp(t["bc"]), t["s"])
    for t in heads:
        q, k, bc, rev = t["q"], t["k"], t["bc"], t["rev"]
        pieces = []
        for blk in range(nsub):
            r0 = blk * D_SUB
            edge = bc[r0 + D_SUB - 1:r0 + D_SUB] if rev else bc[r0:r0 + 1]
            has_other = blk < nsub - 1 if rev else blk > 0
            if has_other:
                qs = q[r0:r0 + D_SUB] * jnp.exp(bc[r0:r0 + D_SUB] - edge)
                ks = k * jnp.exp(jnp.minimum(edge - bc, 0.0))
                pieces.append(_nt(qs, ks))
            else:
                pieces.append(jnp.zeros((D_SUB, n), F32))
        t["sc"] = jnp.concatenate(pieces, axis=0)
    for t in pairs:
        q, k, bc = t["q"], t["k"], t["bc"]
        es = []
        for j in range(D_SUB):
            kj = jnp.concatenate([jnp.broadcast_to(k[b * D_SUB + j:b * D_SUB + j + 1], (D_SUB, pw))
                                  for b in range(nsub)], axis=0)
            bj = jnp.concatenate([jnp.broadcast_to(bc[b * D_SUB + j:b * D_SUB + j + 1], (D_SUB, pw))
                                  for b in range(nsub)], axis=0)
            es.append(q * kj * jnp.exp(bc - bj))
        t["e"] = jnp.concatenate(es, axis=0)
    for t in pairs:
        t["c"] = _seg_sum(t["e"], ones_pair)
    for t in pairs:
        rev = t["rev"]
        sc = jnp.concatenate([heads[t["heads"][0]]["sc"], heads[t["heads"][1]]["sc"]], axis=1)
        a = jnp.where((rel >= D_SUB) if rev else (rel < 0), sc, 0.0)
        for j in range(D_SUB):
            keep = (rel == j) & ((rin <= j) if rev else (rin >= j))
            a = jnp.where(keep, t["c"][j * n:(j + 1) * n], a)
        heads[t["heads"][0]]["a"] = a[:, 0:D_DK]
        heads[t["heads"][1]]["a"] = a[:, D_DK:pw]
    for t in heads:
        t["o_ref"][:, t["vsl"]] = t["o"] + _mm(t["a"], t["v"])
        bc = t["bc"]
        b_l = bc[0:1] if t["rev"] else bc[n - 1:n]
        s_ref[t["idx"]] = t["s"] * jnp.exp(b_l) + _tn(t["v"], t["k"] * jnp.exp(b_l - bc))


def _gla_scan(p, aupp, abias, col_q, col_k, col_v, col_a, nb, t_lat, t_ctx, wb):
    fwd, rev, row, nch = _scan_geometry(nb, t_lat, t_ctx, CHUNK)
    wk = aupp.shape[-1]
    nh = wk // D_DK

    def specs(cm):
        return (_scan_in_specs(nb, row, cm, CHUNK, wk, col_q) + _scan_in_specs(nb, row, cm, CHUNK, wk, col_k)
                + _scan_in_specs(nb, row, cm, CHUNK, wb, col_v) + _scan_in_specs(nb, row, cm, CHUNK, 128, col_a))

    return pl.pallas_call(
        functools.partial(_gla_scan_kernel, nb=nb),
        grid=(nch,),
        in_specs=specs(fwd) + specs(rev) + [pl.BlockSpec(aupp.shape, lambda j: (0, 0, 0)),
                                           pl.BlockSpec(abias.shape, lambda j: (0, 0, 0))],
        out_specs=[_scan_out_spec(nb, fwd, CHUNK, wb), _scan_out_spec(nb, rev, CHUNK, wb)],
        out_shape=[jax.ShapeDtypeStruct((nch, nb, CHUNK, wb), F32)] * 2,
        scratch_shapes=[pltpu.VMEM((2 * nb * nh, D_DV, D_DK), F32)],
        compiler_params=_cp("arbitrary"),
        name="gla_scan",
    )(*([p] * (8 * nb)), aupp, abias)


def _branch_out_kernel(ya_f, ya_b, bonus, ga, pb, hc_f, hc_b, oc, od_f, od_b, gdd,
                       a_lng, a_lnb, ones, b_ws, b_bias, b_lng, b_lnb, c_lng, d_lng,
                       hs_a, hs_b, hs_c, hs_d):
    wb = hs_a.shape[1]
    tm = hs_a.shape[0]
    y = ya_f[...].reshape(tm, wb) + ya_b[...].reshape(tm, wb)
    on = ones[...]
    mu = _seg_sum(y, on) * (1.0 / A_HD)
    yc = y - mu
    var = _seg_sum(yc * yc, on) * (1.0 / A_HD)
    yn = yc * lax.rsqrt(var + A_LN_EPS) * a_lng[...] + a_lnb[...]
    hs_a[...] = ((yn + bonus[...]) * ga[...]).astype(hs_a.dtype)
    z = jax.nn.gelu(pb[...])
    u = z[:, 0:wb]
    vv = z[:, wb:2 * wb]
    mu = jnp.mean(vv, axis=-1, keepdims=True)
    vc = vv - mu
    vn = (vc * lax.rsqrt(jnp.mean(vc * vc, axis=-1, keepdims=True) + 1e-5) * b_lng[...] + b_lnb[...])
    vn = vn.astype(BF16)
    for ck in range(u.shape[0] // B_CHUNK):
        rs = slice(ck * B_CHUNK, (ck + 1) * B_CHUNK)
        for g in range(wb // B_CHUNK):
            cs = slice(g * B_CHUNK, (g + 1) * B_CHUNK)
            s = _mm(b_ws[g].astype(BF16), vn[rs, cs]) + b_bias[:, cs]
            hs_b[rs, cs] = (u[rs, cs] * s).astype(hs_b.dtype)
    hc = hc_f[...].reshape(tm, wb) + hc_b[...].reshape(tm, wb)
    ogate = jax.nn.sigmoid(oc[...])
    gc = c_lng[...]
    for h in range(wb // C_HD):
        sl = slice(h * C_HD, (h + 1) * C_HD)
        x = hc[:, sl]
        xc = x - jnp.mean(x, axis=-1, keepdims=True)
        xn = xc * lax.rsqrt(jnp.mean(xc * xc, axis=-1, keepdims=True) + C_LN_EPS) * gc[:, sl]
        hs_c[:, sl] = (xn * ogate[:, sl]).astype(hs_c.dtype)
    od = od_f[...].reshape(tm, wb) + od_b[...].reshape(tm, wb)
    gg = gdd[...]
    gg = gg * jax.nn.sigmoid(gg)
    gd_ = d_lng[...]
    for h in range(wb // D_DV):
        sl = slice(h * D_DV, (h + 1) * D_DV)
        x = od[:, sl]
        xn = x * lax.rsqrt(jnp.mean(x * x, axis=-1, keepdims=True) + D_LN_EPS) * gd_[:, sl]
        hs_d[:, sl] = (xn * gg[:, sl]).astype(hs_d.dtype)


def _branch_out(ya_f, ya_b, bonus, ga, p, pd, hc_f, hc_b, od_f, od_b, prm, cols, rows, nb, t_lat, t_ctx):
    wb = bonus.shape[1]
    tm = min(256, t_ctx)
    n_lat_tiles, lat_tiles, ctx_tiles = nb * t_lat // tm, t_lat // tm, t_ctx // tm
    row = lambda w: pl.BlockSpec((tm, w), lambda i: (i, 0))
    cols_at = lambda w, c0: pl.BlockSpec((pl.Element(tm), pl.Element(w)), lambda i: (i * tm, c0))

    def scan_out(a):
        per_tile = tm // a.shape[2]

        def imap(i):
            ic = i - n_lat_tiles
            seq = jnp.where(i < n_lat_tiles, i // lat_tiles, ic // ctx_tiles)
            blk = jnp.where(i < n_lat_tiles, lax.rem(i, lat_tiles), lat_tiles + lax.rem(ic, ctx_tiles))
            return blk, seq, 0, 0

        return pl.BlockSpec((per_tile, None, a.shape[2], wb), imap)

    small = [prm["a_lng"], prm["a_lnb"], prm["ones"], prm["b_ws"], prm["b_bias"], prm["b_lng"], prm["b_lnb"],
             prm["c_lng"], prm["d_lng"]]
    full = lambda a: pl.BlockSpec(a.shape, lambda i: (0,) * a.ndim)
    return pl.pallas_call(
        _branch_out_kernel,
        grid=(rows // tm,),
        in_specs=[scan_out(ya_f), scan_out(ya_b), row(wb), row(wb), cols_at(2 * wb, cols["b"]),
                  scan_out(hc_f), scan_out(hc_b), cols_at(wb, cols["c_o"]), scan_out(od_f), scan_out(od_b),
                  cols_at(wb, cols["d_g"])] + [full(a) for a in small],
        out_specs=[row(wb)] * 4,
        out_shape=[jax.ShapeDtypeStruct((rows, wb), BF16)] * 4,
        compiler_params=_cp("arbitrary"),
        name="branch_out",
    )(ya_f, ya_b, bonus, ga, p, hc_f, hc_b, p, od_f, od_b, pd, *small)


def _pick_tile(*sizes):
    for t in (1024, 512, 256, 128):
        if all(s % t == 0 for s in sizes):
            return t
    raise ValueError("token counts must be multiples of 128")


def _pick_cols(n, cands):
    for t in cands:
        if n % t == 0:
            return t
    raise ValueError(f"no column tile for {n}")


def _proj_cols(d_model):
    wb = d_model // 4
    col = {"a": 0, "b": 3 * wb + 384}
    col["c_qk"] = col["b"] + 2 * wb
    col["c_v"] = col["c_qk"] + 2 * wb
    col["c_o"] = col["c_v"] + wb
    col["c_gate"] = col["c_o"] + wb
    col["abc_end"] = col["c_gate"] + 128
    col.update({"d_q": 0, "d_k": wb // 2, "d_v": wb, "d_g": 2 * wb, "d_a": 3 * wb, "d_end": 3 * wb + 128})
    assert all(v % 128 == 0 for v in col.values())
    return col


def _split_proj(a, d_model):
    col = _proj_cols(d_model)
    o_d = col["c_gate"] + 16
    n_d = col["d_a"] + 32
    o_g = o_d + n_d
    pad = jnp.zeros(a.shape[:-1] + (col["d_end"] - n_d,), a.dtype)
    return jnp.concatenate([a[..., o_d:o_g], pad], axis=-1), a[..., o_g:o_g + 4 * d_model]


def _mixers(p, pd, l, nb, t_lat, t_ctx, rows, w):
    wb = w["a_kk"].shape[1]
    col = _proj_cols(4 * wb)
    a_cols = 3 * wb + 384
    row = lambda a: a.reshape(1, -1)
    ones_bd = jnp.kron(jnp.eye(wb // A_HD, dtype=F32), jnp.ones((A_HD, A_HD), F32)).astype(BF16)

    a_wup, a_aup = w["a_wup"], w["a_aup"]
    wupp = jnp.zeros((2, 128, wb), F32).at[0, 0:64].set(a_wup[l, 0]).at[1, 64:128].set(a_wup[l, 1])
    aupp = jnp.zeros((2, 128, wb), F32).at[0, 0:64].set(a_aup[l, 0]).at[1, 64:128].set(a_aup[l, 1])
    small_a = [row(w["a_mu"][l]), row(w["a_kk"][l]), row(w["a_ka"][l]), row(w["a_rk"][l]), w["a_w0"][l],
               w["a_a0"][l], wupp, aupp, w["a_gup"][l], ones_bd]
    r_, v_, kk_, lwf, lwb, ktf, ktb, bf_, bb_, bonus, ga = _sequence_tiles(
        _rwkv_pre_kernel, p, col["a"], a_cols, GRID_W, small_a, [wb] * 11, nb, t_lat, t_ctx, "rwkv_pre")
    ya_f, ya_b = _rwkv_scan(r_, v_, kk_, lwf, lwb, ktf, ktb, bf_, bb_, nb, t_lat, t_ctx)

    kscale = jnp.concatenate([jnp.ones((wb,), F32), jnp.full((wb,), C_HD ** -0.5, F32)]).reshape(1, -1)
    qk, = _sequence_tiles(_conv_kernel, p, col["c_qk"], 2 * wb, 8,
                          [w["c_conv_w"][l], row(w["c_conv_b"][l]), kscale], [2 * wb], nb, t_lat, t_ctx,
                          "mlstm_conv")
    gbias = jnp.zeros((1, 128), F32).at[0, 0:16].set(w["c_gate_b"][l].reshape(-1))
    hc_f, hc_b = _mlstm_scan(qk, p, gbias, col["c_v"], col["c_gate"], nb, t_lat, t_ctx, wb)

    d_aup = w["d_aup"]
    rk_d = d_aup.shape[2]
    aupp_d = (jnp.zeros((2, 128, wb // 2), F32).at[0, 0:rk_d].set(d_aup[l, 0])
              .at[1, rk_d:2 * rk_d].set(d_aup[l, 1]))
    od_f, od_b = _gla_scan(pd, aupp_d, w["d_ab"][l].reshape(2, 1, -1), col["d_q"], col["d_k"], col["d_v"],
                           col["d_a"], nb, t_lat, t_ctx, wb)

    b_bias = jnp.repeat(w["b_bs"][l].T, B_CHUNK, axis=1)
    prm_o = {"a_lng": row(w["a_ln_g"][l]), "a_lnb": row(w["a_ln_b"][l]), "ones": ones_bd, "b_ws": w["b_ws"][l],
             "b_bias": b_bias, "b_lng": row(w["b_ln_g"][l]), "b_lnb": row(w["b_ln_b"][l]),
             "c_lng": row(w["c_ln_g"][l]), "d_lng": row(w["d_ln_g"][l])}
    return _branch_out(ya_f, ya_b, bonus, ga, p, pd, hc_f, hc_b, od_f, od_b, prm_o, col, rows, nb, t_lat, t_ctx)


def kernel(x, c, ctx, c_ctx, ada_w, ada_b, norm_g, ffn_w1, ffn_w3, ffn_w2, in_w, in_b, a_mu, a_w0, a_wup, a_a0, a_aup, a_gup, a_kk, a_ka, a_rk, a_ln_g, a_ln_b, b_ws, b_bs, b_ln_g, b_ln_b, c_conv_w, c_conv_b, c_gate_b, c_ln_g, d_aup, d_ab, d_ln_g, br_w, out_w, final_g):
    weights = dict(a_mu=a_mu, a_w0=a_w0, a_wup=a_wup, a_a0=a_a0, a_aup=a_aup, a_gup=a_gup, a_kk=a_kk, a_ka=a_ka,
                   a_rk=a_rk.reshape(a_rk.shape[0], -1), a_ln_g=a_ln_g, a_ln_b=a_ln_b, b_ws=b_ws, b_bs=b_bs,
                   b_ln_g=b_ln_g, b_ln_b=b_ln_b, c_conv_w=c_conv_w, c_conv_b=c_conv_b, c_gate_b=c_gate_b,
                   c_ln_g=c_ln_g, d_aup=d_aup, d_ab=d_ab, d_ln_g=d_ln_g)
    nb, t_lat, d_model = x.shape
    t_ctx = ctx.shape[1]
    depth = ada_w.shape[0]
    wb = d_model // 4
    d_ff = ffn_w1.shape[-1]
    n_lat = nb * t_lat
    n_ctx = nb * t_ctx
    rows_all = n_lat + n_ctx
    assert t_lat % (GRID_W * 2) == 0 and t_ctx % B_CHUNK == 0 and n_lat % t_ctx == 0
    assert wb == 512 and a_wup.shape[2] == 64 and a_aup.shape[2] == 64 and a_gup.shape[1] == 128

    tm = _pick_tile(t_lat, n_ctx)

    def grp(i):
        return jnp.where(i < n_lat // tm, 1 + i // (t_lat // tm), 0)

    tm_s = _pick_tile(t_lat, n_ctx, 512)
    tf = _pick_cols(d_ff, (512, 256, 128))

    h = jnp.concatenate([x.reshape(n_lat, d_model), ctx.reshape(n_ctx, d_model)], axis=0)

    m_pad = -(-(nb + 1) // 8) * 8
    cpad = jnp.zeros((m_pad, d_model), F32).at[0].set(c_ctx).at[1:nb + 1].set(c)
    mod_all = _ada_mod(cpad, ada_w, ada_b, 128)
    mod_all = mod_all.reshape(depth, m_pad, N_MOD, 1, d_model)

    for l in range(depth):
        modp = mod_all[l]
        last = l == depth - 1

        hn = _norm_mod(h, norm_g[l, 0], modp, 0, rows_all, tm_s, lambda i: jnp.where(
            i < n_lat // tm_s, 1 + i // (t_lat // tm_s), 0))
        gact = _ffn_up(hn, ffn_w1, ffn_w3, l, 0, tm, tf)
        h = _ffn_down(gact, ffn_w2[l, 0].astype(BF16), h, modp, 2, tm, 512, grp)

        hn = _norm_mod(h, norm_g[l, 1], modp, 3, rows_all, tm_s, lambda i: jnp.where(
            i < n_lat // tm_s, 1 + i // (t_lat // tm_s), 0))
        w_d, w_gate = _split_proj(in_w[l], d_model)
        b_d, b_gate = _split_proj(in_b[l], d_model)
        n_abc = _proj_cols(d_model)["abc_end"]
        rows = n_lat if last else rows_all
        w_abc, w_d, w_gate = lax.optimization_barrier((in_w[l, :, 0:n_abc], w_d, w_gate))
        p = _in_proj(hn, w_abc, in_b[l, 0:n_abc].reshape(1, -1), rows_all, tm, 1280)
        pd = _in_proj(hn, w_d.astype(BF16), b_d.reshape(1, -1), rows_all, tm, w_d.shape[-1])
        gates = _in_proj(hn, w_gate.astype(BF16), b_gate.reshape(1, -1), rows, tm, 1024, gates=True)

        hs = _mixers(p, pd, l, nb, t_lat, t_ctx, rows, weights)
        h = _merge_out(hs, gates, br_w.astype(BF16), out_w.astype(BF16), h, modp, 5, l, rows, tm, 512, grp)

        hn = _norm_mod(h, norm_g[l, 2], modp, 6, rows, tm_s, lambda i: jnp.where(
            i < n_lat // tm_s, 1 + i // (t_lat // tm_s), 0))
        gact = _ffn_up(hn, ffn_w1, ffn_w3, l, 1, tm, tf)
        h = _ffn_down(gact, ffn_w2[l, 1].astype(BF16), h, modp, 8, tm, 512, grp)

    out = _final_norm(h, final_g, tm_s)
    return out.reshape(nb, t_lat, d_model)
```

```python
import functools

import jax
import jax.numpy as jnp
from jax import lax
from jax.experimental import pallas as pl
from jax.experimental.pallas import tpu as pltpu

F32 = jnp.float32
BF16 = jnp.bfloat16
HIGHEST = lax.Precision.HIGHEST

EPS = 1e-6
GRID_W = 64
N_MOD = 9
CHUNK = 64
A_HD = 64
A_LN_EPS = 64e-5
B_CHUNK = 128
C_HD = 128
C_CHUNK = 128
C_LN_EPS = 1e-5
D_DK = 64
D_DV = 128
D_TAU = 16.0
D_SUB = 16
D_LN_EPS = 1e-6
VMEM_LIMIT = 56 * 1024 * 1024


def _cp(*sem):
    return pltpu.CompilerParams(dimension_semantics=sem, vmem_limit_bytes=VMEM_LIMIT)


def _operands(a, b, precision):
    if precision is None:
        return a.astype(BF16), b.astype(BF16)
    return a, b


def _mm(a, b, precision=None):
    a, b = _operands(a, b, precision)
    return jnp.dot(a, b, precision=precision, preferred_element_type=F32)


def _nt(a, b, precision=None):
    a, b = _operands(a, b, precision)
    return lax.dot_general(a, b, (((1,), (1,)), ((), ())), precision=precision,
                           preferred_element_type=F32)


def _tn(a, b, precision=None):
    a, b = _operands(a, b, precision)
    return lax.dot_general(a, b, (((0,), (0,)), ((), ())), precision=precision,
                           preferred_element_type=F32)


def _log_sigmoid(x):
    return jnp.minimum(x, 0.0) - jnp.log(1.0 + jnp.exp(-jnp.abs(x)))


def _seg_sum(x, ones_blockdiag):
    hi = x.astype(BF16)
    lo = (x - hi.astype(F32)).astype(BF16)
    return _mm(hi, ones_blockdiag) + _mm(lo, ones_blockdiag)


def _tri(n, rev, strict):
    row = lax.broadcasted_iota(jnp.int32, (n, n), 0)
    col = lax.broadcasted_iota(jnp.int32, (n, n), 1)
    if rev:
        return (col > row) if strict else (col >= row)
    return (col < row) if strict else (col <= row)


def _norm_mod_kernel(h_ref, g_ref, sh_ref, sc_ref, o_ref):
    x = h_ref[...]
    y = x * lax.rsqrt(jnp.mean(x * x, axis=-1, keepdims=True) + EPS) * g_ref[...]
    o_ref[...] = (y * (1.0 + sc_ref[...]) + sh_ref[...]).astype(o_ref.dtype)


def _norm_mod(h, g, modp, k_shift, rows, tm, grp):
    d = h.shape[1]
    return pl.pallas_call(
        _norm_mod_kernel,
        grid=(rows // tm,),
        in_specs=[pl.BlockSpec((tm, d), lambda i: (i, 0)),
                  pl.BlockSpec((1, d), lambda i: (0, 0)),
                  pl.BlockSpec((None, None, 1, d), lambda i: (grp(i), k_shift, 0, 0)),
                  pl.BlockSpec((None, None, 1, d), lambda i: (grp(i), k_shift + 1, 0, 0))],
        out_specs=pl.BlockSpec((tm, d), lambda i: (i, 0)),
        out_shape=jax.ShapeDtypeStruct((rows, d), BF16),
        compiler_params=_cp("arbitrary"),
        name="norm_mod",
    )(h, g.reshape(1, d), modp, modp)


def _final_norm_kernel(h_ref, g_ref, o_ref):
    x = h_ref[...]
    o_ref[...] = x * lax.rsqrt(jnp.mean(x * x, axis=-1, keepdims=True) + EPS) * g_ref[...]


def _final_norm(h, g, tm):
    rows, d = h.shape
    return pl.pallas_call(
        _final_norm_kernel,
        grid=(rows // tm,),
        in_specs=[pl.BlockSpec((tm, d), lambda i: (i, 0)), pl.BlockSpec((1, d), lambda i: (0, 0))],
        out_specs=pl.BlockSpec((tm, d), lambda i: (i, 0)),
        out_shape=jax.ShapeDtypeStruct((rows, d), F32),
        compiler_params=_cp("arbitrary"),
        name="final_norm",
    )(h, g.reshape(1, d))


def _ada_kernel(c_ref, wa_ref, wb_ref, b_ref, o_ref):
    @pl.when(pl.program_id(1) == 0)
    def _():
        o_ref[...] = jnp.broadcast_to(b_ref[...], o_ref.shape)

    c = c_ref[...]
    cond = (c * jax.nn.sigmoid(c)).astype(BF16)
    half = wa_ref.shape[1]
    o_ref[:, 0:half] += _mm(cond, wa_ref[...].astype(BF16))
    o_ref[:, half:] += _mm(cond, wb_ref[...].astype(BF16))


def _ada_mod(cpad, ada_w, ada_b, tk):
    depth, d, n = ada_w.shape
    m = cpad.shape[0]
    return pl.pallas_call(
        _ada_kernel,
        grid=(depth, d // tk),
        in_specs=[pl.BlockSpec((m, tk), lambda l, k: (0, k)),
                  pl.BlockSpec((None, tk, n // 2), lambda l, k: (l, k, 0)),
                  pl.BlockSpec((None, tk, n // 2), lambda l, k: (l, k, 1)),
                  pl.BlockSpec((None, 1, n), lambda l, k: (l, 0, 0))],
        out_specs=pl.BlockSpec((None, m, n), lambda l, k: (l, 0, 0)),
        out_shape=jax.ShapeDtypeStruct((depth, m, n), F32),
        compiler_params=_cp("arbitrary", "arbitrary"),
        name="ada_mod",
    )(cpad, ada_w, ada_w, ada_b.reshape(depth, 1, n))


def _ffn_up_kernel(x_ref, w1_ref, w3_ref, o_ref, w1b, w3b):
    @pl.when(pl.program_id(1) == 0)
    def _():
        w1b[...] = w1_ref[...].astype(BF16)
        w3b[...] = w3_ref[...].astype(BF16)

    x = x_ref[...]
    a = _mm(x, w1b[...])
    b = _mm(x, w3b[...])
    o_ref[...] = (a * jax.nn.sigmoid(a) * b).astype(o_ref.dtype)


def _ffn_up(x, w1, w3, l, s, tm, tf):
    rows, d = x.shape
    f = w1.shape[-1]
    wspec = pl.BlockSpec((None, None, d, tf), lambda j, i: (l, s, 0, j))
    return pl.pallas_call(
        _ffn_up_kernel,
        grid=(f // tf, rows // tm),
        in_specs=[pl.BlockSpec((tm, d), lambda j, i: (i, 0)), wspec, wspec],
        out_specs=pl.BlockSpec((tm, tf), lambda j, i: (i, j)),
        out_shape=jax.ShapeDtypeStruct((rows, f), BF16),
        scratch_shapes=[pltpu.VMEM((d, tf), BF16), pltpu.VMEM((d, tf), BF16)],
        compiler_params=_cp("arbitrary", "arbitrary"),
        name="ffn_up",
    )(x, w1, w3)


def _ffn_down_kernel(g_ref, w_ref, h_ref, gate_ref, o_ref):
    acc = _mm(g_ref[...], w_ref[...])
    o_ref[...] = h_ref[...] + (0.5 * acc) * gate_ref[...]


def _ffn_down(gact, w2, h, modp, k_gate, l, s, tm, tn, grp):
    rows, f = gact.shape
    d = h.shape[1]
    return pl.pallas_call(
        _ffn_down_kernel,
        grid=(rows // tm, d // tn),
        in_specs=[pl.BlockSpec((tm, f), lambda i, j: (i, 0)),
                  pl.BlockSpec((None, None, f, tn), lambda i, j: (l, s, 0, j)),
                  pl.BlockSpec((tm, tn), lambda i, j: (i, j)),
                  pl.BlockSpec((None, None, 1, tn), lambda i, j: (grp(i), k_gate, 0, j))],
        out_specs=pl.BlockSpec((tm, tn), lambda i, j: (i, j)),
        out_shape=jax.ShapeDtypeStruct((rows, d), F32),
        compiler_params=_cp("arbitrary", "arbitrary"),
        name="ffn_down",
    )(gact, w2, h, modp)


def _in_proj_kernel(x_ref, w_ref, b_ref, o_ref):
    o_ref[...] = _mm(x_ref[...], w_ref[...]) + b_ref[...]


def _gate_proj_kernel(x_ref, w_ref, b_ref, o_ref):
    o_ref[...] = jax.nn.sigmoid(_mm(x_ref[...], w_ref[...]) + b_ref[...]).astype(o_ref.dtype)


def _in_proj(x, w, b, rows, tm, tn, gates=False, layer=None):
    d = x.shape[1]
    n = b.shape[1]
    if layer is None:
        wspec = pl.BlockSpec((d, tn), lambda i, j: (0, j))
    else:
        wspec = pl.BlockSpec((None, d, tn), lambda i, j: (layer, 0, j))
    return pl.pallas_call(
        _gate_proj_kernel if gates else _in_proj_kernel,
        grid=(rows // tm, n // tn),
        in_specs=[pl.BlockSpec((tm, d), lambda i, j: (i, 0)), wspec,
                  pl.BlockSpec((1, tn), lambda i, j: (0, j))],
        out_specs=pl.BlockSpec((tm, tn), lambda i, j: (i, j)),
        out_shape=jax.ShapeDtypeStruct((rows, n), BF16 if gates else F32),
        compiler_params=_cp("arbitrary", "arbitrary"),
        name="gate_proj" if gates else "in_proj",
    )(x, w, b)


def _merge_out_kernel(ha, hb, hc, hd, ga, gb, gc, gd, wa, wb, wc, wd, ow_ref, h_ref, gate_ref, o_ref, y_scr, *, nj):
    j = pl.program_id(1)

    @pl.when(j < nj)
    def _():
        y = ga[...].astype(F32) * _mm(ha[...], wa[...])
        y = y + gb[...].astype(F32) * _mm(hb[...], wb[...])
        y = y + gc[...].astype(F32) * _mm(hc[...], wc[...])
        y = y + gd[...].astype(F32) * _mm(hd[...], wd[...])
        y_scr[j] = y.astype(y_scr.dtype)

    @pl.when(j >= nj)
    def _():
        tk = y_scr.shape[2]
        w = ow_ref[...]
        acc = _mm(y_scr[0], w[0:tk])
        for k in range(1, nj):
            acc = acc + _mm(y_scr[k], w[k * tk:(k + 1) * tk])
        o_ref[...] = h_ref[...] + acc * gate_ref[...]


def _merge_out(hs, gates, br_w, out_w, h, modp, k_gate, l, rows, tm, tn, grp):
    wbr = hs[0].shape[1]
    d = br_w.shape[-1]
    nj = d // tn
    mj = lambda j: jnp.minimum(j, nj - 1)
    oj = lambda j: jnp.maximum(j - nj, 0)
    hspec = pl.BlockSpec((tm, wbr), lambda i, j: (i, 0))
    gspecs = [pl.BlockSpec((tm, tn), functools.partial(lambda i, j, n: (i, n * nj + mj(j)), n=n))
              for n in range(4)]
    wspecs = [pl.BlockSpec((None, None, wbr, tn), functools.partial(lambda i, j, n: (l, n, 0, mj(j)), n=n))
              for n in range(4)]
    return pl.pallas_call(
        functools.partial(_merge_out_kernel, nj=nj),
        grid=(rows // tm, 2 * nj),
        in_specs=[hspec] * 4 + gspecs + wspecs
                 + [pl.BlockSpec((None, d, tn), lambda i, j: (l, 0, oj(j))),
                    pl.BlockSpec((tm, tn), lambda i, j: (i, oj(j))),
                    pl.BlockSpec((None, None, 1, tn), lambda i, j: (grp(i), k_gate, 0, oj(j)))],
        out_specs=pl.BlockSpec((tm, tn), lambda i, j: (i, oj(j))),
        out_shape=jax.ShapeDtypeStruct((rows, d), F32),
        scratch_shapes=[pltpu.VMEM((nj, tm, tn), BF16)],
        compiler_params=_cp("arbitrary", "arbitrary"),
        name="merge_out",
    )(*hs, gates, gates, gates, gates, br_w, br_w, br_w, br_w, out_w, h, modp)


def _tile_place(geom):
    n_lat_tiles, lat_tiles, ctx_tiles = geom
    i = pl.program_id(0)
    is_ctx = i >= n_lat_tiles
    per_seq = jnp.where(is_ctx, ctx_tiles, lat_tiles)
    pos = lax.rem(jnp.where(is_ctx, i - n_lat_tiles, i), per_seq)
    return is_ctx, pos == 0, pos == per_seq - 1


def _row_neighbours(xp_ref, x, xn_ref, first, last):
    ts = x.shape[0]
    hp = xp_ref.shape[0]
    t = lax.broadcasted_iota(jnp.int32, x.shape, 0)
    prv = jnp.where(t == 0, jnp.where(first, 0.0, xp_ref[hp - 1:hp, :]), pltpu.roll(x, 1, 0))
    nxt = jnp.where(t == ts - 1, jnp.where(last, 0.0, xn_ref[0:1, :]), pltpu.roll(x, ts - 1, 0))
    return prv, nxt


def _token_shift(xp_ref, x_ref, xn_ref, mu_ref, geom):
    is_ctx, first, last = _tile_place(geom)
    x = x_ref[...]
    ts = x.shape[0]
    t = lax.broadcasted_iota(jnp.int32, x.shape, 0)
    lane = lax.broadcasted_iota(jnp.int32, x.shape, 1)
    prv, nxt = _row_neighbours(xp_ref, x, xn_ref, first, last)
    sh_ctx = jnp.where((lane & 1) == 0, prv, nxt)
    tw = t & (GRID_W - 1)
    left = jnp.where(tw == 0, 0.0, prv)
    right = jnp.where(tw == GRID_W - 1, 0.0, nxt)
    up = jnp.concatenate([jnp.where(first, 0.0, xp_ref[...]), x[0:ts - GRID_W]], axis=0)
    down = jnp.concatenate([x[GRID_W:ts], jnp.where(last, 0.0, xn_ref[...])], axis=0)
    c4 = lane & 3
    sh_lat = jnp.where(c4 == 0, left, jnp.where(c4 == 1, right, jnp.where(c4 == 2, up, down)))
    sh = jnp.where(is_ctx, sh_ctx, sh_lat)
    return x + (sh - x) * mu_ref[...]


def _conv_kernel(xp_ref, x_ref, xn_ref, w_ref, b_ref, s_ref, o_ref, *, geom):
    _, first, last = _tile_place(geom)
    x = x_ref[...]
    prv, nxt = _row_neighbours(xp_ref, x, xn_ref, first, last)
    w = w_ref[...]
    y = prv * w[0:1] + x * w[1:2] + nxt * w[2:3] + b_ref[...]
    o_ref[...] = y * jax.nn.sigmoid(y) * s_ref[...]


def _sequence_tiles(body, p, col0, width, halo, small, out_widths, nb, t_lat, t_ctx, name):
    rows = p.shape[0]
    ts = min(256, t_ctx)
    assert t_lat % ts == 0 and t_ctx % ts == 0 and ts >= 2 * GRID_W and ts % halo == 0 and col0 % 128 == 0
    nt = rows // ts
    nh = rows // halo
    r = ts // halo
    geom = (nb * t_lat // ts, t_lat // ts, t_ctx // ts)
    full = lambda a: pl.BlockSpec(a.shape, lambda i: (0,) * a.ndim)
    window = lambda n: (pl.Element(n), pl.Element(width))
    return pl.pallas_call(
        functools.partial(body, geom=geom),
        grid=(nt,),
        in_specs=[pl.BlockSpec(window(halo), lambda i: (jnp.maximum(i * r - 1, 0) * halo, col0)),
                  pl.BlockSpec(window(ts), lambda i: (i * ts, col0)),
                  pl.BlockSpec(window(halo), lambda i: (jnp.minimum((i + 1) * r, nh - 1) * halo, col0))]
                 + [full(a) for a in small],
        out_specs=[pl.BlockSpec((ts, ow), lambda i: (i, 0)) for ow in out_widths],
        out_shape=[jax.ShapeDtypeStruct((rows, ow), F32) for ow in out_widths],
        compiler_params=_cp("arbitrary"),
        name=name,
    )(p, p, p, *small)


def _rwkv_pre_kernel(xp_ref, x_ref, xn_ref, mu_ref, kk_ref, ka_ref, rk_ref, w0_ref, a0_ref, wup_ref, aup_ref,
                     gup_ref, ones_ref, r_o, v_o, kk_o, lwf_o, lwb_o, ktf_o, ktb_o, bf_o, bb_o, bonus_o, g_o, *,
                     geom):
    wb = r_o.shape[1]
    za = _token_shift(xp_ref, x_ref, xn_ref, mu_ref, geom)
    r = za[:, 0:wb]
    k = za[:, wb:2 * wb]
    v = za[:, 2 * wb:3 * wb]
    wd = jnp.tanh(za[:, 3 * wb:3 * wb + 128])
    ad = za[:, 3 * wb + 128:3 * wb + 256]
    gd = jax.nn.sigmoid(za[:, 3 * wb + 256:3 * wb + 384])
    ones = ones_ref[...]
    kq = k * kk_ref[...]
    kk = kq * lax.rsqrt(jnp.maximum(_seg_sum(kq * kq, ones), 1e-24))
    r_o[...] = r
    v_o[...] = v
    kk_o[...] = kk
    ka = ka_ref[...]
    for d, (lw_o, kt_o, b_o) in enumerate(((lwf_o, ktf_o, bf_o), (lwb_o, ktb_o, bb_o))):
        xw = w0_ref[d:d + 1] + _mm(wd, wup_ref[d])
        lw_o[...] = -jax.nn.sigmoid(xw) * 0.6065306597126334
        a = jax.nn.sigmoid(a0_ref[d:d + 1] + _mm(ad, aup_ref[d]))
        kt_o[...] = k * (1.0 + (a - 1.0) * ka)
        b_o[...] = kk * a
    bonus_o[...] = _seg_sum(r * k * rk_ref[...], ones) * v
    g_o[...] = _mm(gd, gup_ref[...])


def _scan_geometry(nb, t_lat, t_ctx, chunk):
    nlc = t_lat // chunk
    ncc = t_ctx // chunk
    nch = nlc + ncc

    def fwd(j):
        return jnp.where(j < ncc, nlc + j, j - ncc)

    def rev(j):
        return nch - 1 - j

    def row(b, c):
        return jnp.where(c < nlc, b * nlc + c, nb * nlc + b * ncc + (c - nlc))

    return fwd, rev, row, nch


def _scan_in_specs(nb, row, cm, chunk, width, col):
    return [pl.BlockSpec((pl.Element(chunk), pl.Element(width)),
                         functools.partial(lambda j, b: (row(b, cm(j)) * chunk, col), b=b)) for b in range(nb)]


def _scan_out_spec(nb, cm, chunk, width):
    return pl.BlockSpec((None, nb, chunk, width), lambda j: (cm(j), 0, 0, 0))


def _rwkv_scan_kernel(*refs, nb):
    ins = refs[:12 * nb]
    yf, yb, s_ref = refs[12 * nb:]

    @pl.when(pl.program_id(0) == 0)
    def _():
        s_ref[...] = jnp.zeros_like(s_ref)

    n, wbw = ins[0].shape
    nh = wbw // A_HD
    heads = []
    for d, y_r in enumerate((yf, yb)):
        rev = d == 1
        arr = lambda a, b: ins[(d * 6 + a) * nb + b][...]
        lws = [arr(3, b) for b in range(nb)]
        c_all = _mm(_tri(n, rev, False).astype(F32), jnp.concatenate(lws, axis=1), HIGHEST)
        strict = _tri(n, rev, True)
        incl = _tri(n, rev, False)
        for b in range(nb):
            lw = lws[b]
            c = c_all[:, b * wbw:(b + 1) * wbw]
            eg = jnp.exp(c)
            ieg = jnp.exp(-c)
            r_all = arr(0, b) * eg
            kk_all = arr(2, b) * jnp.exp(c - lw)
            kt_all = arr(4, b) * ieg
            b_all = arr(5, b) * ieg
            v_all = arr(1, b)
            g_last = eg[0:1] if rev else eg[n - 1:n]
            for h in range(nh):
                sl = slice(h * A_HD, (h + 1) * A_HD)
                idx = (d * nb + b) * nh + h
                heads.append(dict(r=r_all[:, sl], kk=kk_all[:, sl], b=b_all[:, sl], kt=kt_all[:, sl],
                                  v=v_all[:, sl], s=s_ref[idx], g=g_last[:, sl], strict=strict, incl=incl,
                                  y_ref=y_r.at[b], sl=sl, idx=idx))
    for t in heads:
        t["z"] = _nt(jnp.concatenate([t["kk"], t["r"]], axis=0), jnp.concatenate([t["kt"], t["b"], t["s"]], axis=0))
    for t in heads:
        z = t["z"]
        t["a_kv"] = jnp.where(t["strict"], z[0:n, 0:n], 0.0)
        t["a_kb"] = jnp.where(t["strict"], z[0:n, n:2 * n], 0.0)
        t["r_kv"] = jnp.where(t["incl"], z[n:2 * n, 0:n], 0.0)
        t["r_kb"] = jnp.where(t["incl"], z[n:2 * n, n:2 * n], 0.0)
    for t in heads:
        t["av"] = _mm(jnp.concatenate([t["a_kv"], t["r_kv"]], axis=0), t["v"])
    left = lax.broadcasted_iota(jnp.int32, (n, 2 * n), 1) < n
    for t in heads:
        t["w"] = jnp.concatenate([t["a_kb"], t["z"][0:n, 2 * n:] + t["av"][0:n]], axis=1)
    for t in heads:
        r = _mm(t["a_kb"], t["w"])
        t["w"] = jnp.where(left, r, t["w"] - r)
    m = 2
    while m < n:
        for t in heads:
            r = _mm(t["w"][:, 0:n], t["w"])
            t["w"] = jnp.where(left, r, t["w"] + r)
        m *= 2
    for t in heads:
        t["u"] = t["w"][:, n:2 * n]
        y = t["z"][n:2 * n, 2 * n:] + t["av"][n:2 * n] - _mm(t["r_kb"], t["u"])
        t["y_ref"][:, t["sl"]] = y
        s_new = t["s"] + _tn(jnp.concatenate([t["v"], t["u"]], axis=0), jnp.concatenate([t["kt"], -t["b"]], axis=0))
        s_ref[t["idx"]] = s_new * t["g"]


def _rwkv_scan(r, v, kk, lwf, lwb, ktf, ktb, bf, bb, nb, t_lat, t_ctx):
    wb = r.shape[1]
    fwd, rev, row, nch = _scan_geometry(nb, t_lat, t_ctx, CHUNK)
    in_specs = [s for cm in (fwd, rev) for _ in range(6) for s in _scan_in_specs(nb, row, cm, CHUNK, wb, 0)]
    args = [a for grp in ((r, v, kk, lwf, ktf, bf), (r, v, kk, lwb, ktb, bb)) for a in grp for _ in range(nb)]
    return pl.pallas_call(
        functools.partial(_rwkv_scan_kernel, nb=nb),
        grid=(nch,),
        in_specs=in_specs,
        out_specs=[_scan_out_spec(nb, fwd, CHUNK, wb), _scan_out_spec(nb, rev, CHUNK, wb)],
        out_shape=[jax.ShapeDtypeStruct((nch, nb, CHUNK, wb), F32)] * 2,
        scratch_shapes=[pltpu.VMEM((2 * nb * (wb // A_HD), A_HD, A_HD), F32)],
        compiler_params=_cp("arbitrary"),
        name="rwkv_scan",
    )(*args)


def _mlstm_scan_kernel(*refs, nb):
    ins = refs[:8 * nb]
    gbias_ref, hf, hb, c_ref, n_ref, m_ref = refs[8 * nb:]

    @pl.when(pl.program_id(0) == 0)
    def _():
        c_ref[...] = jnp.zeros_like(c_ref)
        n_ref[...] = jnp.zeros_like(n_ref)
        m_ref[...] = jnp.zeros_like(m_ref)

    n = ins[0].shape[0]
    nh = ins[0].shape[1] // C_HD
    heads = []
    for d, h_r in enumerate((hf, hb)):
        rev = d == 1
        arr = lambda a, b: ins[(d * 4 + a) * nb + b][...]
        gates = [arr(3, b) + gbias_ref[...] for b in range(nb)]
        fgs = [_log_sigmoid(g) for g in gates]
        mask = _tri(n, rev, False)
        mi = mask.astype(F32)
        bcol_all = _mm(mi, jnp.concatenate(fgs, axis=1), HIGHEST)
        brow_all = _nt(jnp.concatenate([f.T for f in fgs], axis=0), mi, HIGHEST)
        for b in range(nb):
            bcol = bcol_all[:, b * 128:(b + 1) * 128]
            brow = brow_all[b * 128:(b + 1) * 128]
            gates_t = gates[b].T
            q_all, k_all, v_all = arr(0, b), arr(1, b), arr(2, b)
            for h in range(nh):
                sl = slice(h * C_HD, (h + 1) * C_HD)
                ii = d * 2 * nh + h
                fi = ii + nh
                idx = (d * nb + b) * nh + h
                b_c = bcol[:, fi:fi + 1]
                heads.append(dict(q=q_all[:, sl], k=k_all[:, sl], v=v_all[:, sl], b_c=b_c, b_r=brow[fi:fi + 1, :],
                                  i_c=gates[b][:, ii:ii + 1], i_r=gates_t[ii:ii + 1, :], m=m_ref[idx][:, 0:1],
                                  cm=c_ref[idx], nn=n_ref[idx], mask=mask,
                                  b_l=b_c[0:1] if rev else b_c[n - 1:n], idx=idx, h_ref=h_r.at[b], sl=sl))
    for t in heads:
        t["qk"] = _nt(t["q"], t["k"])
        t["qc"] = _nt(t["q"], t["cm"])
    for t in heads:
        t["dlog"] = jnp.where(t["mask"], t["b_c"] - t["b_r"] + t["i_r"], -jnp.inf)
        t["gl"] = t["b_l"] - t["b_c"] + t["i_c"]
    for t in heads:
        t["dmax"] = jnp.max(t["dlog"], axis=1, keepdims=True)
        t["gmax"] = jnp.max(t["gl"], axis=0, keepdims=True)
    for t in heads:
        inter = t["b_c"] + t["m"]
        m_t = jnp.maximum(inter, t["dmax"])
        t["iw"] = jnp.exp(inter - m_t)
        t["m_t"] = m_t
        t["s"] = t["qk"] * jnp.exp(t["dlog"] - m_t)
        m_new = jnp.maximum(t["b_l"] + t["m"], t["gmax"])
        t["sw"] = jnp.exp(t["gl"] - m_new)
        t["dec"] = jnp.exp(t["b_l"] + t["m"] - m_new)
        t["m_new"] = m_new
    for t in heads:
        t["sv"] = _mm(t["s"], t["v"])
        t["vk"] = _tn(t["v"] * t["sw"], t["k"])
    for t in heads:
        t["rs"] = jnp.sum(t["s"], axis=1, keepdims=True)
        t["qn"] = jnp.sum(t["q"] * t["nn"], axis=1, keepdims=True)
        t["ks"] = jnp.sum(t["sw"] * t["k"], axis=0, keepdims=True)
    for t in heads:
        iw = t["iw"]
        num = t["sv"] + iw * t["qc"]
        den = jnp.maximum(jnp.abs(t["rs"] + iw * t["qn"]), jnp.exp(-t["m_t"]))
        t["h_ref"][:, t["sl"]] = num / den
        idx = t["idx"]
        c_ref[idx] = t["dec"] * t["cm"] + t["vk"]
        n_ref[idx] = t["dec"] * t["nn"] + t["ks"]
        m_ref[idx] = jnp.broadcast_to(t["m_new"], m_ref.shape[1:])


def _mlstm_scan(qk, p, gbias, col_v, col_g, nb, t_lat, t_ctx, wb):
    ck = C_CHUNK
    fwd, rev, row, nch = _scan_geometry(nb, t_lat, t_ctx, ck)
    nh = wb // C_HD

    def specs(cm):
        return (_scan_in_specs(nb, row, cm, ck, wb, 0) + _scan_in_specs(nb, row, cm, ck, wb, wb)
                + _scan_in_specs(nb, row, cm, ck, wb, col_v) + _scan_in_specs(nb, row, cm, ck, 128, col_g))

    args = ([qk] * nb + [qk] * nb + [p] * nb + [p] * nb) * 2
    return pl.pallas_call(
        functools.partial(_mlstm_scan_kernel, nb=nb),
        grid=(nch,),
        in_specs=specs(fwd) + specs(rev) + [pl.BlockSpec((1, 128), lambda j: (0, 0))],
        out_specs=[_scan_out_spec(nb, fwd, ck, wb), _scan_out_spec(nb, rev, ck, wb)],
        out_shape=[jax.ShapeDtypeStruct((nch, nb, ck, wb), F32)] * 2,
        scratch_shapes=[pltpu.VMEM((2 * nb * nh, C_HD, C_HD), F32), pltpu.VMEM((2 * nb * nh, 1, C_HD), F32),
                        pltpu.VMEM((2 * nb * nh, 1, 128), F32)],
        compiler_params=_cp("arbitrary"),
        name="mlstm_scan",
    )(*args, gbias)


def _gla_scan_kernel(*refs, nb):
    ins = refs[:8 * nb]
    aup_ref, abias_ref, of, ob, s_ref = refs[8 * nb:]

    @pl.when(pl.program_id(0) == 0)
    def _():
        s_ref[...] = jnp.zeros_like(s_ref)

    n, wkw = ins[0].shape
    nh = wkw // D_DK
    nsub = n // D_SUB
    pw = 2 * D_DK
    row = lax.broadcasted_iota(jnp.int32, (n, pw), 0)
    colx = lax.broadcasted_iota(jnp.int32, (n, pw), 1) & (D_DK - 1)
    rel = colx - (row & -D_SUB)
    rin = row & (D_SUB - 1)
    same_head = ((lax.broadcasted_iota(jnp.int32, (pw, pw), 0) & D_DK)
                 == (lax.broadcasted_iota(jnp.int32, (pw, pw), 1) & D_DK))
    ones_pair = jnp.where(same_head, 1.0, 0.0).astype(BF16)
    heads, pairs = [], []
    for d, o_r in enumerate((of, ob)):
        rev = d == 1
        arr = lambda a, b: ins[(d * 4 + a) * nb + b][...]
        la_rows = _log_sigmoid(_mm(jnp.concatenate([arr(3, b) for b in range(nb)], axis=0), aup_ref[d])
                               + abias_ref[d]) * (1.0 / D_TAU)
        la_all = jnp.concatenate([la_rows[b * n:(b + 1) * n] for b in range(nb)], axis=1)
        bc_all = _mm(_tri(n, rev, False).astype(F32), la_all, HIGHEST)
        for b in range(nb):
            bc = bc_all[:, b * wkw:(b + 1) * wkw]
            q_all = arr(0, b) * (D_DK ** -0.5)
            k_all = arr(1, b)
            v_all = arr(2, b)
            base = (d * nb + b) * nh
            for h in range(nh):
                ksl = slice(h * D_DK, (h + 1) * D_DK)
                vsl = slice(h * D_DV, (h + 1) * D_DV)
                heads.append(dict(q=q_all[:, ksl], k=k_all[:, ksl], v=v_all[:, vsl], bc=bc[:, ksl], rev=rev,
                                  s=s_ref[base + h], idx=base + h, o_ref=o_r.at[b], vsl=vsl))
            for p in range(nh // 2):
                psl = slice(p * pw, (p + 1) * pw)
                pairs.append(dict(q=q_all[:, psl], k=k_all[:, psl], bc=bc[:, psl], rev=rev,
                                  heads=(base + 2 * p, base + 2 * p + 1)))
    for t in heads:
        t["o"] = _nt(t["q"] * jnp.exp(t["bc"]), t["s"])
    for t in heads:
        q, k, bc, rev = t["q"], t["k"], t["bc"], t["rev"]
        pieces = []
        for blk in range(nsub):
            r0 = blk * D_SUB
            edge = bc[r0 + D_SUB - 1:r0 + D_SUB] if rev else bc[r0:r0 + 1]
            has_other = blk < nsub - 1 if rev else blk > 0
            if has_other:
                qs = q[r0:r0 + D_SUB] * jnp.exp(bc[r0:r0 + D_SUB] - edge)
                ks = k * jnp.exp(jnp.minimum(edge - bc, 0.0))
                pieces.append(_nt(qs, ks))
            else:
                pieces.append(jnp.zeros((D_SUB, n), F32))
        t["sc"] = jnp.concatenate(pieces, axis=0)
    for t in pairs:
        q, k, bc = t["q"], t["k"], t["bc"]
        es = []
        for j in range(D_SUB):
            kj = jnp.concatenate([jnp.broadcast_to(k[b * D_SUB + j:b * D_SUB + j + 1], (D_SUB, pw))
                                  for b in range(nsub)], axis=0)
            bj = jnp.concatenate([jnp.broadcast_to(bc[b * D_SUB + j:b * D_SUB + j + 1], (D_SUB, pw))
                                  for b in range(nsub)], axis=0)
            es.append(q * kj * jnp.exp(bc - bj))
        t["e"] = jnp.concatenate(es, axis=0)
    for t in pairs:
        t["c"] = _seg_sum(t["e"], ones_pair)
    for t in pairs:
        rev = t["rev"]
        sc = jnp.concatenate([heads[t["heads"][0]]["sc"], heads[t["heads"][1]]["sc"]], axis=1)
        a = jnp.where((rel >= D_SUB) if rev else (rel < 0), sc, 0.0)
        for j in range(D_SUB):
            keep = (rel == j) & ((rin <= j) if rev else (rin >= j))
            a = jnp.where(keep, t["c"][j * n:(j + 1) * n], a)
        heads[t["heads"][0]]["a"] = a[:, 0:D_DK]
        heads[t["heads"][1]]["a"] = a[:, D_DK:pw]
    for t in heads:
        t["o_ref"][:, t["vsl"]] = t["o"] + _mm(t["a"], t["v"])
        bc = t["bc"]
        b_l = bc[0:1] if t["rev"] else bc[n - 1:n]
        s_ref[t["idx"]] = t["s"] * jnp.exp(b_l) + _tn(t["v"], t["k"] * jnp.exp(b_l - bc))


def _gla_scan(p, aupp, abias, col_q, col_k, col_v, col_a, nb, t_lat, t_ctx, wb):
    fwd, rev, row, nch = _scan_geometry(nb, t_lat, t_ctx, CHUNK)
    wk = aupp.shape[-1]
    nh = wk // D_DK

    def specs(cm):
        return (_scan_in_specs(nb, row, cm, CHUNK, wk, col_q) + _scan_in_specs(nb, row, cm, CHUNK, wk, col_k)
                + _scan_in_specs(nb, row, cm, CHUNK, wb, col_v) + _scan_in_specs(nb, row, cm, CHUNK, 128, col_a))

    return pl.pallas_call(
        functools.partial(_gla_scan_kernel, nb=nb),
        grid=(nch,),
        in_specs=specs(fwd) + specs(rev) + [pl.BlockSpec(aupp.shape, lambda j: (0, 0, 0)),
                                           pl.BlockSpec(abias.shape, lambda j: (0, 0, 0))],
        out_specs=[_scan_out_spec(nb, fwd, CHUNK, wb), _scan_out_spec(nb, rev, CHUNK, wb)],
        out_shape=[jax.ShapeDtypeStruct((nch, nb, CHUNK, wb), F32)] * 2,
        scratch_shapes=[pltpu.VMEM((2 * nb * nh, D_DV, D_DK), F32)],
        compiler_params=_cp("arbitrary"),
        name="gla_scan",
    )(*([p] * (8 * nb)), aupp, abias)


def _branch_out_kernel(ya_f, ya_b, bonus, ga, pb, hc_f, hc_b, oc, od_f, od_b, gdd,
                       a_lng, a_lnb, ones, b_ws, b_bias, b_lng, b_lnb, c_lng, d_lng,
                       hs_a, hs_b, hs_c, hs_d):
    wb = hs_a.shape[1]
    tm = hs_a.shape[0]
    y = ya_f[...].reshape(tm, wb) + ya_b[...].reshape(tm, wb)
    on = ones[...]
    mu = _seg_sum(y, on) * (1.0 / A_HD)
    yc = y - mu
    var = _seg_sum(yc * yc, on) * (1.0 / A_HD)
    yn = yc * lax.rsqrt(var + A_LN_EPS) * a_lng[...] + a_lnb[...]
    hs_a[...] = ((yn + bonus[...]) * ga[...]).astype(hs_a.dtype)
    z = jax.nn.gelu(pb[...])
    u = z[:, 0:wb]
    vv = z[:, wb:2 * wb]
    mu = jnp.mean(vv, axis=-1, keepdims=True)
    vc = vv - mu
    vn = (vc * lax.rsqrt(jnp.mean(vc * vc, axis=-1, keepdims=True) + 1e-5) * b_lng[...] + b_lnb[...])
    vn = vn.astype(BF16)
    for ck in range(u.shape[0] // B_CHUNK):
        rs = slice(ck * B_CHUNK, (ck + 1) * B_CHUNK)
        for g in range(wb // B_CHUNK):
            cs = slice(g * B_CHUNK, (g + 1) * B_CHUNK)
            s = _mm(b_ws[g].astype(BF16), vn[rs, cs]) + b_bias[:, cs]
            hs_b[rs, cs] = (u[rs, cs] * s).astype(hs_b.dtype)
    hc = hc_f[...].reshape(tm, wb) + hc_b[...].reshape(tm, wb)
    ogate = jax.nn.sigmoid(oc[...])
    gc = c_lng[...]
    for h in range(wb // C_HD):
        sl = slice(h * C_HD, (h + 1) * C_HD)
        x = hc[:, sl]
        xc = x - jnp.mean(x, axis=-1, keepdims=True)
        xn = xc * lax.rsqrt(jnp.mean(xc * xc, axis=-1, keepdims=True) + C_LN_EPS) * gc[:, sl]
        hs_c[:, sl] = (xn * ogate[:, sl]).astype(hs_c.dtype)
    od = od_f[...].reshape(tm, wb) + od_b[...].reshape(tm, wb)
    gg = gdd[...]
    gg = gg * jax.nn.sigmoid(gg)
    gd_ = d_lng[...]
    for h in range(wb // D_DV):
        sl = slice(h * D_DV, (h + 1) * D_DV)
        x = od[:, sl]
        xn = x * lax.rsqrt(jnp.mean(x * x, axis=-1, keepdims=True) + D_LN_EPS) * gd_[:, sl]
        hs_d[:, sl] = (xn * gg[:, sl]).astype(hs_d.dtype)


def _branch_out(ya_f, ya_b, bonus, ga, p, pd, hc_f, hc_b, od_f, od_b, prm, cols, rows, nb, t_lat, t_ctx):
    wb = bonus.shape[1]
    tm = min(256, t_ctx)
    n_lat_tiles, lat_tiles, ctx_tiles = nb * t_lat // tm, t_lat // tm, t_ctx // tm
    row = lambda w: pl.BlockSpec((tm, w), lambda i: (i, 0))
    cols_at = lambda w, c0: pl.BlockSpec((pl.Element(tm), pl.Element(w)), lambda i: (i * tm, c0))

    def scan_out(a):
        per_tile = tm // a.shape[2]

        def imap(i):
            ic = i - n_lat_tiles
            seq = jnp.where(i < n_lat_tiles, i // lat_tiles, ic // ctx_tiles)
            blk = jnp.where(i < n_lat_tiles, lax.rem(i, lat_tiles), lat_tiles + lax.rem(ic, ctx_tiles))
            return blk, seq, 0, 0

        return pl.BlockSpec((per_tile, None, a.shape[2], wb), imap)

    small = [prm["a_lng"], prm["a_lnb"], prm["ones"], prm["b_ws"], prm["b_bias"], prm["b_lng"], prm["b_lnb"],
             prm["c_lng"], prm["d_lng"]]
    full = lambda a: pl.BlockSpec(a.shape, lambda i: (0,) * a.ndim)
    return pl.pallas_call(
        _branch_out_kernel,
        grid=(rows // tm,),
        in_specs=[scan_out(ya_f), scan_out(ya_b), row(wb), row(wb), cols_at(2 * wb, cols["b"]),
                  scan_out(hc_f), scan_out(hc_b), cols_at(wb, cols["c_o"]), scan_out(od_f), scan_out(od_b),
                  cols_at(wb, cols["d_g"])] + [full(a) for a in small],
        out_specs=[row(wb)] * 4,
        out_shape=[jax.ShapeDtypeStruct((rows, wb), BF16)] * 4,
        compiler_params=_cp("arbitrary"),
        name="branch_out",
    )(ya_f, ya_b, bonus, ga, p, hc_f, hc_b, p, od_f, od_b, pd, *small)


def _pick_tile(*sizes):
    for t in (1024, 512, 256, 128):
        if all(s % t == 0 for s in sizes):
            return t
    raise ValueError("token counts must be multiples of 128")


def _pick_cols(n, cands):
    for t in cands:
        if n % t == 0:
            return t
    raise ValueError(f"no column tile for {n}")


def _proj_cols(d_model):
    wb = d_model // 4
    col = {"a": 0, "b": 3 * wb + 384}
    col["c_qk"] = col["b"] + 2 * wb
    col["c_v"] = col["c_qk"] + 2 * wb
    col["c_o"] = col["c_v"] + wb
    col["c_gate"] = col["c_o"] + wb
    col["abc_end"] = col["c_gate"] + 128
    col.update({"d_q": 0, "d_k": wb // 2, "d_v": wb, "d_g": 2 * wb, "d_a": 3 * wb, "d_end": 3 * wb + 128})
    assert all(v % 128 == 0 for v in col.values())
    return col


def _split_proj(a, d_model):
    col = _proj_cols(d_model)
    o_d = col["c_gate"] + 16
    n_d = col["d_a"] + 32
    o_g = o_d + n_d
    pad = jnp.zeros(a.shape[:-1] + (col["d_end"] - n_d,), a.dtype)
    return jnp.concatenate([a[..., o_d:o_g], pad], axis=-1), a[..., o_g:o_g + 4 * d_model]


def _mixers(p, pd, l, nb, t_lat, t_ctx, rows, w):
    wb = w["a_kk"].shape[1]
    col = _proj_cols(4 * wb)
    a_cols = 3 * wb + 384
    row = lambda a: a.reshape(1, -1)
    ones_bd = jnp.kron(jnp.eye(wb // A_HD, dtype=F32), jnp.ones((A_HD, A_HD), F32)).astype(BF16)

    a_wup, a_aup = w["a_wup"], w["a_aup"]
    wupp = jnp.zeros((2, 128, wb), F32).at[0, 0:64].set(a_wup[l, 0]).at[1, 64:128].set(a_wup[l, 1])
    aupp = jnp.zeros((2, 128, wb), F32).at[0, 0:64].set(a_aup[l, 0]).at[1, 64:128].set(a_aup[l, 1])
    small_a = [row(w["a_mu"][l]), row(w["a_kk"][l]), row(w["a_ka"][l]), row(w["a_rk"][l]), w["a_w0"][l],
               w["a_a0"][l], wupp, aupp, w["a_gup"][l], ones_bd]
    r_, v_, kk_, lwf, lwb, ktf, ktb, bf_, bb_, bonus, ga = _sequence_tiles(
        _rwkv_pre_kernel, p, col["a"], a_cols, GRID_W, small_a, [wb] * 11, nb, t_lat, t_ctx, "rwkv_pre")
    ya_f, ya_b = _rwkv_scan(r_, v_, kk_, lwf, lwb, ktf, ktb, bf_, bb_, nb, t_lat, t_ctx)

    kscale = jnp.concatenate([jnp.ones((wb,), F32), jnp.full((wb,), C_HD ** -0.5, F32)]).reshape(1, -1)
    qk, = _sequence_tiles(_conv_kernel, p, col["c_qk"], 2 * wb, 8,
                          [w["c_conv_w"][l], row(w["c_conv_b"][l]), kscale], [2 * wb], nb, t_lat, t_ctx,
                          "mlstm_conv")
    gbias = jnp.zeros((1, 128), F32).at[0, 0:16].set(w["c_gate_b"][l].reshape(-1))
    hc_f, hc_b = _mlstm_scan(qk, p, gbias, col["c_v"], col["c_gate"], nb, t_lat, t_ctx, wb)

    d_aup = w["d_aup"]
    rk_d = d_aup.shape[2]
    aupp_d = (jnp.zeros((2, 128, wb // 2), F32).at[0, 0:rk_d].set(d_aup[l, 0])
              .at[1, rk_d:2 * rk_d].set(d_aup[l, 1]))
    od_f, od_b = _gla_scan(pd, aupp_d, w["d_ab"][l].reshape(2, 1, -1), col["d_q"], col["d_k"], col["d_v"],
                           col["d_a"], nb, t_lat, t_ctx, wb)

    b_bias = jnp.repeat(w["b_bs"][l].T, B_CHUNK, axis=1)
    prm_o = {"a_lng": row(w["a_ln_g"][l]), "a_lnb": row(w["a_ln_b"][l]), "ones": ones_bd, "b_ws": w["b_ws"][l],
             "b_bias": b_bias, "b_lng": row(w["b_ln_g"][l]), "b_lnb": row(w["b_ln_b"][l]),
             "c_lng": row(w["c_ln_g"][l]), "d_lng": row(w["d_ln_g"][l])}
    return _branch_out(ya_f, ya_b, bonus, ga, p, pd, hc_f, hc_b, od_f, od_b, prm_o, col, rows, nb, t_lat, t_ctx)


def kernel(x, c, ctx, c_ctx, ada_w, ada_b, norm_g, ffn_w1, ffn_w3, ffn_w2, in_w, in_b, a_mu, a_w0, a_wup, a_a0, a_aup, a_gup, a_kk, a_ka, a_rk, a_ln_g, a_ln_b, b_ws, b_bs, b_ln_g, b_ln_b, c_conv_w, c_conv_b, c_gate_b, c_ln_g, d_aup, d_ab, d_ln_g, br_w, out_w, final_g):
    weights = dict(a_mu=a_mu, a_w0=a_w0, a_wup=a_wup, a_a0=a_a0, a_aup=a_aup, a_gup=a_gup, a_kk=a_kk, a_ka=a_ka,
                   a_rk=a_rk.reshape(a_rk.shape[0], -1), a_ln_g=a_ln_g, a_ln_b=a_ln_b, b_ws=b_ws, b_bs=b_bs,
                   b_ln_g=b_ln_g, b_ln_b=b_ln_b, c_conv_w=c_conv_w, c_conv_b=c_conv_b, c_gate_b=c_gate_b,
                   c_ln_g=c_ln_g, d_aup=d_aup, d_ab=d_ab, d_ln_g=d_ln_g)
    nb, t_lat, d_model = x.shape
    t_ctx = ctx.shape[1]
    depth = ada_w.shape[0]
    wb = d_model // 4
    d_ff = ffn_w1.shape[-1]
    n_lat = nb * t_lat
    n_ctx = nb * t_ctx
    rows_all = n_lat + n_ctx
    assert t_lat % (GRID_W * 2) == 0 and t_ctx % B_CHUNK == 0 and n_lat % t_ctx == 0
    assert wb == 512 and a_wup.shape[2] == 64 and a_aup.shape[2] == 64 and a_gup.shape[1] == 128

    tm = _pick_tile(t_lat, n_ctx)

    def grp(i):
        return jnp.where(i < n_lat // tm, 1 + i // (t_lat // tm), 0)

    tm_s = _pick_tile(t_lat, n_ctx, 512)
    tf = _pick_cols(d_ff, (512, 256, 128))

    h = jnp.concatenate([x.reshape(n_lat, d_model), ctx.reshape(n_ctx, d_model)], axis=0)

    m_pad = -(-(nb + 1) // 8) * 8
    cpad = jnp.zeros((m_pad, d_model), F32).at[0].set(c_ctx).at[1:nb + 1].set(c)
    mod_all = _ada_mod(cpad, ada_w, ada_b, 128)
    mod_all = mod_all.reshape(depth, m_pad, N_MOD, 1, d_model)

    n_abc = _proj_cols(d_model)["abc_end"]
    w_abc_all = in_w[:, :, 0:n_abc]
    w_d_all, w_gate_all = _split_proj(in_w, d_model)
    w_d_all, w_gate_all = w_d_all.astype(BF16), w_gate_all.astype(BF16)
    b_d_all, b_gate_all = _split_proj(in_b, d_model)
    w2_all = ffn_w2.astype(BF16)
    br_wb, out_wb = br_w.astype(BF16), out_w.astype(BF16)

    for l in range(depth):
        modp = mod_all[l]
        last = l == depth - 1

        hn = _norm_mod(h, norm_g[l, 0], modp, 0, rows_all, tm_s, lambda i: jnp.where(
            i < n_lat // tm_s, 1 + i // (t_lat // tm_s), 0))
        gact = _ffn_up(hn, ffn_w1, ffn_w3, l, 0, tm, tf)
        h = _ffn_down(gact, w2_all, h, modp, 2, l, 0, tm, 512, grp)

        hn = _norm_mod(h, norm_g[l, 1], modp, 3, rows_all, tm_s, lambda i: jnp.where(
            i < n_lat // tm_s, 1 + i // (t_lat // tm_s), 0))
        rows = n_lat if last else rows_all
        p = _in_proj(hn, w_abc_all, in_b[l, 0:n_abc].reshape(1, -1), rows_all, tm, 1280, layer=l)
        pd = _in_proj(hn, w_d_all, b_d_all[l].reshape(1, -1), rows_all, tm, w_d_all.shape[-1], layer=l)
        gates = _in_proj(hn, w_gate_all, b_gate_all[l].reshape(1, -1), rows, tm, 1024, gates=True, layer=l)

        hs = _mixers(p, pd, l, nb, t_lat, t_ctx, rows, weights)
        h = _merge_out(hs, gates, br_wb, out_wb, h, modp, 5, l, rows, tm, 512, grp)

        hn = _norm_mod(h, norm_g[l, 2], modp, 6, rows, tm_s, lambda i: jnp.where(
            i < n_lat // tm_s, 1 + i // (t_lat // tm_s), 0))
        gact = _ffn_up(hn, ffn_w1, ffn_w3, l, 1, tm, tf)
        h = _ffn_down(gact, w2_all, h, modp, 8, l, 1, tm, 512, grp)

    out = _final_norm(h, final_g, tm_s)
    return out.reshape(nb, t_lat, d_model)
```

```python
import functools

import jax
import jax.numpy as jnp
from jax import lax
from jax.experimental import pallas as pl
from jax.experimental.pallas import tpu as pltpu

F32 = jnp.float32
BF16 = jnp.bfloat16
HIGHEST = lax.Precision.HIGHEST

EPS = 1e-6
GRID_W = 64
N_MOD = 9
CHUNK = 64
A_HD = 64
A_LN_EPS = 64e-5
B_CHUNK = 128
C_HD = 128
C_CHUNK = 128
C_LN_EPS = 1e-5
D_DK = 64
D_DV = 128
D_TAU = 16.0
D_SUB = 16
D_LN_EPS = 1e-6
VMEM_LIMIT = 56 * 1024 * 1024


def _cp(*sem):
    return pltpu.CompilerParams(dimension_semantics=sem, vmem_limit_bytes=VMEM_LIMIT)


def _operands(a, b, precision):
    if precision is None:
        return a.astype(BF16), b.astype(BF16)
    return a, b


def _mm(a, b, precision=None):
    a, b = _operands(a, b, precision)
    return jnp.dot(a, b, precision=precision, preferred_element_type=F32)


def _nt(a, b, precision=None):
    a, b = _operands(a, b, precision)
    return lax.dot_general(a, b, (((1,), (1,)), ((), ())), precision=precision,
                           preferred_element_type=F32)


def _tn(a, b, precision=None):
    a, b = _operands(a, b, precision)
    return lax.dot_general(a, b, (((0,), (0,)), ((), ())), precision=precision,
                           preferred_element_type=F32)


def _log_sigmoid(x):
    return jnp.minimum(x, 0.0) - jnp.log(1.0 + jnp.exp(-jnp.abs(x)))


def _seg_sum(x, ones_blockdiag):
    hi = x.astype(BF16)
    lo = (x - hi.astype(F32)).astype(BF16)
    return _mm(hi, ones_blockdiag) + _mm(lo, ones_blockdiag)


def _tri(n, rev, strict):
    row = lax.broadcasted_iota(jnp.int32, (n, n), 0)
    col = lax.broadcasted_iota(jnp.int32, (n, n), 1)
    if rev:
        return (col > row) if strict else (col >= row)
    return (col < row) if strict else (col <= row)


def _norm_mod_kernel(h_ref, g_ref, sh_ref, sc_ref, o_ref):
    x = h_ref[...]
    y = x * lax.rsqrt(jnp.mean(x * x, axis=-1, keepdims=True) + EPS) * g_ref[...]
    o_ref[...] = (y * (1.0 + sc_ref[...]) + sh_ref[...]).astype(o_ref.dtype)


def _norm_mod(h, g, modp, k_shift, rows, tm, grp):
    d = h.shape[1]
    return pl.pallas_call(
        _norm_mod_kernel,
        grid=(rows // tm,),
        in_specs=[pl.BlockSpec((tm, d), lambda i: (i, 0)),
                  pl.BlockSpec((1, d), lambda i: (0, 0)),
                  pl.BlockSpec((None, None, 1, d), lambda i: (grp(i), k_shift, 0, 0)),
                  pl.BlockSpec((None, None, 1, d), lambda i: (grp(i), k_shift + 1, 0, 0))],
        out_specs=pl.BlockSpec((tm, d), lambda i: (i, 0)),
        out_shape=jax.ShapeDtypeStruct((rows, d), BF16),
        compiler_params=_cp("arbitrary"),
        name="norm_mod",
    )(h, g.reshape(1, d), modp, modp)


def _final_norm_kernel(h_ref, g_ref, o_ref):
    x = h_ref[...]
    o_ref[...] = x * lax.rsqrt(jnp.mean(x * x, axis=-1, keepdims=True) + EPS) * g_ref[...]


def _final_norm(h, g, tm):
    rows, d = h.shape
    return pl.pallas_call(
        _final_norm_kernel,
        grid=(rows // tm,),
        in_specs=[pl.BlockSpec((tm, d), lambda i: (i, 0)), pl.BlockSpec((1, d), lambda i: (0, 0))],
        out_specs=pl.BlockSpec((tm, d), lambda i: (i, 0)),
        out_shape=jax.ShapeDtypeStruct((rows, d), F32),
        compiler_params=_cp("arbitrary"),
        name="final_norm",
    )(h, g.reshape(1, d))


def _ada_kernel(c_ref, wa_ref, wb_ref, b_ref, o_ref):
    @pl.when(pl.program_id(1) == 0)
    def _():
        o_ref[...] = jnp.broadcast_to(b_ref[...], o_ref.shape)

    c = c_ref[...]
    cond = (c * jax.nn.sigmoid(c)).astype(BF16)
    half = wa_ref.shape[1]
    o_ref[:, 0:half] += _mm(cond, wa_ref[...].astype(BF16))
    o_ref[:, half:] += _mm(cond, wb_ref[...].astype(BF16))


def _ada_mod(cpad, ada_w, ada_b, tk):
    depth, d, n = ada_w.shape
    m = cpad.shape[0]
    return pl.pallas_call(
        _ada_kernel,
        grid=(depth, d // tk),
        in_specs=[pl.BlockSpec((m, tk), lambda l, k: (0, k)),
                  pl.BlockSpec((None, tk, n // 2), lambda l, k: (l, k, 0)),
                  pl.BlockSpec((None, tk, n // 2), lambda l, k: (l, k, 1)),
                  pl.BlockSpec((None, 1, n), lambda l, k: (l, 0, 0))],
        out_specs=pl.BlockSpec((None, m, n), lambda l, k: (l, 0, 0)),
        out_shape=jax.ShapeDtypeStruct((depth, m, n), F32),
        compiler_params=_cp("arbitrary", "arbitrary"),
        name="ada_mod",
    )(cpad, ada_w, ada_w, ada_b.reshape(depth, 1, n))


def _ffn_up_kernel(h_ref, g_ref, sh_ref, sc_ref, w1_ref, w3_ref, o_ref, x_scr):
    @pl.when(pl.program_id(1) == 0)
    def _():
        x = h_ref[...]
        y = x * lax.rsqrt(jnp.mean(x * x, axis=-1, keepdims=True) + EPS) * g_ref[...]
        x_scr[...] = (y * (1.0 + sc_ref[...]) + sh_ref[...]).astype(x_scr.dtype)

    x = x_scr[...]
    a = _mm(x, w1_ref[...])
    b = _mm(x, w3_ref[...])
    o_ref[...] = (a * jax.nn.sigmoid(a) * b).astype(o_ref.dtype)


def _ffn_up(h, g, modp, k_shift, w1, w3, l, s, rows, tm, tf, grp):
    d = h.shape[1]
    f = w1.shape[-1]
    wspec = pl.BlockSpec((None, None, d, tf), lambda i, j: (l, s, 0, j))
    return pl.pallas_call(
        _ffn_up_kernel,
        grid=(rows // tm, f // tf),
        in_specs=[pl.BlockSpec((tm, d), lambda i, j: (i, 0)),
                  pl.BlockSpec((1, d), lambda i, j: (0, 0)),
                  pl.BlockSpec((None, None, 1, d), lambda i, j: (grp(i), k_shift, 0, 0)),
                  pl.BlockSpec((None, None, 1, d), lambda i, j: (grp(i), k_shift + 1, 0, 0)), wspec, wspec],
        out_specs=pl.BlockSpec((tm, tf), lambda i, j: (i, j)),
        out_shape=jax.ShapeDtypeStruct((rows, f), BF16),
        scratch_shapes=[pltpu.VMEM((tm, d), BF16)],
        compiler_params=_cp("arbitrary", "arbitrary"),
        name="ffn_up",
    )(h, g.reshape(1, d), modp, modp, w1, w3)


def _ffn_down_kernel(g_ref, w_ref, h_ref, gate_ref, o_ref):
    acc = _mm(g_ref[...], w_ref[...])
    o_ref[...] = h_ref[...] + (0.5 * acc) * gate_ref[...]


def _ffn_down(gact, w2, h, modp, k_gate, tm, tn, grp):
    rows, f = gact.shape
    d = h.shape[1]
    return pl.pallas_call(
        _ffn_down_kernel,
        grid=(rows // tm, d // tn),
        in_specs=[pl.BlockSpec((tm, f), lambda i, j: (i, 0)),
                  pl.BlockSpec((f, tn), lambda i, j: (0, j)),
                  pl.BlockSpec((tm, tn), lambda i, j: (i, j)),
                  pl.BlockSpec((None, None, 1, tn), lambda i, j: (grp(i), k_gate, 0, j))],
        out_specs=pl.BlockSpec((tm, tn), lambda i, j: (i, j)),
        out_shape=jax.ShapeDtypeStruct((rows, d), F32),
        compiler_params=_cp("arbitrary", "arbitrary"),
        name="ffn_down",
    )(gact, w2, h, modp)


def _in_proj_kernel(x_ref, w_ref, b_ref, o_ref):
    o_ref[...] = _mm(x_ref[...], w_ref[...]) + b_ref[...]


def _gate_proj_kernel(x_ref, w_ref, b_ref, o_ref):
    o_ref[...] = jax.nn.sigmoid(_mm(x_ref[...], w_ref[...]) + b_ref[...]).astype(o_ref.dtype)


def _in_proj(x, w, b, rows, tm, tn, gates=False, layer=None):
    d = x.shape[1]
    n = b.shape[1]
    if layer is None:
        wspec = pl.BlockSpec((d, tn), lambda i, j: (0, j))
    else:
        wspec = pl.BlockSpec((None, d, tn), lambda i, j: (layer, 0, j))
    return pl.pallas_call(
        _gate_proj_kernel if gates else _in_proj_kernel,
        grid=(rows // tm, n // tn),
        in_specs=[pl.BlockSpec((tm, d), lambda i, j: (i, 0)), wspec,
                  pl.BlockSpec((1, tn), lambda i, j: (0, j))],
        out_specs=pl.BlockSpec((tm, tn), lambda i, j: (i, j)),
        out_shape=jax.ShapeDtypeStruct((rows, n), BF16 if gates else F32),
        compiler_params=_cp("arbitrary", "arbitrary"),
        name="gate_proj" if gates else "in_proj",
    )(x, w, b)


def _merge_out_kernel(ha, hb, hc, hd, ga, gb, gc, gd, wa, wb, wc, wd, ow_ref, h_ref, gate_ref, o_ref, y_scr, *, nj):
    j = pl.program_id(1)

    @pl.when(j < nj)
    def _():
        y = ga[...].astype(F32) * _mm(ha[...], wa[...])
        y = y + gb[...].astype(F32) * _mm(hb[...], wb[...])
        y = y + gc[...].astype(F32) * _mm(hc[...], wc[...])
        y = y + gd[...].astype(F32) * _mm(hd[...], wd[...])
        y_scr[j] = y.astype(y_scr.dtype)

    @pl.when(j >= nj)
    def _():
        tk = y_scr.shape[2]
        w = ow_ref[...]
        acc = _mm(y_scr[0], w[0:tk])
        for k in range(1, nj):
            acc = acc + _mm(y_scr[k], w[k * tk:(k + 1) * tk])
        o_ref[...] = h_ref[...] + acc * gate_ref[...]


def _merge_out(hs, gates, br_w, out_w, h, modp, k_gate, l, rows, tm, tn, grp):
    wbr = hs[0].shape[1]
    d = br_w.shape[-1]
    nj = d // tn
    mj = lambda j: jnp.minimum(j, nj - 1)
    oj = lambda j: jnp.maximum(j - nj, 0)
    hspec = pl.BlockSpec((tm, wbr), lambda i, j: (i, 0))
    gspecs = [pl.BlockSpec((tm, tn), functools.partial(lambda i, j, n: (i, n * nj + mj(j)), n=n))
              for n in range(4)]
    wspecs = [pl.BlockSpec((None, None, wbr, tn), functools.partial(lambda i, j, n: (l, n, 0, mj(j)), n=n))
              for n in range(4)]
    return pl.pallas_call(
        functools.partial(_merge_out_kernel, nj=nj),
        grid=(rows // tm, 2 * nj),
        in_specs=[hspec] * 4 + gspecs + wspecs
                 + [pl.BlockSpec((None, d, tn), lambda i, j: (l, 0, oj(j))),
                    pl.BlockSpec((tm, tn), lambda i, j: (i, oj(j))),
                    pl.BlockSpec((None, None, 1, tn), lambda i, j: (grp(i), k_gate, 0, oj(j)))],
        out_specs=pl.BlockSpec((tm, tn), lambda i, j: (i, oj(j))),
        out_shape=jax.ShapeDtypeStruct((rows, d), F32),
        scratch_shapes=[pltpu.VMEM((nj, tm, tn), BF16)],
        compiler_params=_cp("arbitrary", "arbitrary"),
        name="merge_out",
    )(*hs, gates, gates, gates, gates, br_w, br_w, br_w, br_w, out_w, h, modp)


def _tile_place(geom):
    n_lat_tiles, lat_tiles, ctx_tiles = geom
    i = pl.program_id(0)
    is_ctx = i >= n_lat_tiles
    per_seq = jnp.where(is_ctx, ctx_tiles, lat_tiles)
    pos = lax.rem(jnp.where(is_ctx, i - n_lat_tiles, i), per_seq)
    return is_ctx, pos == 0, pos == per_seq - 1


def _row_neighbours(xp_ref, x, xn_ref, first, last):
    ts = x.shape[0]
    hp = xp_ref.shape[0]
    t = lax.broadcasted_iota(jnp.int32, x.shape, 0)
    prv = jnp.where(t == 0, jnp.where(first, 0.0, xp_ref[hp - 1:hp, :]), pltpu.roll(x, 1, 0))
    nxt = jnp.where(t == ts - 1, jnp.where(last, 0.0, xn_ref[0:1, :]), pltpu.roll(x, ts - 1, 0))
    return prv, nxt


def _token_shift(xp_ref, x_ref, xn_ref, mu_ref, geom):
    is_ctx, first, last = _tile_place(geom)
    x = x_ref[...]
    ts = x.shape[0]
    t = lax.broadcasted_iota(jnp.int32, x.shape, 0)
    lane = lax.broadcasted_iota(jnp.int32, x.shape, 1)
    prv, nxt = _row_neighbours(xp_ref, x, xn_ref, first, last)
    sh_ctx = jnp.where((lane & 1) == 0, prv, nxt)
    tw = t & (GRID_W - 1)
    left = jnp.where(tw == 0, 0.0, prv)
    right = jnp.where(tw == GRID_W - 1, 0.0, nxt)
    up = jnp.concatenate([jnp.where(first, 0.0, xp_ref[...]), x[0:ts - GRID_W]], axis=0)
    down = jnp.concatenate([x[GRID_W:ts], jnp.where(last, 0.0, xn_ref[...])], axis=0)
    c4 = lane & 3
    sh_lat = jnp.where(c4 == 0, left, jnp.where(c4 == 1, right, jnp.where(c4 == 2, up, down)))
    sh = jnp.where(is_ctx, sh_ctx, sh_lat)
    return x + (sh - x) * mu_ref[...]


def _conv_kernel(xp_ref, x_ref, xn_ref, w_ref, b_ref, s_ref, o_ref, *, geom):
    _, first, last = _tile_place(geom)
    x = x_ref[...]
    prv, nxt = _row_neighbours(xp_ref, x, xn_ref, first, last)
    w = w_ref[...]
    y = prv * w[0:1] + x * w[1:2] + nxt * w[2:3] + b_ref[...]
    o_ref[...] = y * jax.nn.sigmoid(y) * s_ref[...]


def _sequence_tiles(body, p, col0, width, halo, small, out_widths, nb, t_lat, t_ctx, name):
    rows = p.shape[0]
    ts = min(256, t_ctx)
    assert t_lat % ts == 0 and t_ctx % ts == 0 and ts >= 2 * GRID_W and ts % halo == 0 and col0 % 128 == 0
    nt = rows // ts
    nh = rows // halo
    r = ts // halo
    geom = (nb * t_lat // ts, t_lat // ts, t_ctx // ts)
    full = lambda a: pl.BlockSpec(a.shape, lambda i: (0,) * a.ndim)
    window = lambda n: (pl.Element(n), pl.Element(width))
    return pl.pallas_call(
        functools.partial(body, geom=geom),
        grid=(nt,),
        in_specs=[pl.BlockSpec(window(halo), lambda i: (jnp.maximum(i * r - 1, 0) * halo, col0)),
                  pl.BlockSpec(window(ts), lambda i: (i * ts, col0)),
                  pl.BlockSpec(window(halo), lambda i: (jnp.minimum((i + 1) * r, nh - 1) * halo, col0))]
                 + [full(a) for a in small],
        out_specs=[pl.BlockSpec((ts, ow), lambda i: (i, 0)) for ow in out_widths],
        out_shape=[jax.ShapeDtypeStruct((rows, ow), F32) for ow in out_widths],
        compiler_params=_cp("arbitrary"),
        name=name,
    )(p, p, p, *small)


def _rwkv_pre_kernel(xp_ref, x_ref, xn_ref, mu_ref, kk_ref, ka_ref, rk_ref, w0_ref, a0_ref, wup_ref, aup_ref,
                     gup_ref, ones_ref, r_o, v_o, kk_o, lwf_o, lwb_o, ktf_o, ktb_o, bf_o, bb_o, bonus_o, g_o, *,
                     geom):
    wb = r_o.shape[1]
    za = _token_shift(xp_ref, x_ref, xn_ref, mu_ref, geom)
    r = za[:, 0:wb]
    k = za[:, wb:2 * wb]
    v = za[:, 2 * wb:3 * wb]
    wd = jnp.tanh(za[:, 3 * wb:3 * wb + 128])
    ad = za[:, 3 * wb + 128:3 * wb + 256]
    gd = jax.nn.sigmoid(za[:, 3 * wb + 256:3 * wb + 384])
    ones = ones_ref[...]
    kq = k * kk_ref[...]
    kk = kq * lax.rsqrt(jnp.maximum(_seg_sum(kq * kq, ones), 1e-24))
    r_o[...] = r
    v_o[...] = v
    kk_o[...] = kk
    ka = ka_ref[...]
    for d, (lw_o, kt_o, b_o) in enumerate(((lwf_o, ktf_o, bf_o), (lwb_o, ktb_o, bb_o))):
        xw = w0_ref[d:d + 1] + _mm(wd, wup_ref[d])
        lw_o[...] = -jax.nn.sigmoid(xw) * 0.6065306597126334
        a = jax.nn.sigmoid(a0_ref[d:d + 1] + _mm(ad, aup_ref[d]))
        kt_o[...] = k * (1.0 + (a - 1.0) * ka)
        b_o[...] = kk * a
    bonus_o[...] = _seg_sum(r * k * rk_ref[...], ones) * v
    g_o[...] = _mm(gd, gup_ref[...])


def _scan_geometry(nb, t_lat, t_ctx, chunk):
    nlc = t_lat // chunk
    ncc = t_ctx // chunk
    nch = nlc + ncc

    def fwd(j):
        return jnp.where(j < ncc, nlc + j, j - ncc)

    def rev(j):
        return nch - 1 - j

    def row(b, c):
        return jnp.where(c < nlc, b * nlc + c, nb * nlc + b * ncc + (c - nlc))

    return fwd, rev, row, nch


def _scan_in_specs(nb, row, cm, chunk, width, col):
    return [pl.BlockSpec((pl.Element(chunk), pl.Element(width)),
                         functools.partial(lambda j, b: (row(b, cm(j)) * chunk, col), b=b)) for b in range(nb)]


def _scan_out_spec(nb, cm, chunk, width):
    return pl.BlockSpec((None, nb, chunk, width), lambda j: (cm(j), 0, 0, 0))


def _rwkv_scan_kernel(*refs, nb):
    ins = refs[:12 * nb]
    yf, yb, s_ref = refs[12 * nb:]

    @pl.when(pl.program_id(0) == 0)
    def _():
        s_ref[...] = jnp.zeros_like(s_ref)

    n, wbw = ins[0].shape
    nh = wbw // A_HD
    heads = []
    for d, y_r in enumerate((yf, yb)):
        rev = d == 1
        arr = lambda a, b: ins[(d * 6 + a) * nb + b][...]
        lws = [arr(3, b) for b in range(nb)]
        c_all = _mm(_tri(n, rev, False).astype(F32), jnp.concatenate(lws, axis=1), HIGHEST)
        strict = _tri(n, rev, True)
        incl = _tri(n, rev, False)
        for b in range(nb):
            lw = lws[b]
            c = c_all[:, b * wbw:(b + 1) * wbw]
            eg = jnp.exp(c)
            ieg = jnp.exp(-c)
            r_all = arr(0, b) * eg
            kk_all = arr(2, b) * jnp.exp(c - lw)
            kt_all = arr(4, b) * ieg
            b_all = arr(5, b) * ieg
            v_all = arr(1, b)
            g_last = eg[0:1] if rev else eg[n - 1:n]
            for h in range(nh):
                sl = slice(h * A_HD, (h + 1) * A_HD)
                idx = (d * nb + b) * nh + h
                heads.append(dict(r=r_all[:, sl], kk=kk_all[:, sl], b=b_all[:, sl], kt=kt_all[:, sl],
                                  v=v_all[:, sl], s=s_ref[idx], g=g_last[:, sl], strict=strict, incl=incl,
                                  y_ref=y_r.at[b], sl=sl, idx=idx))
    for t in heads:
        t["z"] = _nt(jnp.concatenate([t["kk"], t["r"]], axis=0), jnp.concatenate([t["kt"], t["b"], t["s"]], axis=0))
    for t in heads:
        z = t["z"]
        t["a_kv"] = jnp.where(t["strict"], z[0:n, 0:n], 0.0)
        t["a_kb"] = jnp.where(t["strict"], z[0:n, n:2 * n], 0.0)
        t["r_kv"] = jnp.where(t["incl"], z[n:2 * n, 0:n], 0.0)
        t["r_kb"] = jnp.where(t["incl"], z[n:2 * n, n:2 * n], 0.0)
    for t in heads:
        t["av"] = _mm(jnp.concatenate([t["a_kv"], t["r_kv"]], axis=0), t["v"])
    left = lax.broadcasted_iota(jnp.int32, (n, 2 * n), 1) < n
    for t in heads:
        t["w"] = jnp.concatenate([t["a_kb"], t["z"][0:n, 2 * n:] + t["av"][0:n]], axis=1)
    for t in heads:
        r = _mm(t["a_kb"], t["w"])
        t["w"] = jnp.where(left, r, t["w"] - r)
    m = 2
    while m < n:
        for t in heads:
            r = _mm(t["w"][:, 0:n], t["w"])
            t["w"] = jnp.where(left, r, t["w"] + r)
        m *= 2
    for t in heads:
        t["u"] = t["w"][:, n:2 * n]
        y = t["z"][n:2 * n, 2 * n:] + t["av"][n:2 * n] - _mm(t["r_kb"], t["u"])
        t["y_ref"][:, t["sl"]] = y
        s_new = t["s"] + _tn(jnp.concatenate([t["v"], t["u"]], axis=0), jnp.concatenate([t["kt"], -t["b"]], axis=0))
        s_ref[t["idx"]] = s_new * t["g"]


def _rwkv_scan(r, v, kk, lwf, lwb, ktf, ktb, bf, bb, nb, t_lat, t_ctx):
    wb = r.shape[1]
    fwd, rev, row, nch = _scan_geometry(nb, t_lat, t_ctx, CHUNK)
    in_specs = [s for cm in (fwd, rev) for _ in range(6) for s in _scan_in_specs(nb, row, cm, CHUNK, wb, 0)]
    args = [a for grp in ((r, v, kk, lwf, ktf, bf), (r, v, kk, lwb, ktb, bb)) for a in grp for _ in range(nb)]
    return pl.pallas_call(
        functools.partial(_rwkv_scan_kernel, nb=nb),
        grid=(nch,),
        in_specs=in_specs,
        out_specs=[_scan_out_spec(nb, fwd, CHUNK, wb), _scan_out_spec(nb, rev, CHUNK, wb)],
        out_shape=[jax.ShapeDtypeStruct((nch, nb, CHUNK, wb), F32)] * 2,
        scratch_shapes=[pltpu.VMEM((2 * nb * (wb // A_HD), A_HD, A_HD), F32)],
        compiler_params=_cp("arbitrary"),
        name="rwkv_scan",
    )(*args)


def _mlstm_scan_kernel(*refs, nb):
    ins = refs[:8 * nb]
    gbias_ref, hf, hb, c_ref, n_ref, m_ref = refs[8 * nb:]

    @pl.when(pl.program_id(0) == 0)
    def _():
        c_ref[...] = jnp.zeros_like(c_ref)
        n_ref[...] = jnp.zeros_like(n_ref)
        m_ref[...] = jnp.zeros_like(m_ref)

    n = ins[0].shape[0]
    nh = ins[0].shape[1] // C_HD
    heads = []
    for d, h_r in enumerate((hf, hb)):
        rev = d == 1
        arr = lambda a, b: ins[(d * 4 + a) * nb + b][...]
        gates = [arr(3, b) + gbias_ref[...] for b in range(nb)]
        fgs = [_log_sigmoid(g) for g in gates]
        mask = _tri(n, rev, False)
        mi = mask.astype(F32)
        bcol_all = _mm(mi, jnp.concatenate(fgs, axis=1), HIGHEST)
        brow_all = _nt(jnp.concatenate([f.T for f in fgs], axis=0), mi, HIGHEST)
        for b in range(nb):
            bcol = bcol_all[:, b * 128:(b + 1) * 128]
            brow = brow_all[b * 128:(b + 1) * 128]
            gates_t = gates[b].T
            q_all, k_all, v_all = arr(0, b), arr(1, b), arr(2, b)
            for h in range(nh):
                sl = slice(h * C_HD, (h + 1) * C_HD)
                ii = d * 2 * nh + h
                fi = ii + nh
                idx = (d * nb + b) * nh + h
                b_c = bcol[:, fi:fi + 1]
                heads.append(dict(q=q_all[:, sl], k=k_all[:, sl], v=v_all[:, sl], b_c=b_c, b_r=brow[fi:fi + 1, :],
                                  i_c=gates[b][:, ii:ii + 1], i_r=gates_t[ii:ii + 1, :], m=m_ref[idx][:, 0:1],
                                  cm=c_ref[idx], nn=n_ref[idx], mask=mask,
                                  b_l=b_c[0:1] if rev else b_c[n - 1:n], idx=idx, h_ref=h_r.at[b], sl=sl))
    for t in heads:
        t["qk"] = _nt(t["q"], t["k"])
        t["qc"] = _nt(t["q"], t["cm"])
    for t in heads:
        t["dlog"] = jnp.where(t["mask"], t["b_c"] - t["b_r"] + t["i_r"], -jnp.inf)
        t["gl"] = t["b_l"] - t["b_c"] + t["i_c"]
    for t in heads:
        t["dmax"] = jnp.max(t["dlog"], axis=1, keepdims=True)
        t["gmax"] = jnp.max(t["gl"], axis=0, keepdims=True)
    for t in heads:
        inter = t["b_c"] + t["m"]
        m_t = jnp.maximum(inter, t["dmax"])
        t["iw"] = jnp.exp(inter - m_t)
        t["m_t"] = m_t
        t["s"] = t["qk"] * jnp.exp(t["dlog"] - m_t)
        m_new = jnp.maximum(t["b_l"] + t["m"], t["gmax"])
        t["sw"] = jnp.exp(t["gl"] - m_new)
        t["dec"] = jnp.exp(t["b_l"] + t["m"] - m_new)
        t["m_new"] = m_new
    for t in heads:
        t["sv"] = _mm(t["s"], t["v"])
        t["vk"] = _tn(t["v"] * t["sw"], t["k"])
    for t in heads:
        t["rs"] = jnp.sum(t["s"], axis=1, keepdims=True)
        t["qn"] = jnp.sum(t["q"] * t["nn"], axis=1, keepdims=True)
        t["ks"] = jnp.sum(t["sw"] * t["k"], axis=0, keepdims=True)
    for t in heads:
        iw = t["iw"]
        num = t["sv"] + iw * t["qc"]
        den = jnp.maximum(jnp.abs(t["rs"] + iw * t["qn"]), jnp.exp(-t["m_t"]))
        t["h_ref"][:, t["sl"]] = num / den
        idx = t["idx"]
        c_ref[idx] = t["dec"] * t["cm"] + t["vk"]
        n_ref[idx] = t["dec"] * t["nn"] + t["ks"]
        m_ref[idx] = jnp.broadcast_to(t["m_new"], m_ref.shape[1:])


def _mlstm_scan(qk, p, gbias, col_v, col_g, nb, t_lat, t_ctx, wb):
    ck = C_CHUNK
    fwd, rev, row, nch = _scan_geometry(nb, t_lat, t_ctx, ck)
    nh = wb // C_HD

    def specs(cm):
        return (_scan_in_specs(nb, row, cm, ck, wb, 0) + _scan_in_specs(nb, row, cm, ck, wb, wb)
                + _scan_in_specs(nb, row, cm, ck, wb, col_v) + _scan_in_specs(nb, row, cm, ck, 128, col_g))

    args = ([qk] * nb + [qk] * nb + [p] * nb + [p] * nb) * 2
    return pl.pallas_call(
        functools.partial(_mlstm_scan_kernel, nb=nb),
        grid=(nch,),
        in_specs=specs(fwd) + specs(rev) + [pl.BlockSpec((1, 128), lambda j: (0, 0))],
        out_specs=[_scan_out_spec(nb, fwd, ck, wb), _scan_out_spec(nb, rev, ck, wb)],
        out_shape=[jax.ShapeDtypeStruct((nch, nb, ck, wb), F32)] * 2,
        scratch_shapes=[pltpu.VMEM((2 * nb * nh, C_HD, C_HD), F32), pltpu.VMEM((2 * nb * nh, 1, C_HD), F32),
                        pltpu.VMEM((2 * nb * nh, 1, 128), F32)],
        compiler_params=_cp("arbitrary"),
        name="mlstm_scan",
    )(*args, gbias)


def _gla_scan_kernel(*refs, nb):
    ins = refs[:8 * nb]
    aup_ref, abias_ref, of, ob, s_ref = refs[8 * nb:]

    @pl.when(pl.program_id(0) == 0)
    def _():
        s_ref[...] = jnp.zeros_like(s_ref)

    n, wkw = ins[0].shape
    nh = wkw // D_DK
    nsub = n // D_SUB
    pw = 2 * D_DK
    row = lax.broadcasted_iota(jnp.int32, (n, pw), 0)
    colx = lax.broadcasted_iota(jnp.int32, (n, pw), 1) & (D_DK - 1)
    rel = colx - (row & -D_SUB)
    rin = row & (D_SUB - 1)
    same_head = ((lax.broadcasted_iota(jnp.int32, (pw, pw), 0) & D_DK)
                 == (lax.broadcasted_iota(jnp.int32, (pw, pw), 1) & D_DK))
    ones_pair = jnp.where(same_head, 1.0, 0.0).astype(BF16)
    heads, pairs = [], []
    for d, o_r in enumerate((of, ob)):
        rev = d == 1
        arr = lambda a, b: ins[(d * 4 + a) * nb + b][...]
        la_rows = _log_sigmoid(_mm(jnp.concatenate([arr(3, b) for b in range(nb)], axis=0), aup_ref[d])
                               + abias_ref[d]) * (1.0 / D_TAU)
        la_all = jnp.concatenate([la_rows[b * n:(b + 1) * n] for b in range(nb)], axis=1)
        bc_all = _mm(_tri(n, rev, False).astype(F32), la_all, HIGHEST)
        for b in range(nb):
            bc = bc_all[:, b * wkw:(b + 1) * wkw]
            q_all = arr(0, b) * (D_DK ** -0.5)
            k_all = arr(1, b)
            v_all = arr(2, b)
            base = (d * nb + b) * nh
            for h in range(nh):
                ksl = slice(h * D_DK, (h + 1) * D_DK)
                vsl = slice(h * D_DV, (h + 1) * D_DV)
                heads.append(dict(q=q_all[:, ksl], k=k_all[:, ksl], v=v_all[:, vsl], bc=bc[:, ksl], rev=rev,
                                  s=s_ref[base + h], idx=base + h, o_ref=o_r.at[b], vsl=vsl))
            for p in range(nh // 2):
                psl = slice(p * pw, (p + 1) * pw)
                pairs.append(dict(q=q_all[:, psl], k=k_all[:, psl], bc=bc[:, psl], rev=rev,
                                  heads=(base + 2 * p, base + 2 * p + 1)))
    for t in heads:
        t["o"] = _nt(t["q"] * jnp.exp(t["bc"]), t["s"])
    for t in heads:
        q, k, bc, rev = t["q"], t["k"], t["bc"], t["rev"]
        pieces = []
        for blk in range(nsub):
            r0 = blk * D_SUB
            edge = bc[r0 + D_SUB - 1:r0 + D_SUB] if rev else bc[r0:r0 + 1]
            has_other = blk < nsub - 1 if rev else blk > 0
            if has_other:
                qs = q[r0:r0 + D_SUB] * jnp.exp(bc[r0:r0 + D_SUB] - edge)
                ks = k * jnp.exp(jnp.minimum(edge - bc, 0.0))
                pieces.append(_nt(qs, ks))
            else:
                pieces.append(jnp.zeros((D_SUB, n), F32))
        t["sc"] = jnp.concatenate(pieces, axis=0)
    for t in pairs:
        q, k, bc = t["q"], t["k"], t["bc"]
        es = []
        for j in range(D_SUB):
            kj = jnp.concatenate([jnp.broadcast_to(k[b * D_SUB + j:b * D_SUB + j + 1], (D_SUB, pw))
                                  for b in range(nsub)], axis=0)
            bj = jnp.concatenate([jnp.broadcast_to(bc[b * D_SUB + j:b * D_SUB + j + 1], (D_SUB, pw))
                                  for b in range(nsub)], axis=0)
            es.append(q * kj * jnp.exp(bc - bj))
        t["e"] = jnp.concatenate(es, axis=0)
    for t in pairs:
        t["c"] = _seg_sum(t["e"], ones_pair)
    for t in pairs:
        rev = t["rev"]
        sc = jnp.concatenate([heads[t["heads"][0]]["sc"], heads[t["heads"][1]]["sc"]], axis=1)
        a = jnp.where((rel >= D_SUB) if rev else (rel < 0), sc, 0.0)
        for j in range(D_SUB):
            keep = (rel == j) & ((rin <= j) if rev else (rin >= j))
            a = jnp.where(keep, t["c"][j * n:(j + 1) * n], a)
        heads[t["heads"][0]]["a"] = a[:, 0:D_DK]
        heads[t["heads"][1]]["a"] = a[:, D_DK:pw]
    for t in heads:
        t["o_ref"][:, t["vsl"]] = t["o"] + _mm(t["a"], t["v"])
        bc = t["bc"]
        b_l = bc[0:1] if t["rev"] else bc[n - 1:n]
        s_ref[t["idx"]] = t["s"] * jnp.exp(b_l) + _tn(t["v"], t["k"] * jnp.exp(b_l - bc))


def _gla_scan(p, aupp, abias, col_q, col_k, col_v, col_a, nb, t_lat, t_ctx, wb):
    fwd, rev, row, nch = _scan_geometry(nb, t_lat, t_ctx, CHUNK)
    wk = aupp.shape[-1]
    nh = wk // D_DK

    def specs(cm):
        return (_scan_in_specs(nb, row, cm, CHUNK, wk, col_q) + _scan_in_specs(nb, row, cm, CHUNK, wk, col_k)
                + _scan_in_specs(nb, row, cm, CHUNK, wb, col_v) + _scan_in_specs(nb, row, cm, CHUNK, 128, col_a))

    return pl.pallas_call(
        functools.partial(_gla_scan_kernel, nb=nb),
        grid=(nch,),
        in_specs=specs(fwd) + specs(rev) + [pl.BlockSpec(aupp.shape, lambda j: (0, 0, 0)),
                                           pl.BlockSpec(abias.shape, lambda j: (0, 0, 0))],
        out_specs=[_scan_out_spec(nb, fwd, CHUNK, wb), _scan_out_spec(nb, rev, CHUNK, wb)],
        out_shape=[jax.ShapeDtypeStruct((nch, nb, CHUNK, wb), F32)] * 2,
        scratch_shapes=[pltpu.VMEM((2 * nb * nh, D_DV, D_DK), F32)],
        compiler_params=_cp("arbitrary"),
        name="gla_scan",
    )(*([p] * (8 * nb)), aupp, abias)


def _branch_out_kernel(ya_f, ya_b, bonus, ga, pb, hc_f, hc_b, oc, od_f, od_b, gdd,
                       a_lng, a_lnb, ones, b_ws, b_bias, b_lng, b_lnb, c_lng, d_lng,
                       hs_a, hs_b, hs_c, hs_d):
    wb = hs_a.shape[1]
    tm = hs_a.shape[0]
    y = ya_f[...].reshape(tm, wb) + ya_b[...].reshape(tm, wb)
    on = ones[...]
    mu = _seg_sum(y, on) * (1.0 / A_HD)
    yc = y - mu
    var = _seg_sum(yc * yc, on) * (1.0 / A_HD)
    yn = yc * lax.rsqrt(var + A_LN_EPS) * a_lng[...] + a_lnb[...]
    hs_a[...] = ((yn + bonus[...]) * ga[...]).astype(hs_a.dtype)
    z = jax.nn.gelu(pb[...])
    u = z[:, 0:wb]
    vv = z[:, wb:2 * wb]
    mu = jnp.mean(vv, axis=-1, keepdims=True)
    vc = vv - mu
    vn = (vc * lax.rsqrt(jnp.mean(vc * vc, axis=-1, keepdims=True) + 1e-5) * b_lng[...] + b_lnb[...])
    vn = vn.astype(BF16)
    for ck in range(u.shape[0] // B_CHUNK):
        rs = slice(ck * B_CHUNK, (ck + 1) * B_CHUNK)
        for g in range(wb // B_CHUNK):
            cs = slice(g * B_CHUNK, (g + 1) * B_CHUNK)
            s = _mm(b_ws[g].astype(BF16), vn[rs, cs]) + b_bias[:, cs]
            hs_b[rs, cs] = (u[rs, cs] * s).astype(hs_b.dtype)
    hc = hc_f[...].reshape(tm, wb) + hc_b[...].reshape(tm, wb)
    ogate = jax.nn.sigmoid(oc[...])
    gc = c_lng[...]
    for h in range(wb // C_HD):
        sl = slice(h * C_HD, (h + 1) * C_HD)
        x = hc[:, sl]
        xc = x - jnp.mean(x, axis=-1, keepdims=True)
        xn = xc * lax.rsqrt(jnp.mean(xc * xc, axis=-1, keepdims=True) + C_LN_EPS) * gc[:, sl]
        hs_c[:, sl] = (xn * ogate[:, sl]).astype(hs_c.dtype)
    od = od_f[...].reshape(tm, wb) + od_b[...].reshape(tm, wb)
    gg = gdd[...]
    gg = gg * jax.nn.sigmoid(gg)
    gd_ = d_lng[...]
    for h in range(wb // D_DV):
        sl = slice(h * D_DV, (h + 1) * D_DV)
        x = od[:, sl]
        xn = x * lax.rsqrt(jnp.mean(x * x, axis=-1, keepdims=True) + D_LN_EPS) * gd_[:, sl]
        hs_d[:, sl] = (xn * gg[:, sl]).astype(hs_d.dtype)


def _branch_out(ya_f, ya_b, bonus, ga, p, pd, hc_f, hc_b, od_f, od_b, prm, cols, rows, nb, t_lat, t_ctx):
    wb = bonus.shape[1]
    tm = min(256, t_ctx)
    n_lat_tiles, lat_tiles, ctx_tiles = nb * t_lat // tm, t_lat // tm, t_ctx // tm
    row = lambda w: pl.BlockSpec((tm, w), lambda i: (i, 0))
    cols_at = lambda w, c0: pl.BlockSpec((pl.Element(tm), pl.Element(w)), lambda i: (i * tm, c0))

    def scan_out(a):
        per_tile = tm // a.shape[2]

        def imap(i):
            ic = i - n_lat_tiles
            seq = jnp.where(i < n_lat_tiles, i // lat_tiles, ic // ctx_tiles)
            blk = jnp.where(i < n_lat_tiles, lax.rem(i, lat_tiles), lat_tiles + lax.rem(ic, ctx_tiles))
            return blk, seq, 0, 0

        return pl.BlockSpec((per_tile, None, a.shape[2], wb), imap)

    small = [prm["a_lng"], prm["a_lnb"], prm["ones"], prm["b_ws"], prm["b_bias"], prm["b_lng"], prm["b_lnb"],
             prm["c_lng"], prm["d_lng"]]
    full = lambda a: pl.BlockSpec(a.shape, lambda i: (0,) * a.ndim)
    return pl.pallas_call(
        _branch_out_kernel,
        grid=(rows // tm,),
        in_specs=[scan_out(ya_f), scan_out(ya_b), row(wb), row(wb), cols_at(2 * wb, cols["b"]),
                  scan_out(hc_f), scan_out(hc_b), cols_at(wb, cols["c_o"]), scan_out(od_f), scan_out(od_b),
                  cols_at(wb, cols["d_g"])] + [full(a) for a in small],
        out_specs=[row(wb)] * 4,
        out_shape=[jax.ShapeDtypeStruct((rows, wb), BF16)] * 4,
        compiler_params=_cp("arbitrary"),
        name="branch_out",
    )(ya_f, ya_b, bonus, ga, p, hc_f, hc_b, p, od_f, od_b, pd, *small)


def _pick_tile(*sizes):
    for t in (1024, 512, 256, 128):
        if all(s % t == 0 for s in sizes):
            return t
    raise ValueError("token counts must be multiples of 128")


def _pick_cols(n, cands):
    for t in cands:
        if n % t == 0:
            return t
    raise ValueError(f"no column tile for {n}")


def _proj_cols(d_model):
    wb = d_model // 4
    col = {"a": 0, "b": 3 * wb + 384}
    col["c_qk"] = col["b"] + 2 * wb
    col["c_v"] = col["c_qk"] + 2 * wb
    col["c_o"] = col["c_v"] + wb
    col["c_gate"] = col["c_o"] + wb
    col["abc_end"] = col["c_gate"] + 128
    col.update({"d_q": 0, "d_k": wb // 2, "d_v": wb, "d_g": 2 * wb, "d_a": 3 * wb, "d_end": 3 * wb + 128})
    assert all(v % 128 == 0 for v in col.values())
    return col


def _split_proj(a, d_model):
    col = _proj_cols(d_model)
    o_d = col["c_gate"] + 16
    n_d = col["d_a"] + 32
    o_g = o_d + n_d
    pad = jnp.zeros(a.shape[:-1] + (col["d_end"] - n_d,), a.dtype)
    return jnp.concatenate([a[..., o_d:o_g], pad], axis=-1), a[..., o_g:o_g + 4 * d_model]


def _mixers(p, pd, l, nb, t_lat, t_ctx, rows, w):
    wb = w["a_kk"].shape[1]
    col = _proj_cols(4 * wb)
    a_cols = 3 * wb + 384
    row = lambda a: a.reshape(1, -1)
    ones_bd = jnp.kron(jnp.eye(wb // A_HD, dtype=F32), jnp.ones((A_HD, A_HD), F32)).astype(BF16)

    a_wup, a_aup = w["a_wup"], w["a_aup"]
    wupp = jnp.zeros((2, 128, wb), F32).at[0, 0:64].set(a_wup[l, 0]).at[1, 64:128].set(a_wup[l, 1])
    aupp = jnp.zeros((2, 128, wb), F32).at[0, 0:64].set(a_aup[l, 0]).at[1, 64:128].set(a_aup[l, 1])
    small_a = [row(w["a_mu"][l]), row(w["a_kk"][l]), row(w["a_ka"][l]), row(w["a_rk"][l]), w["a_w0"][l],
               w["a_a0"][l], wupp, aupp, w["a_gup"][l], ones_bd]
    r_, v_, kk_, lwf, lwb, ktf, ktb, bf_, bb_, bonus, ga = _sequence_tiles(
        _rwkv_pre_kernel, p, col["a"], a_cols, GRID_W, small_a, [wb] * 11, nb, t_lat, t_ctx, "rwkv_pre")
    ya_f, ya_b = _rwkv_scan(r_, v_, kk_, lwf, lwb, ktf, ktb, bf_, bb_, nb, t_lat, t_ctx)

    kscale = jnp.concatenate([jnp.ones((wb,), F32), jnp.full((wb,), C_HD ** -0.5, F32)]).reshape(1, -1)
    qk, = _sequence_tiles(_conv_kernel, p, col["c_qk"], 2 * wb, 8,
                          [w["c_conv_w"][l], row(w["c_conv_b"][l]), kscale], [2 * wb], nb, t_lat, t_ctx,
                          "mlstm_conv")
    gbias = jnp.zeros((1, 128), F32).at[0, 0:16].set(w["c_gate_b"][l].reshape(-1))
    hc_f, hc_b = _mlstm_scan(qk, p, gbias, col["c_v"], col["c_gate"], nb, t_lat, t_ctx, wb)

    d_aup = w["d_aup"]
    rk_d = d_aup.shape[2]
    aupp_d = (jnp.zeros((2, 128, wb // 2), F32).at[0, 0:rk_d].set(d_aup[l, 0])
              .at[1, rk_d:2 * rk_d].set(d_aup[l, 1]))
    od_f, od_b = _gla_scan(pd, aupp_d, w["d_ab"][l].reshape(2, 1, -1), col["d_q"], col["d_k"], col["d_v"],
                           col["d_a"], nb, t_lat, t_ctx, wb)

    b_bias = jnp.repeat(w["b_bs"][l].T, B_CHUNK, axis=1)
    prm_o = {"a_lng": row(w["a_ln_g"][l]), "a_lnb": row(w["a_ln_b"][l]), "ones": ones_bd, "b_ws": w["b_ws"][l],
             "b_bias": b_bias, "b_lng": row(w["b_ln_g"][l]), "b_lnb": row(w["b_ln_b"][l]),
             "c_lng": row(w["c_ln_g"][l]), "d_lng": row(w["d_ln_g"][l])}
    return _branch_out(ya_f, ya_b, bonus, ga, p, pd, hc_f, hc_b, od_f, od_b, prm_o, col, rows, nb, t_lat, t_ctx)


def kernel(x, c, ctx, c_ctx, ada_w, ada_b, norm_g, ffn_w1, ffn_w3, ffn_w2, in_w, in_b, a_mu, a_w0, a_wup, a_a0, a_aup, a_gup, a_kk, a_ka, a_rk, a_ln_g, a_ln_b, b_ws, b_bs, b_ln_g, b_ln_b, c_conv_w, c_conv_b, c_gate_b, c_ln_g, d_aup, d_ab, d_ln_g, br_w, out_w, final_g):
    weights = dict(a_mu=a_mu, a_w0=a_w0, a_wup=a_wup, a_a0=a_a0, a_aup=a_aup, a_gup=a_gup, a_kk=a_kk, a_ka=a_ka,
                   a_rk=a_rk.reshape(a_rk.shape[0], -1), a_ln_g=a_ln_g, a_ln_b=a_ln_b, b_ws=b_ws, b_bs=b_bs,
                   b_ln_g=b_ln_g, b_ln_b=b_ln_b, c_conv_w=c_conv_w, c_conv_b=c_conv_b, c_gate_b=c_gate_b,
                   c_ln_g=c_ln_g, d_aup=d_aup, d_ab=d_ab, d_ln_g=d_ln_g)
    nb, t_lat, d_model = x.shape
    t_ctx = ctx.shape[1]
    depth = ada_w.shape[0]
    wb = d_model // 4
    d_ff = ffn_w1.shape[-1]
    n_lat = nb * t_lat
    n_ctx = nb * t_ctx
    rows_all = n_lat + n_ctx
    assert t_lat % (GRID_W * 2) == 0 and t_ctx % B_CHUNK == 0 and n_lat % t_ctx == 0
    assert wb == 512 and a_wup.shape[2] == 64 and a_aup.shape[2] == 64 and a_gup.shape[1] == 128

    tm = _pick_tile(t_lat, n_ctx)

    def grp(i):
        return jnp.where(i < n_lat // tm, 1 + i // (t_lat // tm), 0)

    tm_s = _pick_tile(t_lat, n_ctx, 512)
    tf = _pick_cols(d_ff, (512, 256, 128))

    h = jnp.concatenate([x.reshape(n_lat, d_model), ctx.reshape(n_ctx, d_model)], axis=0)

    m_pad = -(-(nb + 1) // 8) * 8
    cpad = jnp.zeros((m_pad, d_model), F32).at[0].set(c_ctx).at[1:nb + 1].set(c)
    mod_all = _ada_mod(cpad, ada_w, ada_b, 128)
    mod_all = mod_all.reshape(depth, m_pad, N_MOD, 1, d_model)

    for l in range(depth):
        modp = mod_all[l]
        last = l == depth - 1

        gact = _ffn_up(h, norm_g[l, 0], modp, 0, ffn_w1, ffn_w3, l, 0, rows_all, tm, tf, grp)
        h = _ffn_down(gact, ffn_w2[l, 0].astype(BF16), h, modp, 2, tm, 512, grp)

        hn = _norm_mod(h, norm_g[l, 1], modp, 3, rows_all, tm_s, lambda i: jnp.where(
            i < n_lat // tm_s, 1 + i // (t_lat // tm_s), 0))
        w_d, w_gate = _split_proj(in_w[l], d_model)
        b_d, b_gate = _split_proj(in_b[l], d_model)
        n_abc = _proj_cols(d_model)["abc_end"]
        rows = n_lat if last else rows_all
        p = _in_proj(hn, in_w, in_b[l, 0:n_abc].reshape(1, -1), rows_all, tm, 1280, layer=l)
        pd = _in_proj(hn, w_d.astype(BF16), b_d.reshape(1, -1), rows_all, tm, w_d.shape[-1])
        gates = _in_proj(hn, w_gate.astype(BF16), b_gate.reshape(1, -1), rows, tm, 1024, gates=True)

        hs = _mixers(p, pd, l, nb, t_lat, t_ctx, rows, weights)
        h = _merge_out(hs, gates, br_w.astype(BF16), out_w.astype(BF16), h, modp, 5, l, rows, tm, 512, grp)

        gact = _ffn_up(h, norm_g[l, 2], modp, 6, ffn_w1, ffn_w3, l, 1, rows, tm, tf, grp)
        h = _ffn_down(gact, ffn_w2[l, 1].astype(BF16), h, modp, 8, tm, 512, grp)

    out = _final_norm(h, final_g, tm_s)
    return out.reshape(nb, t_lat, d_model)
```
